```python
import jax, jax.numpy as jnp
from jax import lax
import numpy as np

D_MODEL = 2048
BATCH = 1
SEQ = 8192
DEPTH = 1

NORM_EPS = 1e-5
D_SSM = 2048
SSD_HEAD_DIM = 64
SSD_HEADS = D_SSM // SSD_HEAD_DIM
SSD_GROUPS = 4
SSD_HEADS_PER_GROUP = SSD_HEADS // SSD_GROUPS
SSD_STATE = 128
CONV_WIDTH = 4
CHUNK = 256
D_CONV_CH = D_SSM + 2 * SSD_GROUPS * SSD_STATE
D_POOL = 2048
POOL_WINDOWS = (2, 4, 8, 16)
POOL_GROUPS = len(POOL_WINDOWS)
POOL_GROUP_DIM = D_POOL // POOL_GROUPS
D_MIX = D_SSM + D_POOL
D_IN_PROJ = D_SSM + D_CONV_CH + SSD_HEADS + D_POOL
D_FF = -(-8 * D_MODEL // (3 * 256)) * 256

kernel_name = "hybrid_ssd_multiscale_pool_block"


def rms_norm(x, w):
    xf = x.astype(jnp.float32)
    y = xf * lax.rsqrt(jnp.mean(xf * xf, axis=-1, keepdims=True) + NORM_EPS)
    return (y * w.astype(jnp.float32)).astype(x.dtype)


def causal_depthwise_conv(u, w, b):
    ch = u.shape[-1]
    out = lax.conv_general_dilated(
        u, w.astype(u.dtype).reshape(CONV_WIDTH, 1, ch),
        window_strides=(1,), padding=[(CONV_WIDTH - 1, 0)],
        dimension_numbers=("NWC", "WIO", "NWC"), feature_group_count=ch)
    return out + b.astype(u.dtype)


def ssd_chunked_scan(xh, dt, a, b_mat, c_mat):
    bsz, seqlen = xh.shape[:2]
    pad = (-seqlen) % CHUNK
    if pad:
        padw = lambda t: jnp.pad(t, [(0, 0), (0, pad)] + [(0, 0)] * (t.ndim - 2))
        xh, dt, b_mat, c_mat = padw(xh), padw(dt), padw(b_mat), padw(c_mat)
    nc = (seqlen + pad) // CHUNK
    rs = lambda t: t.reshape((bsz, nc, CHUNK) + t.shape[2:])
    xh, dt, b_mat, c_mat = rs(xh), rs(dt), rs(b_mat), rs(c_mat)

    a_cum = jnp.cumsum(dt * a, axis=2)
    xdt = xh * dt[..., None]

    seg = a_cum[:, :, :, None] - a_cum[:, :, None, :]
    causal = jnp.tril(jnp.ones((CHUNK, CHUNK), dtype=bool))[:, :, None, None]
    decay = jnp.exp(jnp.where(causal, seg, -jnp.inf))
    cb = jnp.einsum("bclgn,bcsgn->bclsg", c_mat, b_mat)
    y_diag = jnp.einsum("bclsg,bclsgr,bcsgrp->bclgrp", cb, decay, xdt)

    decay_to_end = jnp.exp(a_cum[:, :, -1:] - a_cum)
    states = jnp.einsum("bclgn,bclgr,bclgrp->bcgrpn", b_mat, decay_to_end, xdt)
    chunk_decay = jnp.exp(a_cum[:, :, -1])

    def step(h, inp):
        s, dcy = inp
        return h * dcy[..., None, None] + s, h
    h0 = jnp.zeros(states.shape[:1] + states.shape[2:], jnp.float32)
    _, prev = lax.scan(step, h0, (jnp.moveaxis(states, 1, 0), jnp.moveaxis(chunk_decay, 1, 0)))
    prev = jnp.moveaxis(prev, 0, 1)

    y_off = jnp.einsum("bclgn,bcgrpn,bclgr->bclgrp", c_mat, prev, jnp.exp(a_cum))
    y = (y_diag + y_off).reshape((bsz, nc * CHUNK) + xh.shape[3:])
    return y[:, :seqlen]


def gated_group_rmsnorm(y, z, w):
    g = y * jax.nn.silu(z.astype(jnp.float32))
    shp = g.shape
    g = g.reshape(shp[:-1] + (SSD_GROUPS, shp[-1] // SSD_GROUPS))
    g = g * lax.rsqrt(jnp.mean(g * g, axis=-1, keepdims=True) + NORM_EPS)
    return g.reshape(shp) * w.astype(jnp.float32)


def multiscale_causal_pool(u):
    uf = u.astype(jnp.float32)
    seqlen = u.shape[1]
    cs = jnp.pad(jnp.cumsum(uf, axis=1), ((0, 0), (1, 0), (0, 0)))
    t = jnp.arange(seqlen)
    outs = []
    for gi, w in enumerate(POOL_WINDOWS):
        sl = slice(gi * POOL_GROUP_DIM, (gi + 1) * POOL_GROUP_DIM)
        csg = cs[..., sl]
        start = jnp.maximum(t + 1 - w, 0)
        win_sum = csg[:, 1:] - csg[:, start]
        count = jnp.minimum(t + 1, w).astype(jnp.float32)
        outs.append(win_sum / count[None, :, None] - uf[..., sl])
    return jnp.stack(outs, axis=2)


def hybrid_mixer(h, w_in, conv_w, conv_b, dt_bias, a_log, d_skip, ssd_norm_w,
                 pool_w, pool_scale, w_out):
    bsz, seqlen, _ = h.shape
    proj = h @ w_in.astype(h.dtype)
    z, xbc, dt_raw, u = jnp.split(
        proj, [D_SSM, D_SSM + D_CONV_CH, D_SSM + D_CONV_CH + SSD_HEADS], axis=-1)

    xbc = jax.nn.silu(causal_depthwise_conv(xbc, conv_w, conv_b)).astype(jnp.float32)
    xs, bm, cm = jnp.split(xbc, [D_SSM, D_SSM + SSD_GROUPS * SSD_STATE], axis=-1)
    dt = jax.nn.softplus(dt_raw.astype(jnp.float32) + dt_bias.astype(jnp.float32))
    a = -jnp.exp(a_log.astype(jnp.float32))
    xh = xs.reshape(bsz, seqlen, SSD_GROUPS, SSD_HEADS_PER_GROUP, SSD_HEAD_DIM)
    y = ssd_chunked_scan(
        xh,
        dt.reshape(bsz, seqlen, SSD_GROUPS, SSD_HEADS_PER_GROUP),
        a.reshape(SSD_GROUPS, SSD_HEADS_PER_GROUP),
        bm.reshape(bsz, seqlen, SSD_GROUPS, SSD_STATE),
        cm.reshape(bsz, seqlen, SSD_GROUPS, SSD_STATE))
    y = y + d_skip.astype(jnp.float32).reshape(SSD_GROUPS, SSD_HEADS_PER_GROUP)[..., None] * xh
    y_ssd = gated_group_rmsnorm(y.reshape(bsz, seqlen, D_SSM), z, ssd_norm_w)

    pooled = multiscale_causal_pool(u)
    y_pool = jnp.einsum("blgc,gcd->blgd", pooled, pool_w.astype(jnp.float32))
    y_pool = y_pool.reshape(bsz, seqlen, D_POOL) * pool_scale.astype(jnp.float32)

    mixed = jnp.concatenate([y_ssd, y_pool], axis=-1).astype(h.dtype)
    return mixed @ w_out.astype(h.dtype)


def swiglu(h, w_gate, w_up, w_down):
    return (jax.nn.silu(h @ w_gate.astype(h.dtype)) * (h @ w_up.astype(h.dtype))) @ w_down.astype(h.dtype)


def setup_inputs(seed: int = 0) -> dict:
    key = jax.random.key(seed)
    ks = jax.random.split(key, 20)
    f32 = jnp.float32
    nrm = lambda k, shp, s: jax.random.normal(k, shp, f32) * s
    dt_init = jnp.exp(jax.random.uniform(ks[5], (DEPTH, SSD_HEADS), f32,
                                         np.log(1e-3), np.log(1e-1)))
    return {
        "x": jax.random.normal(ks[0], (BATCH, SEQ, D_MODEL), f32),
        "attn_norm_w": 1.0 + nrm(ks[1], (DEPTH, D_MODEL), 0.02),
        "w_in": nrm(ks[2], (DEPTH, D_MODEL, D_IN_PROJ), D_MODEL ** -0.5),
        "conv_w": nrm(ks[3], (DEPTH, CONV_WIDTH, D_CONV_CH), CONV_WIDTH ** -0.5),
        "conv_b": nrm(ks[4], (DEPTH, D_CONV_CH), 0.02),
        "dt_bias": dt_init + jnp.log(-jnp.expm1(-dt_init)),
        "a_log": jnp.log(jax.random.uniform(ks[6], (DEPTH, SSD_HEADS), f32, 1.0, 16.0)),
        "d_skip": 1.0 + nrm(ks[7], (DEPTH, SSD_HEADS), 0.02),
        "ssd_norm_w": 1.0 + nrm(ks[8], (DEPTH, D_SSM), 0.02),
        "pool_w": nrm(ks[9], (DEPTH, POOL_GROUPS, POOL_GROUP_DIM, POOL_GROUP_DIM), POOL_GROUP_DIM ** -0.5),
        "pool_scale": 1.0 + nrm(ks[10], (DEPTH, D_POOL), 0.02),
        "w_out": nrm(ks[11], (DEPTH, D_MIX, D_MODEL), D_MIX ** -0.5),
        "ffn_norm_w": 1.0 + nrm(ks[12], (DEPTH, D_MODEL), 0.02),
        "w_gate": nrm(ks[13], (DEPTH, D_MODEL, D_FF), D_MODEL ** -0.5),
        "w_up": nrm(ks[14], (DEPTH, D_MODEL, D_FF), D_MODEL ** -0.5),
        "w_down": nrm(ks[15], (DEPTH, D_FF, D_MODEL), D_FF ** -0.5),
        "final_norm_w": 1.0 + nrm(ks[16], (D_MODEL,), 0.02),
    }


def reference(x, attn_norm_w, w_in, conv_w, conv_b, dt_bias, a_log, d_skip,
              ssd_norm_w, pool_w, pool_scale, w_out, ffn_norm_w, w_gate, w_up,
              w_down, final_norm_w):
    h = x
    for i in range(DEPTH):
        h = h + hybrid_mixer(rms_norm(h, attn_norm_w[i]), w_in[i], conv_w[i], conv_b[i],
                             dt_bias[i], a_log[i], d_skip[i], ssd_norm_w[i],
                             pool_w[i], pool_scale[i], w_out[i])
        h = h + swiglu(rms_norm(h, ffn_norm_w[i]), w_gate[i], w_up[i], w_down[i])
    return rms_norm(h, final_norm_w)
```

```python
import functools

import jax
import jax.numpy as jnp
from jax import lax
from jax.experimental import pallas as pl
from jax.experimental.pallas import tpu as pltpu

F32 = jnp.float32
BF16 = jnp.bfloat16

NORM_EPS = 1e-5
D_MODEL = 2048
D_SSM = 2048
HEAD_DIM = 64
N_HEADS = D_SSM // HEAD_DIM
N_GROUPS = 4
HEADS_PER_GROUP = N_HEADS // N_GROUPS
GROUP_WIDTH = D_SSM // N_GROUPS
D_STATE = 128
CONV_WIDTH = 4
D_POOL = 2048
POOL_WINDOWS = (2, 4, 8, 16)
POOL_GROUP_DIM = D_POOL // len(POOL_WINDOWS)
D_MIX = D_SSM + D_POOL

V7X_LANES = 128
V7X_SUBLANES = 8
V7X_VMEM_LIMIT_BYTES = 56 * 1024 * 1024

COL_Z = 0
COL_U = COL_Z + D_SSM
COL_XS = COL_U + D_POOL
COL_B = COL_XS + D_SSM
COL_C = COL_B + N_GROUPS * D_STATE
COL_DT = COL_C + N_GROUPS * D_STATE
D_PROJ = COL_DT + N_GROUPS * V7X_LANES

SSD_CHUNK = 256
HALO_ROWS = V7X_SUBLANES
POOL_HALO = 16


def _sigmoid(x):
    return 1.0 / (1.0 + jnp.exp(-x))


def _silu(x):
    return x * _sigmoid(x)


def _rms_normalize(x, w):
    ms = jnp.mean(x * x, axis=-1, keepdims=True)
    return x * lax.rsqrt(ms + NORM_EPS) * w


def _split3(x):
    hi = x.astype(BF16)
    r1 = x - hi.astype(F32)
    mid = r1.astype(BF16)
    lo = (r1 - mid.astype(F32)).astype(BF16)
    return hi, mid, lo


def _dot(a, b):
    return jnp.dot(a, b, preferred_element_type=F32)


def _dot3(x, rhs_bf16):
    hi, mid, lo = _split3(x)
    return _dot(hi, rhs_bf16) + _dot(mid, rhs_bf16) + _dot(lo, rhs_bf16)


def _dot3_left(lhs_bf16, x):
    hi, mid, lo = _split3(x)
    return _dot(lhs_bf16, hi) + _dot(lhs_bf16, mid) + _dot(lhs_bf16, lo)


def _norm_matmul_kernel(x_ref, nw_ref, w_ref, o_ref, hn_ref):
    @pl.when(pl.program_id(1) == 0)
    def _():
        hn_ref[...] = _rms_normalize(x_ref[...], nw_ref[...]).astype(BF16)

    o_ref[...] = _dot(hn_ref[...], w_ref[...])


def _norm_matmul(x, nw, w, *, tm, tn):
    m, k = x.shape
    n = w.shape[1]
    return pl.pallas_call(
        _norm_matmul_kernel,
        out_shape=jax.ShapeDtypeStruct((m, n), F32),
        grid=(m // tm, n // tn),
        in_specs=[
            pl.BlockSpec((tm, k), lambda i, j: (i, 0)),
            pl.BlockSpec((1, k), lambda i, j: (0, 0)),
            pl.BlockSpec((k, tn), lambda i, j: (0, j)),
        ],
        out_specs=pl.BlockSpec((tm, tn), lambda i, j: (i, j)),
        scratch_shapes=[pltpu.VMEM((tm, k), BF16)],
        compiler_params=pltpu.CompilerParams(
            dimension_semantics=("parallel", "arbitrary"),
            vmem_limit_bytes=V7X_VMEM_LIMIT_BYTES),
        name="in_proj",
    )(x, nw, w)


def _norm_swiglu_kernel(x_ref, nw_ref, wg_ref, wu_ref, o_ref, hn_ref):
    @pl.when(pl.program_id(1) == 0)
    def _():
        hn_ref[...] = _rms_normalize(x_ref[...], nw_ref[...]).astype(BF16)

    hn = hn_ref[...]
    gate = _dot(hn, wg_ref[...])
    up = _dot(hn, wu_ref[...])
    o_ref[...] = (_silu(gate) * up).astype(BF16)


def _norm_swiglu(x, nw, wg, wu, *, tm, tn):
    m, k = x.shape
    n = wg.shape[1]
    return pl.pallas_call(
        _norm_swiglu_kernel,
        out_shape=jax.ShapeDtypeStruct((m, n), BF16),
        grid=(m // tm, n // tn),
        in_specs=[
            pl.BlockSpec((tm, k), lambda i, j: (i, 0)),
            pl.BlockSpec((1, k), lambda i, j: (0, 0)),
            pl.BlockSpec((k, tn), lambda i, j: (0, j)),
            pl.BlockSpec((k, tn), lambda i, j: (0, j)),
        ],
        out_specs=pl.BlockSpec((tm, tn), lambda i, j: (i, j)),
        scratch_shapes=[pltpu.VMEM((tm, k), BF16)],
        compiler_params=pltpu.CompilerParams(
            dimension_semantics=("parallel", "arbitrary"),
            vmem_limit_bytes=V7X_VMEM_LIMIT_BYTES),
        name="ffn_up",
    )(x, nw, wg, wu)


def _out_proj_kernel(ys_ref, yp_ref, ws_ref, wp_ref, x_ref, o_ref):
    o_ref[...] = x_ref[...] + _dot(ys_ref[...], ws_ref[...]) + _dot(yp_ref[...], wp_ref[...])


def _out_proj(ys, yp, ws, wp, x, *, tm, tn):
    m, k = ys.shape
    n = ws.shape[1]
    return pl.pallas_call(
        _out_proj_kernel,
        out_shape=jax.ShapeDtypeStruct((m, n), F32),
        grid=(m // tm, n // tn),
        in_specs=[
            pl.BlockSpec((tm, k), lambda i, j: (i, 0)),
            pl.BlockSpec((tm, k), lambda i, j: (i, 0)),
            pl.BlockSpec((k, tn), lambda i, j: (0, j)),
            pl.BlockSpec((k, tn), lambda i, j: (0, j)),
            pl.BlockSpec((tm, tn), lambda i, j: (i, j)),
        ],
        out_specs=pl.BlockSpec((tm, tn), lambda i, j: (i, j)),
        compiler_params=pltpu.CompilerParams(
            dimension_semantics=("parallel", "parallel"),
            vmem_limit_bytes=V7X_VMEM_LIMIT_BYTES),
        name="out_proj",
    )(ys, yp, ws, wp, x)


def _down_kernel(a_ref, w_ref, h_ref, nw_ref, o_ref, *, nk, final_norm):
    kk = pl.program_id(1)

    @pl.when(kk == 0)
    def _():
        o_ref[...] = h_ref[...]

    o_ref[...] += _dot(a_ref[...], w_ref[...])

    if final_norm:
        @pl.when(kk == nk - 1)
        def _():
            o_ref[...] = _rms_normalize(o_ref[...], nw_ref[...])


def _down_proj(act, w, h, nw, *, tm, tk, final_norm):
    m, k = act.shape
    n = w.shape[1]
    nk = k // tk
    return pl.pallas_call(
        functools.partial(_down_kernel, nk=nk, final_norm=final_norm),
        out_shape=jax.ShapeDtypeStruct((m, n), F32),
        grid=(m // tm, nk),
        in_specs=[
            pl.BlockSpec((tm, tk), lambda i, kk: (i, kk)),
            pl.BlockSpec((tk, n), lambda i, kk: (kk, 0)),
            pl.BlockSpec((tm, n), lambda i, kk: (i, 0)),
            pl.BlockSpec((1, n), lambda i, kk: (0, 0)),
        ],
        out_specs=pl.BlockSpec((tm, n), lambda i, kk: (i, 0)),
        compiler_params=pltpu.CompilerParams(
            dimension_semantics=("parallel", "arbitrary"),
            vmem_limit_bytes=V7X_VMEM_LIMIT_BYTES),
        name="ffn_down",
    )(act, w, h, nw)


def _pool_kernel(u_ref, halo_ref, pw_ref, ps_ref, o_ref, *, tl):
    i = pl.program_id(0)
    t = i * tl + lax.broadcasted_iota(jnp.int32, (tl, POOL_GROUP_DIM), 0)
    for g, window in enumerate(POOL_WINDOWS):
        cols = slice(g * POOL_GROUP_DIM, (g + 1) * POOL_GROUP_DIM)
        u = u_ref[:, cols]
        halo = jnp.where(i == 0, 0.0, halo_ref[:, cols])
        s = jnp.concatenate([halo, u], axis=0)
        shift = 1
        while shift < window:
            s = s + pltpu.roll(s, shift, axis=0)
            shift *= 2
        count = jnp.minimum(t + 1, window).astype(F32)
        pooled = s[POOL_HALO:] / count - u
        y = _dot(pooled.astype(BF16), pw_ref[g]) * ps_ref[:, cols]
        o_ref[:, cols] = y.astype(BF16)


def _pool_mixer(proj, pool_w, pool_scale, *, tl):
    m = proj.shape[0]
    assert COL_U % D_POOL == 0 and tl % POOL_HALO == 0
    u_block = COL_U // D_POOL
    halo_blocks_per_tile = tl // POOL_HALO
    return pl.pallas_call(
        functools.partial(_pool_kernel, tl=tl),
        out_shape=jax.ShapeDtypeStruct((m, D_POOL), BF16),
        grid=(m // tl,),
        in_specs=[
            pl.BlockSpec((tl, D_POOL), lambda i: (i, u_block)),
            pl.BlockSpec((POOL_HALO, D_POOL),
                         lambda i: (jnp.maximum(i * halo_blocks_per_tile - 1, 0), u_block)),
            pl.BlockSpec((len(POOL_WINDOWS), POOL_GROUP_DIM, POOL_GROUP_DIM), lambda i: (0, 0, 0)),
            pl.BlockSpec((1, D_POOL), lambda i: (0, 0)),
        ],
        out_specs=pl.BlockSpec((tl, D_POOL), lambda i: (i, 0)),
        compiler_params=pltpu.CompilerParams(
            dimension_semantics=("parallel",),
            vmem_limit_bytes=V7X_VMEM_LIMIT_BYTES),
        name="pool_mixer",
    )(proj, proj, pool_w, pool_scale)


def _causal_conv_silu(raw_ref, halo_ref, w_ref, b_ref):
    cur = raw_ref[...]
    ext = jnp.concatenate([halo_ref[...], cur], axis=0)
    acc = cur * w_ref[CONV_WIDTH - 1:CONV_WIDTH, :] + b_ref[...]
    for back in range(1, CONV_WIDTH):
        tap = w_ref[CONV_WIDTH - 1 - back:CONV_WIDTH - back, :]
        acc = acc + pltpu.roll(ext, back, axis=0)[HALO_ROWS:] * tap
    halo_ref[...] = cur[cur.shape[0] - HALO_ROWS:]
    return _silu(acc)


def _ssd_kernel(z_ref, xs_ref, b_ref, c_ref, dt_ref,
                cwx_ref, cwb_ref, cwc_ref, cbx_ref, cbb_ref, cbc_ref,
                dtb_ref, alog_ref, dskip_ref, nw_ref,
                o_ref,
                state_ref, hx_ref, hb_ref, hc_ref):
    q = SSD_CHUNK

    @pl.when(pl.program_id(1) == 0)
    def _():
        state_ref[...] = jnp.zeros_like(state_ref)
        hx_ref[...] = jnp.zeros_like(hx_ref)
        hb_ref[...] = jnp.zeros_like(hb_ref)
        hc_ref[...] = jnp.zeros_like(hc_ref)

    xs = _causal_conv_silu(xs_ref, hx_ref, cwx_ref, cbx_ref)
    bm = _causal_conv_silu(b_ref, hb_ref, cwb_ref, cbb_ref)
    cm = _causal_conv_silu(c_ref, hc_ref, cwc_ref, cbc_ref)

    x = dt_ref[...] + dtb_ref[...]
    dt = jnp.maximum(x, 0.0) + jnp.log1p(jnp.exp(-jnp.abs(x)))
    a = -jnp.exp(alog_ref[...])

    row_i = lax.broadcasted_iota(jnp.int32, (q, q), 0)
    col_i = lax.broadcasted_iota(jnp.int32, (q, q), 1)
    causal = row_i >= col_i
    tril = jnp.where(causal, 1.0, 0.0).astype(BF16)
    a_cum = _dot3_left(tril, dt * a)
    a_cum_t = a_cum.T
    a_last = a_cum[q - 1:q, :]

    e_row = lax.broadcasted_iota(jnp.int32, (V7X_LANES, GROUP_WIDTH), 0)
    e_col = lax.broadcasted_iota(jnp.int32, (V7X_LANES, GROUP_WIDTH), 1)
    expand = jnp.where(e_col // HEAD_DIM == e_row, 1.0, 0.0).astype(BF16)
    dt_b = _dot3(dt, expand)
    decay_in_b = _dot3(jnp.exp(a_cum), expand)
    decay_out_b = _dot3(jnp.exp(a_last - a_cum), expand)

    xdt = xs * dt_b
    bm_bf = bm.astype(BF16)
    cm_bf = cm.astype(BF16)

    cb = lax.dot_general(cm_bf, bm_bf, (((1,), (1,)), ((), ())), preferred_element_type=F32)

    lane = lax.broadcasted_iota(jnp.int32, (1, GROUP_WIDTH // 2), 1)
    halves = []
    for half in range(2):
        xdt_half = xdt[:, half * 256:(half + 1) * 256]
        acc = jnp.zeros((q, 256), F32)
        for rr in range(4):
            r = half * 4 + rr
            seg = a_cum[:, r:r + 1] - a_cum_t[r:r + 1, :]
            decay = jnp.exp(jnp.where(causal, seg, -jnp.inf))
            m_r = (cb * decay).astype(BF16)
            head_cols = (lane // HEAD_DIM) == rr
            acc = acc + _dot(m_r, jnp.where(head_cols, xdt_half, 0.0).astype(BF16))
        halves.append(acc)
    y_diag = jnp.concatenate(halves, axis=-1)

    state = state_ref[...]
    y_off = _dot(cm_bf, state.astype(BF16)) * decay_in_b
    xw = (xdt * decay_out_b).astype(BF16)
    new_state = lax.dot_general(bm_bf, xw, (((0,), (0,)), ((), ())), preferred_element_type=F32)
    state_ref[...] = state * decay_in_b[q - 1:q, :] + new_state

    y = y_diag + y_off + dskip_ref[...] * xs
    g = y * _silu(z_ref[...])
    o_ref[...] = _rms_normalize(g, nw_ref[...]).astype(BF16)


def _ssd_mixer(proj, conv_w, conv_b, dtb_c, alog_c, dskip_full, norm_w):
    m = proj.shape[0]
    q = SSD_CHUNK
    gw = GROUP_WIDTH
    ds = D_STATE
    col = lambda start, width: start // width
    row_spec = lambda width, block0: pl.BlockSpec((q, width), lambda g, c: (c, block0 + g))
    par_spec = lambda rows, width, block0: pl.BlockSpec((rows, width), lambda g, c: (0, block0 + g))
    return pl.pallas_call(
        _ssd_kernel,
        out_shape=jax.ShapeDtypeStruct((m, D_SSM), BF16),
        grid=(N_GROUPS, m // q),
        in_specs=[
            row_spec(gw, col(COL_Z, gw)),
            row_spec(gw, col(COL_XS, gw)),
            row_spec(ds, col(COL_B, ds)),
            row_spec(ds, col(COL_C, ds)),
            row_spec(V7X_LANES, col(COL_DT, V7X_LANES)),
            par_spec(CONV_WIDTH, gw, 0),
            par_spec(CONV_WIDTH, ds, col(D_SSM, ds)),
            par_spec(CONV_WIDTH, ds, col(D_SSM + N_GROUPS * ds, ds)),
            par_spec(1, gw, 0),
            par_spec(1, ds, col(D_SSM, ds)),
            par_spec(1, ds, col(D_SSM + N_GROUPS * ds, ds)),
            par_spec(1, V7X_LANES, 0),
            par_spec(1, V7X_LANES, 0),
            par_spec(1, gw, 0),
            par_spec(1, gw, 0),
        ],
        out_specs=pl.BlockSpec((q, gw), lambda g, c: (c, g)),
        scratch_shapes=[
            pltpu.VMEM((ds, gw), F32),
            pltpu.VMEM((HALO_ROWS, gw), F32),
            pltpu.VMEM((HALO_ROWS, ds), F32),
            pltpu.VMEM((HALO_ROWS, ds), F32),
        ],
        compiler_params=pltpu.CompilerParams(
            dimension_semantics=("parallel", "arbitrary"),
            vmem_limit_bytes=V7X_VMEM_LIMIT_BYTES),
        name="ssd_mixer",
    )(proj, proj, proj, proj, proj,
      conv_w, conv_w, conv_w, conv_b, conv_b, conv_b,
      dtb_c, alog_c, dskip_full, norm_w)


def _pad_heads_to_lanes(v):
    v = v.reshape(N_GROUPS, HEADS_PER_GROUP)
    v = jnp.pad(v, ((0, 0), (0, V7X_LANES - HEADS_PER_GROUP)))
    return v.reshape(1, N_GROUPS * V7X_LANES)


def kernel(x, attn_norm_w, w_in, conv_w, conv_b, dt_bias, a_log, d_skip, ssd_norm_w, pool_w,
           pool_scale, w_out, ffn_norm_w, w_gate, w_up, w_down, final_norm_w):
    bsz, seqlen, d_model = x.shape
    depth = w_in.shape[0]
    h = x.reshape(bsz * seqlen, d_model)
    assert bsz == 1 and seqlen % 1024 == 0 and d_model == D_MODEL

    for i in range(depth):
        w = w_in[i]
        w_z = w[:, :D_SSM]
        w_xbc = w[:, D_SSM:D_SSM + D_SSM + 2 * N_GROUPS * D_STATE]
        dt0 = D_SSM + D_SSM + 2 * N_GROUPS * D_STATE
        w_dt = w[:, dt0:dt0 + N_HEADS].reshape(d_model, N_GROUPS, HEADS_PER_GROUP)
        w_dt = jnp.pad(w_dt, ((0, 0), (0, 0), (0, V7X_LANES - HEADS_PER_GROUP)))
        w_dt = w_dt.reshape(d_model, N_GROUPS * V7X_LANES)
        w_u = w[:, dt0 + N_HEADS:]
        w_cat = jnp.concatenate([w_z, w_u, w_xbc, w_dt], axis=1).astype(BF16)
        assert w_cat.shape[1] == D_PROJ

        proj = _norm_matmul(h, attn_norm_w[i][None, :], w_cat, tm=1024, tn=512)

        y_ssd = _ssd_mixer(
            proj, conv_w[i], conv_b[i][None, :],
            _pad_heads_to_lanes(dt_bias[i]), _pad_heads_to_lanes(a_log[i]),
            jnp.repeat(d_skip[i], HEAD_DIM)[None, :], ssd_norm_w[i][None, :])
        y_pool = _pool_mixer(proj, pool_w[i].astype(BF16), pool_scale[i][None, :], tl=512)

        w_o = w_out[i].astype(BF16)
        h = _out_proj(y_ssd, y_pool, w_o[:D_SSM], w_o[D_SSM:], h, tm=1024, tn=512)

        act = _norm_swiglu(h, ffn_norm_w[i][None, :], w_gate[i].astype(BF16),
                           w_up[i].astype(BF16), tm=1024, tn=512)
        last = i == depth - 1
        h = _down_proj(act, w_down[i].astype(BF16), h, final_norm_w[None, :],
                       tm=1024, tk=512, final_norm=last)

    return h.reshape(bsz, seqlen, d_model)
```

```python
import functools

import jax
import jax.numpy as jnp
from jax import lax
from jax.experimental import pallas as pl
from jax.experimental.pallas import tpu as pltpu

F32 = jnp.float32
BF16 = jnp.bfloat16

NORM_EPS = 1e-5
D_MODEL = 2048
D_SSM = 2048
HEAD_DIM = 64
N_HEADS = D_SSM // HEAD_DIM
N_GROUPS = 4
HEADS_PER_GROUP = N_HEADS // N_GROUPS
GROUP_WIDTH = D_SSM // N_GROUPS
D_STATE = 128
CONV_WIDTH = 4
D_POOL = 2048
POOL_WINDOWS = (2, 4, 8, 16)
POOL_GROUP_DIM = D_POOL // len(POOL_WINDOWS)
D_MIX = D_SSM + D_POOL

V7X_LANES = 128
V7X_SUBLANES = 8
V7X_VMEM_LIMIT_BYTES = 56 * 1024 * 1024

COL_Z = 0
COL_U = COL_Z + D_SSM
COL_XS = COL_U + D_POOL
COL_B = COL_XS + D_SSM
COL_C = COL_B + N_GROUPS * D_STATE
COL_DT = COL_C + N_GROUPS * D_STATE
D_PROJ = COL_DT + N_GROUPS * V7X_LANES

SSD_CHUNK = 256
HALO_ROWS = V7X_SUBLANES
POOL_HALO = 16


def _sigmoid(x):
    return 1.0 / (1.0 + jnp.exp(-x))


def _silu(x):
    return x * _sigmoid(x)


def _rms_normalize(x, w):
    ms = jnp.mean(x * x, axis=-1, keepdims=True)
    return x * lax.rsqrt(ms + NORM_EPS) * w


def _split3(x):
    hi = x.astype(BF16)
    r1 = x - hi.astype(F32)
    mid = r1.astype(BF16)
    lo = (r1 - mid.astype(F32)).astype(BF16)
    return hi, mid, lo


def _dot(a, b):
    return jnp.dot(a, b, preferred_element_type=F32)


def _dot3(x, rhs_bf16):
    hi, mid, lo = _split3(x)
    return _dot(hi, rhs_bf16) + _dot(mid, rhs_bf16) + _dot(lo, rhs_bf16)


def _dot3_left(lhs_bf16, x):
    hi, mid, lo = _split3(x)
    return _dot(lhs_bf16, hi) + _dot(lhs_bf16, mid) + _dot(lhs_bf16, lo)


SRC_XBC = D_SSM
SRC_DT = SRC_XBC + D_SSM + 2 * N_GROUPS * D_STATE
SRC_U = SRC_DT + N_HEADS
D_IN_PROJ = SRC_U + D_POOL


def _w_in_prep_kernel(w_ref, o_ref):
    o_ref[:, COL_Z:COL_Z + D_SSM] = w_ref[:, :D_SSM].astype(BF16)
    o_ref[:, COL_U:COL_U + D_POOL] = w_ref[:, SRC_U:SRC_U + D_POOL].astype(BF16)
    o_ref[:, COL_XS:COL_DT] = w_ref[:, SRC_XBC:SRC_DT].astype(BF16)
    window = w_ref[:, SRC_DT:SRC_DT + V7X_LANES]
    lane = lax.broadcasted_iota(jnp.int32, (1, V7X_LANES), 1)
    for g in range(N_GROUPS):
        shift = (V7X_LANES - HEADS_PER_GROUP * g) % V7X_LANES
        block = pltpu.roll(window, shift, axis=1) if shift else window
        o_ref[:, COL_DT + g * V7X_LANES:COL_DT + (g + 1) * V7X_LANES] = (
            jnp.where(lane < HEADS_PER_GROUP, block, 0.0).astype(BF16))


def _w_in_prep(w, *, tk):
    k, n = w.shape
    assert n == D_IN_PROJ
    return pl.pallas_call(
        _w_in_prep_kernel,
        out_shape=jax.ShapeDtypeStruct((k, D_PROJ), BF16),
        grid=(k // tk,),
        in_specs=[pl.BlockSpec((tk, n), lambda i: (i, 0))],
        out_specs=pl.BlockSpec((tk, D_PROJ), lambda i: (i, 0)),
        compiler_params=pltpu.CompilerParams(
            dimension_semantics=("parallel",),
            vmem_limit_bytes=V7X_VMEM_LIMIT_BYTES),
        name="w_in_prep",
    )(w)


def _norm_matmul_kernel(x_ref, nw_ref, w_ref, o_ref, hn_ref):
    @pl.when(pl.program_id(1) == 0)
    def _():
        hn_ref[...] = _rms_normalize(x_ref[...], nw_ref[...]).astype(BF16)

    o_ref[...] = _dot(hn_ref[...], w_ref[...])


def _norm_matmul(x, nw, w, *, tm, tn):
    m, k = x.shape
    n = w.shape[1]
    return pl.pallas_call(
        _norm_matmul_kernel,
        out_shape=jax.ShapeDtypeStruct((m, n), F32),
        grid=(m // tm, n // tn),
        in_specs=[
            pl.BlockSpec((tm, k), lambda i, j: (i, 0)),
            pl.BlockSpec((1, k), lambda i, j: (0, 0)),
            pl.BlockSpec((k, tn), lambda i, j: (0, j)),
        ],
        out_specs=pl.BlockSpec((tm, tn), lambda i, j: (i, j)),
        scratch_shapes=[pltpu.VMEM((tm, k), BF16)],
        compiler_params=pltpu.CompilerParams(
            dimension_semantics=("parallel", "arbitrary"),
            vmem_limit_bytes=V7X_VMEM_LIMIT_BYTES),
        name="in_proj",
    )(x, nw, w)


def _norm_swiglu_kernel(x_ref, nw_ref, wg_ref, wu_ref, o_ref, hn_ref):
    @pl.when(pl.program_id(1) == 0)
    def _():
        hn_ref[...] = _rms_normalize(x_ref[...], nw_ref[...]).astype(BF16)

    hn = hn_ref[...]
    gate = _dot(hn, wg_ref[...].astype(BF16))
    up = _dot(hn, wu_ref[...].astype(BF16))
    o_ref[...] = (_silu(gate) * up).astype(BF16)


def _norm_swiglu(x, nw, wg, wu, *, tm, tn):
    m, k = x.shape
    n = wg.shape[1]
    return pl.pallas_call(
        _norm_swiglu_kernel,
        out_shape=jax.ShapeDtypeStruct((m, n), BF16),
        grid=(m // tm, n // tn),
        in_specs=[
            pl.BlockSpec((tm, k), lambda i, j: (i, 0)),
            pl.BlockSpec((1, k), lambda i, j: (0, 0)),
            pl.BlockSpec((k, tn), lambda i, j: (0, j)),
            pl.BlockSpec((k, tn), lambda i, j: (0, j)),
        ],
        out_specs=pl.BlockSpec((tm, tn), lambda i, j: (i, j)),
        scratch_shapes=[pltpu.VMEM((tm, k), BF16)],
        compiler_params=pltpu.CompilerParams(
            dimension_semantics=("parallel", "arbitrary"),
            vmem_limit_bytes=V7X_VMEM_LIMIT_BYTES),
        name="ffn_up",
    )(x, nw, wg, wu)


def _out_proj_kernel(ys_ref, yp_ref, ws_ref, wp_ref, x_ref, o_ref):
    o_ref[...] = (x_ref[...] + _dot(ys_ref[...], ws_ref[...].astype(BF16))
                  + _dot(yp_ref[...], wp_ref[...].astype(BF16)))


def _out_proj(ys, yp, w, x, *, tm, tn):
    m, k = ys.shape
    n = w.shape[1]
    return pl.pallas_call(
        _out_proj_kernel,
        out_shape=jax.ShapeDtypeStruct((m, n), F32),
        grid=(m // tm, n // tn),
        in_specs=[
            pl.BlockSpec((tm, k), lambda i, j: (i, 0)),
            pl.BlockSpec((tm, k), lambda i, j: (i, 0)),
            pl.BlockSpec((k, tn), lambda i, j: (0, j)),
            pl.BlockSpec((k, tn), lambda i, j: (1, j)),
            pl.BlockSpec((tm, tn), lambda i, j: (i, j)),
        ],
        out_specs=pl.BlockSpec((tm, tn), lambda i, j: (i, j)),
        compiler_params=pltpu.CompilerParams(
            dimension_semantics=("parallel", "parallel"),
            vmem_limit_bytes=V7X_VMEM_LIMIT_BYTES),
        name="out_proj",
    )(ys, yp, w, w, x)


def _down_kernel(a_ref, w_ref, h_ref, nw_ref, o_ref, *, nk, final_norm):
    kk = pl.program_id(1)

    @pl.when(kk == 0)
    def _():
        o_ref[...] = h_ref[...]

    o_ref[...] += _dot(a_ref[...], w_ref[...].astype(BF16))

    if final_norm:
        @pl.when(kk == nk - 1)
        def _():
            o_ref[...] = _rms_normalize(o_ref[...], nw_ref[...])


def _down_proj(act, w, h, nw, *, tm, tk, final_norm):
    m, k = act.shape
    n = w.shape[1]
    nk = k // tk
    return pl.pallas_call(
        functools.partial(_down_kernel, nk=nk, final_norm=final_norm),
        out_shape=jax.ShapeDtypeStruct((m, n), F32),
        grid=(m // tm, nk),
        in_specs=[
            pl.BlockSpec((tm, tk), lambda i, kk: (i, kk)),
            pl.BlockSpec((tk, n), lambda i, kk: (kk, 0)),
            pl.BlockSpec((tm, n), lambda i, kk: (i, 0)),
            pl.BlockSpec((1, n), lambda i, kk: (0, 0)),
        ],
        out_specs=pl.BlockSpec((tm, n), lambda i, kk: (i, 0)),
        compiler_params=pltpu.CompilerParams(
            dimension_semantics=("parallel", "arbitrary"),
            vmem_limit_bytes=V7X_VMEM_LIMIT_BYTES),
        name="ffn_down",
    )(act, w, h, nw)


def _pool_kernel(u_ref, halo_ref, pw_ref, ps_ref, o_ref, *, tl):
    i = pl.program_id(0)
    t = i * tl + lax.broadcasted_iota(jnp.int32, (tl, POOL_GROUP_DIM), 0)
    for g, window in enumerate(POOL_WINDOWS):
        cols = slice(g * POOL_GROUP_DIM, (g + 1) * POOL_GROUP_DIM)
        u = u_ref[:, cols]
        halo = jnp.where(i == 0, 0.0, halo_ref[:, cols])
        s = jnp.concatenate([halo, u], axis=0)
        shift = 1
        while shift < window:
            s = s + pltpu.roll(s, shift, axis=0)
            shift *= 2
        count = jnp.minimum(t + 1, window).astype(F32)
        pooled = s[POOL_HALO:] / count - u
        y = _dot(pooled.astype(BF16), pw_ref[g].astype(BF16)) * ps_ref[:, cols]
        o_ref[:, cols] = y.astype(BF16)


def _pool_mixer(proj, pool_w, pool_scale, *, tl):
    m = proj.shape[0]
    assert COL_U % D_POOL == 0 and tl % POOL_HALO == 0
    u_block = COL_U // D_POOL
    halo_blocks_per_tile = tl // POOL_HALO
    return pl.pallas_call(
        functools.partial(_pool_kernel, tl=tl),
        out_shape=jax.ShapeDtypeStruct((m, D_POOL), BF16),
        grid=(m // tl,),
        in_specs=[
            pl.BlockSpec((tl, D_POOL), lambda i: (i, u_block)),
            pl.BlockSpec((POOL_HALO, D_POOL),
                         lambda i: (jnp.maximum(i * halo_blocks_per_tile - 1, 0), u_block)),
            pl.BlockSpec((len(POOL_WINDOWS), POOL_GROUP_DIM, POOL_GROUP_DIM), lambda i: (0, 0, 0)),
            pl.BlockSpec((1, D_POOL), lambda i: (0, 0)),
        ],
        out_specs=pl.BlockSpec((tl, D_POOL), lambda i: (i, 0)),
        compiler_params=pltpu.CompilerParams(
            dimension_semantics=("parallel",),
            vmem_limit_bytes=V7X_VMEM_LIMIT_BYTES),
        name="pool_mixer",
    )(proj, proj, pool_w, pool_scale)


def _causal_conv_silu(raw_ref, halo_ref, w_ref, b_ref):
    cur = raw_ref[...]
    ext = jnp.concatenate([halo_ref[...], cur], axis=0)
    acc = cur * w_ref[CONV_WIDTH - 1:CONV_WIDTH, :] + b_ref[...]
    for back in range(1, CONV_WIDTH):
        tap = w_ref[CONV_WIDTH - 1 - back:CONV_WIDTH - back, :]
        acc = acc + pltpu.roll(ext, back, axis=0)[HALO_ROWS:] * tap
    halo_ref[...] = cur[cur.shape[0] - HALO_ROWS:]
    return _silu(acc)


def _ssd_kernel(z_ref, xs_ref, b_ref, c_ref, dt_ref,
                cwx_ref, cwb_ref, cwc_ref, cbx_ref, cbb_ref, cbc_ref,
                dtb_ref, alog_ref, dskip_ref, nw_ref,
                o_ref,
                state_ref, hx_ref, hb_ref, hc_ref):
    q = SSD_CHUNK

    @pl.when(pl.program_id(1) == 0)
    def _():
        state_ref[...] = jnp.zeros_like(state_ref)
        hx_ref[...] = jnp.zeros_like(hx_ref)
        hb_ref[...] = jnp.zeros_like(hb_ref)
        hc_ref[...] = jnp.zeros_like(hc_ref)

    xs = _causal_conv_silu(xs_ref, hx_ref, cwx_ref, cbx_ref)
    bm = _causal_conv_silu(b_ref, hb_ref, cwb_ref, cbb_ref)
    cm = _causal_conv_silu(c_ref, hc_ref, cwc_ref, cbc_ref)

    x = dt_ref[...] + dtb_ref[...]
    dt = jnp.maximum(x, 0.0) + jnp.log1p(jnp.exp(-jnp.abs(x)))
    a = -jnp.exp(alog_ref[...])

    row_i = lax.broadcasted_iota(jnp.int32, (q, q), 0)
    col_i = lax.broadcasted_iota(jnp.int32, (q, q), 1)
    causal = row_i >= col_i
    tril = jnp.where(causal, 1.0, 0.0).astype(BF16)
    a_cum = _dot3_left(tril, dt * a)
    a_cum_t = a_cum.T
    a_last = a_cum[q - 1:q, :]

    e_row = lax.broadcasted_iota(jnp.int32, (V7X_LANES, GROUP_WIDTH), 0)
    e_col = lax.broadcasted_iota(jnp.int32, (V7X_LANES, GROUP_WIDTH), 1)
    expand = jnp.where(e_col // HEAD_DIM == e_row, 1.0, 0.0).astype(BF16)
    dt_b = _dot3(dt, expand)
    decay_in_b = _dot3(jnp.exp(a_cum), expand)
    decay_out_b = _dot3(jnp.exp(a_last - a_cum), expand)

    xdt = xs * dt_b
    bm_bf = bm.astype(BF16)
    cm_bf = cm.astype(BF16)

    cb = lax.dot_general(cm_bf, bm_bf, (((1,), (1,)), ((), ())), preferred_element_type=F32)

    lane = lax.broadcasted_iota(jnp.int32, (1, GROUP_WIDTH // 2), 1)
    halves = []
    for half in range(2):
        xdt_half = xdt[:, half * 256:(half + 1) * 256]
        acc = jnp.zeros((q, 256), F32)
        for rr in range(4):
            r = half * 4 + rr
            seg = a_cum[:, r:r + 1] - a_cum_t[r:r + 1, :]
            decay = jnp.exp(jnp.where(causal, seg, -jnp.inf))
            m_r = (cb * decay).astype(BF16)
            head_cols = (lane // HEAD_DIM) == rr
            acc = acc + _dot(m_r, jnp.where(head_cols, xdt_half, 0.0).astype(BF16))
        halves.append(acc)
    y_diag = jnp.concatenate(halves, axis=-1)

    state = state_ref[...]
    y_off = _dot(cm_bf, state.astype(BF16)) * decay_in_b
    xw = (xdt * decay_out_b).astype(BF16)
    new_state = lax.dot_general(bm_bf, xw, (((0,), (0,)), ((), ())), preferred_element_type=F32)
    state_ref[...] = state * decay_in_b[q - 1:q, :] + new_state

    y = y_diag + y_off + dskip_ref[...] * xs
    g = y * _silu(z_ref[...])
    o_ref[...] = _rms_normalize(g, nw_ref[...]).astype(BF16)


def _ssd_mixer(proj, conv_w, conv_b, dtb_c, alog_c, dskip_full, norm_w):
    m = proj.shape[0]
    q = SSD_CHUNK
    gw = GROUP_WIDTH
    ds = D_STATE
    col = lambda start, width: start // width
    row_spec = lambda width, block0: pl.BlockSpec((q, width), lambda g, c: (c, block0 + g))
    par_spec = lambda rows, width, block0: pl.BlockSpec((rows, width), lambda g, c: (0, block0 + g))
    return pl.pallas_call(
        _ssd_kernel,
        out_shape=jax.ShapeDtypeStruct((m, D_SSM), BF16),
        grid=(N_GROUPS, m // q),
        in_specs=[
            row_spec(gw, col(COL_Z, gw)),
            row_spec(gw, col(COL_XS, gw)),
            row_spec(ds, col(COL_B, ds)),
            row_spec(ds, col(COL_C, ds)),
            row_spec(V7X_LANES, col(COL_DT, V7X_LANES)),
            par_spec(CONV_WIDTH, gw, 0),
            par_spec(CONV_WIDTH, ds, col(D_SSM, ds)),
            par_spec(CONV_WIDTH, ds, col(D_SSM + N_GROUPS * ds, ds)),
            par_spec(1, gw, 0),
            par_spec(1, ds, col(D_SSM, ds)),
            par_spec(1, ds, col(D_SSM + N_GROUPS * ds, ds)),
            par_spec(1, V7X_LANES, 0),
            par_spec(1, V7X_LANES, 0),
            par_spec(1, gw, 0),
            par_spec(1, gw, 0),
        ],
        out_specs=pl.BlockSpec((q, gw), lambda g, c: (c, g)),
        scratch_shapes=[
            pltpu.VMEM((ds, gw), F32),
            pltpu.VMEM((HALO_ROWS, gw), F32),
            pltpu.VMEM((HALO_ROWS, ds), F32),
            pltpu.VMEM((HALO_ROWS, ds), F32),
        ],
        compiler_params=pltpu.CompilerParams(
            dimension_semantics=("parallel", "arbitrary"),
            vmem_limit_bytes=V7X_VMEM_LIMIT_BYTES),
        name="ssd_mixer",
    )(proj, proj, proj, proj, proj,
      conv_w, conv_w, conv_w, conv_b, conv_b, conv_b,
      dtb_c, alog_c, dskip_full, norm_w)


def _pad_heads_to_lanes(v):
    v = v.reshape(N_GROUPS, HEADS_PER_GROUP)
    v = jnp.pad(v, ((0, 0), (0, V7X_LANES - HEADS_PER_GROUP)))
    return v.reshape(1, N_GROUPS * V7X_LANES)


def kernel(x, attn_norm_w, w_in, conv_w, conv_b, dt_bias, a_log, d_skip, ssd_norm_w, pool_w,
           pool_scale, w_out, ffn_norm_w, w_gate, w_up, w_down, final_norm_w):
    bsz, seqlen, d_model = x.shape
    depth = w_in.shape[0]
    h = x.reshape(bsz * seqlen, d_model)
    assert bsz == 1 and seqlen % 1024 == 0 and d_model == D_MODEL

    for i in range(depth):
        w_cat = _w_in_prep(w_in[i], tk=256)
        proj = _norm_matmul(h, attn_norm_w[i][None, :], w_cat, tm=1024, tn=512)

        y_ssd = _ssd_mixer(
            proj, conv_w[i], conv_b[i][None, :],
            _pad_heads_to_lanes(dt_bias[i]), _pad_heads_to_lanes(a_log[i]),
            jnp.repeat(d_skip[i], HEAD_DIM)[None, :], ssd_norm_w[i][None, :])
        y_pool = _pool_mixer(proj, pool_w[i], pool_scale[i][None, :], tl=512)

        h = _out_proj(y_ssd, y_pool, w_out[i], h, tm=1024, tn=512)

        act = _norm_swiglu(h, ffn_norm_w[i][None, :], w_gate[i], w_up[i], tm=1024, tn=512)
        last = i == depth - 1
        h = _down_proj(act, w_down[i], h, final_norm_w[None, :],
                       tm=1024, tk=512, final_norm=last)

    return h.reshape(bsz, seqlen, d_model)
```

```python
import functools

import jax
import jax.numpy as jnp
from jax import lax
from jax.experimental import pallas as pl
from jax.experimental.pallas import tpu as pltpu

F32 = jnp.float32
BF16 = jnp.bfloat16

NORM_EPS = 1e-5
D_MODEL = 2048
D_SSM = 2048
HEAD_DIM = 64
N_HEADS = D_SSM // HEAD_DIM
N_GROUPS = 4
HEADS_PER_GROUP = N_HEADS // N_GROUPS
GROUP_WIDTH = D_SSM // N_GROUPS
D_STATE = 128
CONV_WIDTH = 4
D_POOL = 2048
POOL_WINDOWS = (2, 4, 8, 16)
POOL_GROUP_DIM = D_POOL // len(POOL_WINDOWS)
D_MIX = D_SSM + D_POOL

V7X_LANES = 128
V7X_SUBLANES = 8
V7X_VMEM_LIMIT_BYTES = 56 * 1024 * 1024

COL_Z = 0
COL_U = COL_Z + D_SSM
COL_XS = COL_U + D_POOL
COL_B = COL_XS + D_SSM
COL_C = COL_B + N_GROUPS * D_STATE
COL_DT = COL_C + N_GROUPS * D_STATE
D_PROJ = COL_DT + N_GROUPS * V7X_LANES

SSD_CHUNK = 256
HALO_ROWS = V7X_SUBLANES
POOL_HALO = 16


def _sigmoid(x):
    return 1.0 / (1.0 + jnp.exp(-x))


def _silu(x):
    return x * _sigmoid(x)


def _rms_normalize(x, w):
    ms = jnp.mean(x * x, axis=-1, keepdims=True)
    return x * lax.rsqrt(ms + NORM_EPS) * w


def _split3(x):
    hi = x.astype(BF16)
    r1 = x - hi.astype(F32)
    mid = r1.astype(BF16)
    lo = (r1 - mid.astype(F32)).astype(BF16)
    return hi, mid, lo


def _dot(a, b):
    return jnp.dot(a, b, preferred_element_type=F32)


def _dot_nt(a, b):
    return lax.dot_general(a, b, (((1,), (1,)), ((), ())), preferred_element_type=F32)


def _dot3(x, rhs_bf16):
    hi, mid, lo = _split3(x)
    return _dot(hi, rhs_bf16) + _dot(mid, rhs_bf16) + _dot(lo, rhs_bf16)


def _dot3_left(lhs_bf16, x):
    hi, mid, lo = _split3(x)
    return _dot(lhs_bf16, hi) + _dot(lhs_bf16, mid) + _dot(lhs_bf16, lo)


SRC_XBC = D_SSM
SRC_DT = SRC_XBC + D_SSM + 2 * N_GROUPS * D_STATE
SRC_U = SRC_DT + N_HEADS
D_IN_PROJ = SRC_U + D_POOL


def _w_in_prep_kernel(w_ref, o_ref):
    o_ref[COL_Z:COL_Z + D_SSM, :] = w_ref[:D_SSM, :].astype(BF16)
    o_ref[COL_U:COL_U + D_POOL, :] = w_ref[SRC_U:SRC_U + D_POOL, :].astype(BF16)
    o_ref[COL_XS:COL_DT, :] = w_ref[SRC_XBC:SRC_DT, :].astype(BF16)
    zeros = jnp.zeros((V7X_LANES - HEADS_PER_GROUP, o_ref.shape[1]), F32)
    for g in range(N_GROUPS):
        row0 = COL_DT + g * V7X_LANES
        src0 = SRC_DT + g * HEADS_PER_GROUP
        block = jnp.concatenate([w_ref[src0:src0 + HEADS_PER_GROUP, :], zeros], axis=0)
        o_ref[row0:row0 + V7X_LANES, :] = block.astype(BF16)


def _w_in_prep(w_t, *, tk):
    n, k = w_t.shape
    assert n == D_IN_PROJ
    return pl.pallas_call(
        _w_in_prep_kernel,
        out_shape=jax.ShapeDtypeStruct((D_PROJ, k), BF16),
        grid=(k // tk,),
        in_specs=[pl.BlockSpec((n, tk), lambda i: (0, i))],
        out_specs=pl.BlockSpec((D_PROJ, tk), lambda i: (0, i)),
        compiler_params=pltpu.CompilerParams(
            dimension_semantics=("parallel",),
            vmem_limit_bytes=V7X_VMEM_LIMIT_BYTES),
        name="w_in_prep",
    )(w_t)


def _norm_matmul_kernel(x_ref, nw_ref, w_ref, o_ref, hn_ref):
    @pl.when(pl.program_id(1) == 0)
    def _():
        hn_ref[...] = _rms_normalize(x_ref[...], nw_ref[...]).astype(BF16)

    o_ref[...] = _dot_nt(hn_ref[...], w_ref[...])


def _norm_matmul(x, nw, w_t, *, tm, tn):
    m, k = x.shape
    n = w_t.shape[0]
    return pl.pallas_call(
        _norm_matmul_kernel,
        out_shape=jax.ShapeDtypeStruct((m, n), F32),
        grid=(m // tm, n // tn),
        in_specs=[
            pl.BlockSpec((tm, k), lambda i, j: (i, 0)),
            pl.BlockSpec((1, k), lambda i, j: (0, 0)),
            pl.BlockSpec((tn, k), lambda i, j: (j, 0)),
        ],
        out_specs=pl.BlockSpec((tm, tn), lambda i, j: (i, j)),
        scratch_shapes=[pltpu.VMEM((tm, k), BF16)],
        compiler_params=pltpu.CompilerParams(
            dimension_semantics=("parallel", "arbitrary"),
            vmem_limit_bytes=V7X_VMEM_LIMIT_BYTES),
        name="in_proj",
    )(x, nw, w_t)


def _norm_swiglu_kernel(x_ref, nw_ref, wg_ref, wu_ref, o_ref, hn_ref):
    @pl.when(pl.program_id(1) == 0)
    def _():
        hn_ref[...] = _rms_normalize(x_ref[...], nw_ref[...]).astype(BF16)

    hn = hn_ref[...]
    gate = _dot(hn, wg_ref[...].astype(BF16))
    up = _dot(hn, wu_ref[...].astype(BF16))
    o_ref[...] = (_silu(gate) * up).astype(BF16)


def _norm_swiglu(x, nw, wg, wu, *, tm, tn):
    m, k = x.shape
    n = wg.shape[1]
    return pl.pallas_call(
        _norm_swiglu_kernel,
        out_shape=jax.ShapeDtypeStruct((m, n), BF16),
        grid=(m // tm, n // tn),
        in_specs=[
            pl.BlockSpec((tm, k), lambda i, j: (i, 0)),
            pl.BlockSpec((1, k), lambda i, j: (0, 0)),
            pl.BlockSpec((k, tn), lambda i, j: (0, j)),
            pl.BlockSpec((k, tn), lambda i, j: (0, j)),
        ],
        out_specs=pl.BlockSpec((tm, tn), lambda i, j: (i, j)),
        scratch_shapes=[pltpu.VMEM((tm, k), BF16)],
        compiler_params=pltpu.CompilerParams(
            dimension_semantics=("parallel", "arbitrary"),
            vmem_limit_bytes=V7X_VMEM_LIMIT_BYTES),
        name="ffn_up",
    )(x, nw, wg, wu)


def _out_proj_kernel(ys_ref, yp_ref, ws_ref, wp_ref, x_ref, o_ref):
    o_ref[...] = (x_ref[...] + _dot(ys_ref[...], ws_ref[...].astype(BF16))
                  + _dot(yp_ref[...], wp_ref[...].astype(BF16)))


def _out_proj(ys, yp, w, x, *, tm, tn):
    m, k = ys.shape
    n = w.shape[1]
    return pl.pallas_call(
        _out_proj_kernel,
        out_shape=jax.ShapeDtypeStruct((m, n), F32),
        grid=(m // tm, n // tn),
        in_specs=[
            pl.BlockSpec((tm, k), lambda i, j: (i, 0)),
            pl.BlockSpec((tm, k), lambda i, j: (i, 0)),
            pl.BlockSpec((k, tn), lambda i, j: (0, j)),
            pl.BlockSpec((k, tn), lambda i, j: (1, j)),
            pl.BlockSpec((tm, tn), lambda i, j: (i, j)),
        ],
        out_specs=pl.BlockSpec((tm, tn), lambda i, j: (i, j)),
        compiler_params=pltpu.CompilerParams(
            dimension_semantics=("parallel", "parallel"),
            vmem_limit_bytes=V7X_VMEM_LIMIT_BYTES),
        name="out_proj",
    )(ys, yp, w, w, x)


def _down_kernel(a_ref, w_ref, h_ref, nw_ref, o_ref, *, nk, final_norm):
    kk = pl.program_id(1)

    @pl.when(kk == 0)
    def _():
        o_ref[...] = h_ref[...]

    o_ref[...] += _dot(a_ref[...], w_ref[...].astype(BF16))

    if final_norm:
        @pl.when(kk == nk - 1)
        def _():
            o_ref[...] = _rms_normalize(o_ref[...], nw_ref[...])


def _down_proj(act, w, h, nw, *, tm, tk, final_norm):
    m, k = act.shape
    n = w.shape[1]
    nk = k // tk
    return pl.pallas_call(
        functools.partial(_down_kernel, nk=nk, final_norm=final_norm),
        out_shape=jax.ShapeDtypeStruct((m, n), F32),
        grid=(m // tm, nk),
        in_specs=[
            pl.BlockSpec((tm, tk), lambda i, kk: (i, kk)),
            pl.BlockSpec((tk, n), lambda i, kk: (kk, 0)),
            pl.BlockSpec((tm, n), lambda i, kk: (i, 0)),
            pl.BlockSpec((1, n), lambda i, kk: (0, 0)),
        ],
        out_specs=pl.BlockSpec((tm, n), lambda i, kk: (i, 0)),
        compiler_params=pltpu.CompilerParams(
            dimension_semantics=("parallel", "arbitrary"),
            vmem_limit_bytes=V7X_VMEM_LIMIT_BYTES),
        name="ffn_down",
    )(act, w, h, nw)


def _pool_kernel(u_ref, halo_ref, pw_ref, ps_ref, o_ref, *, tl):
    i = pl.program_id(0)
    t = i * tl + lax.broadcasted_iota(jnp.int32, (tl, POOL_GROUP_DIM), 0)
    for g, window in enumerate(POOL_WINDOWS):
        cols = slice(g * POOL_GROUP_DIM, (g + 1) * POOL_GROUP_DIM)
        u = u_ref[:, cols]
        halo = jnp.where(i == 0, 0.0, halo_ref[:, cols])
        s = jnp.concatenate([halo, u], axis=0)
        shift = 1
        while shift < window:
            s = s + pltpu.roll(s, shift, axis=0)
            shift *= 2
        count = jnp.minimum(t + 1, window).astype(F32)
        pooled = s[POOL_HALO:] / count - u
        y = _dot(pooled.astype(BF16), pw_ref[g].astype(BF16)) * ps_ref[:, cols]
        o_ref[:, cols] = y.astype(BF16)


def _pool_mixer(proj, pool_w, pool_scale, *, tl):
    m = proj.shape[0]
    assert COL_U % D_POOL == 0 and tl % POOL_HALO == 0
    u_block = COL_U // D_POOL
    halo_blocks_per_tile = tl // POOL_HALO
    return pl.pallas_call(
        functools.partial(_pool_kernel, tl=tl),
        out_shape=jax.ShapeDtypeStruct((m, D_POOL), BF16),
        grid=(m // tl,),
        in_specs=[
            pl.BlockSpec((tl, D_POOL), lambda i: (i, u_block)),
            pl.BlockSpec((POOL_HALO, D_POOL),
                         lambda i: (jnp.maximum(i * halo_blocks_per_tile - 1, 0), u_block)),
            pl.BlockSpec((len(POOL_WINDOWS), POOL_GROUP_DIM, POOL_GROUP_DIM), lambda i: (0, 0, 0)),
            pl.BlockSpec((1, D_POOL), lambda i: (0, 0)),
        ],
        out_specs=pl.BlockSpec((tl, D_POOL), lambda i: (i, 0)),
        compiler_params=pltpu.CompilerParams(
            dimension_semantics=("parallel",),
            vmem_limit_bytes=V7X_VMEM_LIMIT_BYTES),
        name="pool_mixer",
    )(proj, proj, pool_w, pool_scale)


def _causal_conv_silu(raw_ref, halo_ref, w_ref, b_ref):
    cur = raw_ref[...]
    ext = jnp.concatenate([halo_ref[...], cur], axis=0)
    acc = cur * w_ref[CONV_WIDTH - 1:CONV_WIDTH, :] + b_ref[...]
    for back in range(1, CONV_WIDTH):
        tap = w_ref[CONV_WIDTH - 1 - back:CONV_WIDTH - back, :]
        acc = acc + pltpu.roll(ext, back, axis=0)[HALO_ROWS:] * tap
    halo_ref[...] = cur[cur.shape[0] - HALO_ROWS:]
    return _silu(acc)


def _ssd_kernel(z_ref, xs_ref, b_ref, c_ref, dt_ref,
                cwx_ref, cwb_ref, cwc_ref, cbx_ref, cbb_ref, cbc_ref,
                dtb_ref, alog_ref, dskip_ref, nw_ref,
                o_ref,
                state_ref, hx_ref, hb_ref, hc_ref):
    q = SSD_CHUNK

    @pl.when(pl.program_id(1) == 0)
    def _():
        state_ref[...] = jnp.zeros_like(state_ref)
        hx_ref[...] = jnp.zeros_like(hx_ref)
        hb_ref[...] = jnp.zeros_like(hb_ref)
        hc_ref[...] = jnp.zeros_like(hc_ref)

    xs = _causal_conv_silu(xs_ref, hx_ref, cwx_ref, cbx_ref)
    bm = _causal_conv_silu(b_ref, hb_ref, cwb_ref, cbb_ref)
    cm = _causal_conv_silu(c_ref, hc_ref, cwc_ref, cbc_ref)

    x = dt_ref[...] + dtb_ref[...]
    dt = jnp.maximum(x, 0.0) + jnp.log1p(jnp.exp(-jnp.abs(x)))
    a = -jnp.exp(alog_ref[...])

    row_i = lax.broadcasted_iota(jnp.int32, (q, q), 0)
    col_i = lax.broadcasted_iota(jnp.int32, (q, q), 1)
    causal = row_i >= col_i
    tril = jnp.where(causal, 1.0, 0.0).astype(BF16)
    a_cum = _dot3_left(tril, dt * a)
    a_cum_t = a_cum.T
    a_last = a_cum[q - 1:q, :]

    e_row = lax.broadcasted_iota(jnp.int32, (V7X_LANES, GROUP_WIDTH), 0)
    e_col = lax.broadcasted_iota(jnp.int32, (V7X_LANES, GROUP_WIDTH), 1)
    expand = jnp.where(e_col // HEAD_DIM == e_row, 1.0, 0.0).astype(BF16)
    dt_b = _dot3(dt, expand)
    decay_in_b = _dot3(jnp.exp(a_cum), expand)
    decay_out_b = _dot3(jnp.exp(a_last - a_cum), expand)

    xdt = xs * dt_b
    bm_bf = bm.astype(BF16)
    cm_bf = cm.astype(BF16)

    cb = lax.dot_general(cm_bf, bm_bf, (((1,), (1,)), ((), ())), preferred_element_type=F32)

    lane = lax.broadcasted_iota(jnp.int32, (1, GROUP_WIDTH // 2), 1)
    halves = []
    for half in range(2):
        xdt_half = xdt[:, half * 256:(half + 1) * 256]
        acc = jnp.zeros((q, 256), F32)
        for rr in range(4):
            r = half * 4 + rr
            seg = a_cum[:, r:r + 1] - a_cum_t[r:r + 1, :]
            decay = jnp.exp(jnp.where(causal, seg, -jnp.inf))
            m_r = (cb * decay).astype(BF16)
            head_cols = (lane // HEAD_DIM) == rr
            acc = acc + _dot(m_r, jnp.where(head_cols, xdt_half, 0.0).astype(BF16))
        halves.append(acc)
    y_diag = jnp.concatenate(halves, axis=-1)

    state = state_ref[...]
    y_off = _dot(cm_bf, state.astype(BF16)) * decay_in_b
    xw = (xdt * decay_out_b).astype(BF16)
    new_state = lax.dot_general(bm_bf, xw, (((0,), (0,)), ((), ())), preferred_element_type=F32)
    state_ref[...] = state * decay_in_b[q - 1:q, :] + new_state

    y = y_diag + y_off + dskip_ref[...] * xs
    g = y * _silu(z_ref[...])
    o_ref[...] = _rms_normalize(g, nw_ref[...]).astype(BF16)


def _ssd_mixer(proj, conv_w, conv_b, dtb_c, alog_c, dskip_full, norm_w):
    m = proj.shape[0]
    q = SSD_CHUNK
    gw = GROUP_WIDTH
    ds = D_STATE
    col = lambda start, width: start // width
    row_spec = lambda width, block0: pl.BlockSpec((q, width), lambda g, c: (c, block0 + g))
    par_spec = lambda rows, width, block0: pl.BlockSpec((rows, width), lambda g, c: (0, block0 + g))
    return pl.pallas_call(
        _ssd_kernel,
        out_shape=jax.ShapeDtypeStruct((m, D_SSM), BF16),
        grid=(N_GROUPS, m // q),
        in_specs=[
            row_spec(gw, col(COL_Z, gw)),
            row_spec(gw, col(COL_XS, gw)),
            row_spec(ds, col(COL_B, ds)),
            row_spec(ds, col(COL_C, ds)),
            row_spec(V7X_LANES, col(COL_DT, V7X_LANES)),
            par_spec(CONV_WIDTH, gw, 0),
            par_spec(CONV_WIDTH, ds, col(D_SSM, ds)),
            par_spec(CONV_WIDTH, ds, col(D_SSM + N_GROUPS * ds, ds)),
            par_spec(1, gw, 0),
            par_spec(1, ds, col(D_SSM, ds)),
            par_spec(1, ds, col(D_SSM + N_GROUPS * ds, ds)),
            par_spec(1, V7X_LANES, 0),
            par_spec(1, V7X_LANES, 0),
            par_spec(1, gw, 0),
            par_spec(1, gw, 0),
        ],
        out_specs=pl.BlockSpec((q, gw), lambda g, c: (c, g)),
        scratch_shapes=[
            pltpu.VMEM((ds, gw), F32),
            pltpu.VMEM((HALO_ROWS, gw), F32),
            pltpu.VMEM((HALO_ROWS, ds), F32),
            pltpu.VMEM((HALO_ROWS, ds), F32),
        ],
        compiler_params=pltpu.CompilerParams(
            dimension_semantics=("parallel", "arbitrary"),
            vmem_limit_bytes=V7X_VMEM_LIMIT_BYTES),
        name="ssd_mixer",
    )(proj, proj, proj, proj, proj,
      conv_w, conv_w, conv_w, conv_b, conv_b, conv_b,
      dtb_c, alog_c, dskip_full, norm_w)


def _pad_heads_to_lanes(v):
    v = v.reshape(N_GROUPS, HEADS_PER_GROUP)
    v = jnp.pad(v, ((0, 0), (0, V7X_LANES - HEADS_PER_GROUP)))
    return v.reshape(1, N_GROUPS * V7X_LANES)


def kernel(x, attn_norm_w, w_in, conv_w, conv_b, dt_bias, a_log, d_skip, ssd_norm_w, pool_w,
           pool_scale, w_out, ffn_norm_w, w_gate, w_up, w_down, final_norm_w):
    bsz, seqlen, d_model = x.shape
    depth = w_in.shape[0]
    h = x.reshape(bsz * seqlen, d_model)
    assert bsz == 1 and seqlen % 1024 == 0 and d_model == D_MODEL

    for i in range(depth):
        w_cat = _w_in_prep(w_in[i].T, tk=256)
        proj = _norm_matmul(h, attn_norm_w[i][None, :], w_cat, tm=1024, tn=512)

        y_ssd = _ssd_mixer(
            proj, conv_w[i], conv_b[i][None, :],
            _pad_heads_to_lanes(dt_bias[i]), _pad_heads_to_lanes(a_log[i]),
            jnp.repeat(d_skip[i], HEAD_DIM)[None, :], ssd_norm_w[i][None, :])
        y_pool = _pool_mixer(proj, pool_w[i], pool_scale[i][None, :], tl=512)

        h = _out_proj(y_ssd, y_pool, w_out[i], h, tm=1024, tn=512)

        act = _norm_swiglu(h, ffn_norm_w[i][None, :], w_gate[i], w_up[i], tm=1024, tn=512)
        last = i == depth - 1
        h = _down_proj(act, w_down[i], h, final_norm_w[None, :],
                       tm=1024, tk=512, final_norm=last)

    return h.reshape(bsz, seqlen, d_model)
```

```python
import functools

import jax
import jax.numpy as jnp
from jax import lax
from jax.experimental import pallas as pl
from jax.experimental.pallas import tpu as pltpu

F32 = jnp.float32
BF16 = jnp.bfloat16

NORM_EPS = 1e-5
D_MODEL = 2048
D_SSM = 2048
HEAD_DIM = 64
N_HEADS = D_SSM // HEAD_DIM
N_GROUPS = 4
HEADS_PER_GROUP = N_HEADS // N_GROUPS
GROUP_WIDTH = D_SSM // N_GROUPS
D_STATE = 128
CONV_WIDTH = 4
D_POOL = 2048
POOL_WINDOWS = (2, 4, 8, 16)
POOL_GROUP_DIM = D_POOL // len(POOL_WINDOWS)
D_MIX = D_SSM + D_POOL

V7X_LANES = 128
V7X_SUBLANES = 8
V7X_VMEM_LIMIT_BYTES = 56 * 1024 * 1024

COL_Z = 0
COL_U = COL_Z + D_SSM
COL_XS = COL_U + D_POOL
COL_B = COL_XS + D_SSM
COL_C = COL_B + N_GROUPS * D_STATE
COL_DT = COL_C + N_GROUPS * D_STATE
D_PROJ = COL_DT + N_GROUPS * V7X_LANES

SSD_CHUNK = 256
HALO_ROWS = V7X_SUBLANES
POOL_HALO = 16


def _sigmoid(x):
    return 1.0 / (1.0 + jnp.exp(-x))


def _silu(x):
    return x * _sigmoid(x)


def _rms_normalize(x, w):
    ms = jnp.mean(x * x, axis=-1, keepdims=True)
    return x * lax.rsqrt(ms + NORM_EPS) * w


def _split3(x):
    hi = x.astype(BF16)
    r1 = x - hi.astype(F32)
    mid = r1.astype(BF16)
    lo = (r1 - mid.astype(F32)).astype(BF16)
    return hi, mid, lo


def _dot(a, b):
    return jnp.dot(a, b, preferred_element_type=F32)


def _dot_nt(a, b):
    return lax.dot_general(a, b, (((1,), (1,)), ((), ())), preferred_element_type=F32)


def _dot3(x, rhs_bf16):
    hi, mid, lo = _split3(x)
    return _dot(hi, rhs_bf16) + _dot(mid, rhs_bf16) + _dot(lo, rhs_bf16)


def _dot3_left(lhs_bf16, x):
    hi, mid, lo = _split3(x)
    return _dot(lhs_bf16, hi) + _dot(lhs_bf16, mid) + _dot(lhs_bf16, lo)


SRC_XBC = D_SSM
SRC_DT = SRC_XBC + D_SSM + 2 * N_GROUPS * D_STATE
SRC_U = SRC_DT + N_HEADS
D_IN_PROJ = SRC_U + D_POOL


def _w_in_prep_kernel(w_ref, o_ref):
    o_ref[COL_Z:COL_Z + D_SSM, :] = w_ref[:D_SSM, :].astype(BF16)
    o_ref[COL_U:COL_U + D_POOL, :] = w_ref[SRC_U:SRC_U + D_POOL, :].astype(BF16)
    o_ref[COL_XS:COL_DT, :] = w_ref[SRC_XBC:SRC_DT, :].astype(BF16)
    zeros = jnp.zeros((V7X_LANES - HEADS_PER_GROUP, o_ref.shape[1]), F32)
    for g in range(N_GROUPS):
        row0 = COL_DT + g * V7X_LANES
        src0 = SRC_DT + g * HEADS_PER_GROUP
        block = jnp.concatenate([w_ref[src0:src0 + HEADS_PER_GROUP, :], zeros], axis=0)
        o_ref[row0:row0 + V7X_LANES, :] = block.astype(BF16)


def _w_in_prep(w_t, *, tk):
    n, k = w_t.shape
    assert n == D_IN_PROJ
    return pl.pallas_call(
        _w_in_prep_kernel,
        out_shape=jax.ShapeDtypeStruct((D_PROJ, k), BF16),
        grid=(k // tk,),
        in_specs=[pl.BlockSpec((n, tk), lambda i: (0, i))],
        out_specs=pl.BlockSpec((D_PROJ, tk), lambda i: (0, i)),
        compiler_params=pltpu.CompilerParams(
            dimension_semantics=("parallel",),
            vmem_limit_bytes=V7X_VMEM_LIMIT_BYTES),
        name="w_in_prep",
    )(w_t)


def _norm_matmul_kernel(x_ref, nw_ref, w_ref, o_ref, hn_ref):
    @pl.when(pl.program_id(1) == 0)
    def _():
        hn_ref[...] = _rms_normalize(x_ref[...], nw_ref[...]).astype(BF16)

    o_ref[...] = _dot_nt(hn_ref[...], w_ref[...])


def _norm_matmul(x, nw, w_t, *, tm, tn):
    m, k = x.shape
    n = w_t.shape[0]
    return pl.pallas_call(
        _norm_matmul_kernel,
        out_shape=jax.ShapeDtypeStruct((m, n), F32),
        grid=(m // tm, n // tn),
        in_specs=[
            pl.BlockSpec((tm, k), lambda i, j: (i, 0)),
            pl.BlockSpec((1, k), lambda i, j: (0, 0)),
            pl.BlockSpec((tn, k), lambda i, j: (j, 0)),
        ],
        out_specs=pl.BlockSpec((tm, tn), lambda i, j: (i, j)),
        scratch_shapes=[pltpu.VMEM((tm, k), BF16)],
        compiler_params=pltpu.CompilerParams(
            dimension_semantics=("parallel", "arbitrary"),
            vmem_limit_bytes=V7X_VMEM_LIMIT_BYTES),
        name="in_proj",
    )(x, nw, w_t)


def _swiglu_kernel(hn_ref, wg_ref, wu_ref, o_ref, wg_bf_ref, wu_bf_ref):
    @pl.when(pl.program_id(1) == 0)
    def _():
        wg_bf_ref[...] = wg_ref[...].astype(BF16)
        wu_bf_ref[...] = wu_ref[...].astype(BF16)

    hn = hn_ref[...]
    gate = _dot(hn, wg_bf_ref[...])
    up = _dot(hn, wu_bf_ref[...])
    o_ref[...] = (_silu(gate) * up).astype(BF16)


def _swiglu(hn, wg, wu, *, tm, tn):
    m, k = hn.shape
    n = wg.shape[1]
    return pl.pallas_call(
        _swiglu_kernel,
        out_shape=jax.ShapeDtypeStruct((m, n), BF16),
        grid=(n // tn, m // tm),
        in_specs=[
            pl.BlockSpec((tm, k), lambda j, i: (i, 0)),
            pl.BlockSpec((k, tn), lambda j, i: (0, j)),
            pl.BlockSpec((k, tn), lambda j, i: (0, j)),
        ],
        out_specs=pl.BlockSpec((tm, tn), lambda j, i: (i, j)),
        scratch_shapes=[pltpu.VMEM((k, tn), BF16), pltpu.VMEM((k, tn), BF16)],
        compiler_params=pltpu.CompilerParams(
            dimension_semantics=("parallel", "arbitrary"),
            vmem_limit_bytes=V7X_VMEM_LIMIT_BYTES),
        name="ffn_up",
    )(hn, wg, wu)


def _out_proj_kernel(ys_ref, yp_ref, w_ref, x_ref, nw_ref, h_ref, hn_ref):
    k = ys_ref.shape[1]
    h = x_ref[...] + _dot(ys_ref[...], w_ref[:k, :]) + _dot(yp_ref[...], w_ref[k:, :])
    h_ref[...] = h
    hn_ref[...] = _rms_normalize(h, nw_ref[...]).astype(BF16)


def _resident(shape):
    return pl.BlockSpec(shape, lambda i: (0,) * len(shape), pipeline_mode=pl.Buffered(1))


def _out_proj(ys, yp, w, x, nw, *, tm):
    m, k = ys.shape
    n = w.shape[1]
    row_spec = lambda width: pl.BlockSpec((tm, width), lambda i: (i, 0))
    return pl.pallas_call(
        _out_proj_kernel,
        out_shape=(jax.ShapeDtypeStruct((m, n), F32), jax.ShapeDtypeStruct((m, n), BF16)),
        grid=(m // tm,),
        in_specs=[row_spec(k), row_spec(k), _resident(w.shape), row_spec(n), _resident((1, n))],
        out_specs=(row_spec(n), row_spec(n)),
        compiler_params=pltpu.CompilerParams(
            dimension_semantics=("parallel",),
            vmem_limit_bytes=V7X_VMEM_LIMIT_BYTES),
        name="out_proj",
    )(ys, yp, w, x, nw)


def _down_kernel(a_ref, w_ref, h_ref, nw_ref, o_ref, *, final_norm):
    h = h_ref[...] + _dot(a_ref[...], w_ref[...])
    o_ref[...] = _rms_normalize(h, nw_ref[...]) if final_norm else h


def _down_proj(act, w, h, nw, *, tm, final_norm):
    m, k = act.shape
    n = w.shape[1]
    row_spec = lambda width: pl.BlockSpec((tm, width), lambda i: (i, 0))
    return pl.pallas_call(
        functools.partial(_down_kernel, final_norm=final_norm),
        out_shape=jax.ShapeDtypeStruct((m, n), F32),
        grid=(m // tm,),
        in_specs=[row_spec(k), _resident(w.shape), row_spec(n), _resident((1, n))],
        out_specs=row_spec(n),
        compiler_params=pltpu.CompilerParams(
            dimension_semantics=("parallel",),
            vmem_limit_bytes=V7X_VMEM_LIMIT_BYTES),
        name="ffn_down",
    )(act, w, h, nw)


def _pool_kernel(u_ref, halo_ref, pw_ref, ps_ref, o_ref, *, tl):
    i = pl.program_id(0)
    t = i * tl + lax.broadcasted_iota(jnp.int32, (tl, POOL_GROUP_DIM), 0)
    for g, window in enumerate(POOL_WINDOWS):
        cols = slice(g * POOL_GROUP_DIM, (g + 1) * POOL_GROUP_DIM)
        u = u_ref[:, cols]
        halo = jnp.where(i == 0, 0.0, halo_ref[:, cols])
        s = jnp.concatenate([halo, u], axis=0)
        shift = 1
        while shift < window:
            s = s + pltpu.roll(s, shift, axis=0)
            shift *= 2
        count = jnp.minimum(t + 1, window).astype(F32)
        pooled = s[POOL_HALO:] / count - u
        y = _dot(pooled.astype(BF16), pw_ref[g].astype(BF16)) * ps_ref[:, cols]
        o_ref[:, cols] = y.astype(BF16)


def _pool_mixer(proj, pool_w, pool_scale, *, tl):
    m = proj.shape[0]
    assert COL_U % D_POOL == 0 and tl % POOL_HALO == 0
    u_block = COL_U // D_POOL
    halo_blocks_per_tile = tl // POOL_HALO
    return pl.pallas_call(
        functools.partial(_pool_kernel, tl=tl),
        out_shape=jax.ShapeDtypeStruct((m, D_POOL), BF16),
        grid=(m // tl,),
        in_specs=[
            pl.BlockSpec((tl, D_POOL), lambda i: (i, u_block)),
            pl.BlockSpec((POOL_HALO, D_POOL),
                         lambda i: (jnp.maximum(i * halo_blocks_per_tile - 1, 0), u_block)),
            pl.BlockSpec((len(POOL_WINDOWS), POOL_GROUP_DIM, POOL_GROUP_DIM), lambda i: (0, 0, 0)),
            pl.BlockSpec((1, D_POOL), lambda i: (0, 0)),
        ],
        out_specs=pl.BlockSpec((tl, D_POOL), lambda i: (i, 0)),
        compiler_params=pltpu.CompilerParams(
            dimension_semantics=("parallel",),
            vmem_limit_bytes=V7X_VMEM_LIMIT_BYTES),
        name="pool_mixer",
    )(proj, proj, pool_w, pool_scale)


def _causal_conv_silu(raw_ref, halo_ref, w_ref, b_ref):
    cur = raw_ref[...]
    ext = jnp.concatenate([halo_ref[...], cur], axis=0)
    acc = cur * w_ref[CONV_WIDTH - 1:CONV_WIDTH, :] + b_ref[...]
    for back in range(1, CONV_WIDTH):
        tap = w_ref[CONV_WIDTH - 1 - back:CONV_WIDTH - back, :]
        acc = acc + pltpu.roll(ext, back, axis=0)[HALO_ROWS:] * tap
    halo_ref[...] = cur[cur.shape[0] - HALO_ROWS:]
    return _silu(acc)


def _ssd_kernel(z_ref, xs_ref, b_ref, c_ref, dt_ref,
                cwx_ref, cwb_ref, cwc_ref, cbx_ref, cbb_ref, cbc_ref,
                dtb_ref, alog_ref, dskip_ref, nw_ref,
                o_ref,
                state_ref, hx_ref, hb_ref, hc_ref):
    q = SSD_CHUNK

    @pl.when(pl.program_id(1) == 0)
    def _():
        state_ref[...] = jnp.zeros_like(state_ref)
        hx_ref[...] = jnp.zeros_like(hx_ref)
        hb_ref[...] = jnp.zeros_like(hb_ref)
        hc_ref[...] = jnp.zeros_like(hc_ref)

    xs = _causal_conv_silu(xs_ref, hx_ref, cwx_ref, cbx_ref)
    bm = _causal_conv_silu(b_ref, hb_ref, cwb_ref, cbb_ref)
    cm = _causal_conv_silu(c_ref, hc_ref, cwc_ref, cbc_ref)

    x = dt_ref[...] + dtb_ref[...]
    dt = jnp.maximum(x, 0.0) + jnp.log1p(jnp.exp(-jnp.abs(x)))
    a = -jnp.exp(alog_ref[...])

    row_i = lax.broadcasted_iota(jnp.int32, (q, q), 0)
    col_i = lax.broadcasted_iota(jnp.int32, (q, q), 1)
    causal = row_i >= col_i
    tril = jnp.where(causal, 1.0, 0.0).astype(BF16)
    a_cum = _dot3_left(tril, dt * a)
    a_cum_t = a_cum.T
    a_last = a_cum[q - 1:q, :]

    e_row = lax.broadcasted_iota(jnp.int32, (V7X_LANES, GROUP_WIDTH), 0)
    e_col = lax.broadcasted_iota(jnp.int32, (V7X_LANES, GROUP_WIDTH), 1)
    expand = jnp.where(e_col // HEAD_DIM == e_row, 1.0, 0.0).astype(BF16)
    dt_b = _dot3(dt, expand)
    decay_in_b = _dot3(jnp.exp(a_cum), expand)
    decay_out_b = _dot3(jnp.exp(a_last - a_cum), expand)

    xdt = xs * dt_b
    bm_bf = bm.astype(BF16)
    cm_bf = cm.astype(BF16)

    cb = lax.dot_general(cm_bf, bm_bf, (((1,), (1,)), ((), ())), preferred_element_type=F32)

    lane = lax.broadcasted_iota(jnp.int32, (1, GROUP_WIDTH // 2), 1)
    halves = []
    for half in range(2):
        xdt_half = xdt[:, half * 256:(half + 1) * 256]
        acc = jnp.zeros((q, 256), F32)
        for rr in range(4):
            r = half * 4 + rr
            seg = a_cum[:, r:r + 1] - a_cum_t[r:r + 1, :]
            decay = jnp.exp(jnp.where(causal, seg, -jnp.inf))
            m_r = (cb * decay).astype(BF16)
            head_cols = (lane // HEAD_DIM) == rr
            acc = acc + _dot(m_r, jnp.where(head_cols, xdt_half, 0.0).astype(BF16))
        halves.append(acc)
    y_diag = jnp.concatenate(halves, axis=-1)

    state = state_ref[...]
    y_off = _dot(cm_bf, state.astype(BF16)) * decay_in_b
    xw = (xdt * decay_out_b).astype(BF16)
    new_state = lax.dot_general(bm_bf, xw, (((0,), (0,)), ((), ())), preferred_element_type=F32)
    state_ref[...] = state * decay_in_b[q - 1:q, :] + new_state

    y = y_diag + y_off + dskip_ref[...] * xs
    g = y * _silu(z_ref[...])
    o_ref[...] = _rms_normalize(g, nw_ref[...]).astype(BF16)


def _ssd_mixer(proj, conv_w, conv_b, dtb_c, alog_c, dskip_full, norm_w):
    m = proj.shape[0]
    q = SSD_CHUNK
    gw = GROUP_WIDTH
    ds = D_STATE
    col = lambda start, width: start // width
    row_spec = lambda width, block0: pl.BlockSpec((q, width), lambda g, c: (c, block0 + g))
    par_spec = lambda rows, width, block0: pl.BlockSpec((rows, width), lambda g, c: (0, block0 + g))
    return pl.pallas_call(
        _ssd_kernel,
        out_shape=jax.ShapeDtypeStruct((m, D_SSM), BF16),
        grid=(N_GROUPS, m // q),
        in_specs=[
            row_spec(gw, col(COL_Z, gw)),
            row_spec(gw, col(COL_XS, gw)),
            row_spec(ds, col(COL_B, ds)),
            row_spec(ds, col(COL_C, ds)),
            row_spec(V7X_LANES, col(COL_DT, V7X_LANES)),
            par_spec(CONV_WIDTH, gw, 0),
            par_spec(CONV_WIDTH, ds, col(D_SSM, ds)),
            par_spec(CONV_WIDTH, ds, col(D_SSM + N_GROUPS * ds, ds)),
            par_spec(1, gw, 0),
            par_spec(1, ds, col(D_SSM, ds)),
            par_spec(1, ds, col(D_SSM + N_GROUPS * ds, ds)),
            par_spec(1, V7X_LANES, 0),
            par_spec(1, V7X_LANES, 0),
            par_spec(1, gw, 0),
            par_spec(1, gw, 0),
        ],
        out_specs=pl.BlockSpec((q, gw), lambda g, c: (c, g)),
        scratch_shapes=[
            pltpu.VMEM((ds, gw), F32),
            pltpu.VMEM((HALO_ROWS, gw), F32),
            pltpu.VMEM((HALO_ROWS, ds), F32),
            pltpu.VMEM((HALO_ROWS, ds), F32),
        ],
        compiler_params=pltpu.CompilerParams(
            dimension_semantics=("parallel", "arbitrary"),
            vmem_limit_bytes=V7X_VMEM_LIMIT_BYTES),
        name="ssd_mixer",
    )(proj, proj, proj, proj, proj,
      conv_w, conv_w, conv_w, conv_b, conv_b, conv_b,
      dtb_c, alog_c, dskip_full, norm_w)


def _pad_heads_to_lanes(v):
    v = v.reshape(N_GROUPS, HEADS_PER_GROUP)
    v = jnp.pad(v, ((0, 0), (0, V7X_LANES - HEADS_PER_GROUP)))
    return v.reshape(1, N_GROUPS * V7X_LANES)


def kernel(x, attn_norm_w, w_in, conv_w, conv_b, dt_bias, a_log, d_skip, ssd_norm_w, pool_w,
           pool_scale, w_out, ffn_norm_w, w_gate, w_up, w_down, final_norm_w):
    bsz, seqlen, d_model = x.shape
    depth = w_in.shape[0]
    h = x.reshape(bsz * seqlen, d_model)
    assert bsz == 1 and seqlen % 1024 == 0 and d_model == D_MODEL

    for i in range(depth):
        w_cat = _w_in_prep(w_in[i].T, tk=256)
        proj = _norm_matmul(h, attn_norm_w[i][None, :], w_cat, tm=1024, tn=1536)

        y_ssd = _ssd_mixer(
            proj, conv_w[i], conv_b[i][None, :],
            _pad_heads_to_lanes(dt_bias[i]), _pad_heads_to_lanes(a_log[i]),
            jnp.repeat(d_skip[i], HEAD_DIM)[None, :], ssd_norm_w[i][None, :])
        y_pool = _pool_mixer(proj, pool_w[i], pool_scale[i][None, :], tl=512)

        h, hn = _out_proj(y_ssd, y_pool, w_out[i].astype(BF16), h, ffn_norm_w[i][None, :], tm=512)

        act = _swiglu(hn, w_gate[i], w_up[i], tm=2048, tn=512)
        last = i == depth - 1
        h = _down_proj(act, w_down[i].astype(BF16), h, final_norm_w[None, :],
                       tm=512, final_norm=last)

    return h.reshape(bsz, seqlen, d_model)
```

```python
import functools

import jax
import jax.numpy as jnp
from jax import lax
from jax.experimental import pallas as pl
from jax.experimental.pallas import tpu as pltpu

F32 = jnp.float32
BF16 = jnp.bfloat16

NORM_EPS = 1e-5
LOG2_E = 1.4426950408889634
D_MODEL = 2048
D_SSM = 2048
HEAD_DIM = 64
N_HEADS = D_SSM // HEAD_DIM
N_GROUPS = 4
HEADS_PER_GROUP = N_HEADS // N_GROUPS
GROUP_WIDTH = D_SSM // N_GROUPS
D_STATE = 128
CONV_WIDTH = 4
D_POOL = 2048
POOL_WINDOWS = (2, 4, 8, 16)
POOL_GROUP_DIM = D_POOL // len(POOL_WINDOWS)
D_MIX = D_SSM + D_POOL

V7X_LANES = 128
V7X_SUBLANES = 8
V7X_VMEM_LIMIT_BYTES = 56 * 1024 * 1024

COL_XS = 0
COL_B = COL_XS + D_SSM
COL_C = COL_B + N_GROUPS * D_STATE
D_CONV = COL_C + N_GROUPS * D_STATE
COL_Z = D_CONV
COL_U = COL_Z + D_SSM
COL_DT = COL_U + D_POOL
D_PROJ = COL_DT + N_GROUPS * V7X_LANES

SSD_CHUNK = 256
N_EXPANDED = 3
N_REPLICAS = 3 * N_EXPANDED
HALO_ROWS = V7X_SUBLANES
POOL_HALO = 16


def _sigmoid(x):
    return 1.0 / (1.0 + jnp.exp(-x))


def _silu(x):
    return x * _sigmoid(x)


def _rms_normalize(x, w):
    ms = jnp.mean(x * x, axis=-1, keepdims=True)
    return x * lax.rsqrt(ms + NORM_EPS) * w


def _split3(x):
    hi = x.astype(BF16)
    r1 = x - hi.astype(F32)
    mid = r1.astype(BF16)
    lo = (r1 - mid.astype(F32)).astype(BF16)
    return hi, mid, lo


def _dot(a, b):
    return jnp.dot(a, b, preferred_element_type=F32)


def _dot_nt(a, b):
    return lax.dot_general(a, b, (((1,), (1,)), ((), ())), preferred_element_type=F32)


def _dot3(x, rhs_bf16):
    hi, mid, lo = _split3(x)
    return _dot(hi, rhs_bf16) + _dot(mid, rhs_bf16) + _dot(lo, rhs_bf16)


def _dot3_left(lhs_bf16, x):
    hi, mid, lo = _split3(x)
    return _dot(lhs_bf16, hi) + _dot(lhs_bf16, mid) + _dot(lhs_bf16, lo)


SRC_XBC = D_SSM
SRC_DT = SRC_XBC + D_SSM + 2 * N_GROUPS * D_STATE
SRC_U = SRC_DT + N_HEADS
D_IN_PROJ = SRC_U + D_POOL


def _w_in_prep_kernel(w_ref, o_ref):
    o_ref[COL_Z:COL_Z + D_SSM, :] = w_ref[:D_SSM, :].astype(BF16)
    o_ref[COL_U:COL_U + D_POOL, :] = w_ref[SRC_U:SRC_U + D_POOL, :].astype(BF16)
    o_ref[COL_XS:D_CONV, :] = w_ref[SRC_XBC:SRC_DT, :].astype(BF16)
    zeros = jnp.zeros((V7X_LANES - N_REPLICAS * HEADS_PER_GROUP, o_ref.shape[1]), F32)
    for g in range(N_GROUPS):
        row0 = COL_DT + g * V7X_LANES
        src0 = SRC_DT + g * HEADS_PER_GROUP
        heads = w_ref[src0:src0 + HEADS_PER_GROUP, :]
        block = jnp.concatenate([heads] * N_REPLICAS + [zeros], axis=0)
        o_ref[row0:row0 + V7X_LANES, :] = block.astype(BF16)


def _w_in_prep(w_t, *, tk):
    n, k = w_t.shape
    assert n == D_IN_PROJ
    return pl.pallas_call(
        _w_in_prep_kernel,
        out_shape=jax.ShapeDtypeStruct((D_PROJ, k), BF16),
        grid=(k // tk,),
        in_specs=[pl.BlockSpec((n, tk), lambda i: (0, i))],
        out_specs=pl.BlockSpec((D_PROJ, tk), lambda i: (0, i)),
        compiler_params=pltpu.CompilerParams(
            dimension_semantics=("parallel",),
            vmem_limit_bytes=V7X_VMEM_LIMIT_BYTES),
        name="w_in_prep",
    )(w_t)


CONV_CHUNK = 256


def _conv_silu(r, halo, w_ref, b_ref, cols):
    ext = jnp.concatenate([halo, r], axis=0)
    acc = r * w_ref[CONV_WIDTH - 1:CONV_WIDTH, cols] + b_ref[:, cols]
    for back in range(1, CONV_WIDTH):
        tap = w_ref[CONV_WIDTH - 1 - back:CONV_WIDTH - back, cols]
        acc = acc + pltpu.roll(ext, back, axis=0)[HALO_ROWS:] * tap
    return _silu(acc)


def _in_proj_kernel(x_ref, nw_ref, w_ref, cw_ref, cb_ref, o_ref, hn_ref, halo_ref, raw_ref, *,
                    n_conv_blocks):
    i = pl.program_id(0)
    j = pl.program_id(1)
    tm, tn = o_ref.shape

    @pl.when(j == 0)
    def _():
        hn_ref[...] = _rms_normalize(x_ref[...], nw_ref[...]).astype(BF16)

    @pl.when((i == 0) & (j == 0))
    def _():
        halo_ref[...] = jnp.zeros_like(halo_ref)

    @pl.when(j < n_conv_blocks)
    def _():
        n_pieces = tn // CONV_CHUNK
        piece = lambda c: slice(c * CONV_CHUNK, (c + 1) * CONV_CHUNK)
        slot = lambda c: lax.rem(j + c, 2)
        raw_ref[slot(0)] = _dot_nt(hn_ref[...], w_ref[piece(0), :])
        for c in range(n_pieces):
            cols = piece(c)
            if c + 1 < n_pieces:
                raw_ref[slot(c + 1)] = _dot_nt(hn_ref[...], w_ref[piece(c + 1), :])
            r = raw_ref[slot(c)]
            o_ref[:, cols] = _conv_silu(r, halo_ref[j, :, cols], cw_ref, cb_ref, cols)
            halo_ref[j, :, cols] = r[tm - HALO_ROWS:]

    @pl.when(j >= n_conv_blocks)
    def _():
        o_ref[...] = _dot_nt(hn_ref[...], w_ref[...])


def _in_proj(x, nw, w_t, conv_w, conv_b, *, tm, tn):
    m, k = x.shape
    n = w_t.shape[0]
    assert D_CONV % tn == 0 and tn % CONV_CHUNK == 0
    n_conv_blocks = D_CONV // tn
    conv_block = lambda i, j: (0, jnp.minimum(j, n_conv_blocks - 1))
    return pl.pallas_call(
        functools.partial(_in_proj_kernel, n_conv_blocks=n_conv_blocks),
        out_shape=jax.ShapeDtypeStruct((m, n), F32),
        grid=(m // tm, n // tn),
        in_specs=[
            pl.BlockSpec((tm, k), lambda i, j: (i, 0)),
            pl.BlockSpec((1, k), lambda i, j: (0, 0)),
            pl.BlockSpec((tn, k), lambda i, j: (j, 0)),
            pl.BlockSpec((CONV_WIDTH, tn), conv_block),
            pl.BlockSpec((1, tn), conv_block),
        ],
        out_specs=pl.BlockSpec((tm, tn), lambda i, j: (i, j)),
        scratch_shapes=[pltpu.VMEM((tm, k), BF16),
                        pltpu.VMEM((n_conv_blocks, HALO_ROWS, tn), F32),
                        pltpu.VMEM((2, tm, CONV_CHUNK), F32)],
        compiler_params=pltpu.CompilerParams(
            dimension_semantics=("arbitrary", "arbitrary"),
            vmem_limit_bytes=V7X_VMEM_LIMIT_BYTES),
        name="in_proj",
    )(x, nw, w_t, conv_w, conv_b)


def _swiglu_kernel(hn_ref, wg_ref, wu_ref, o_ref, wg_bf_ref, wu_bf_ref):
    @pl.when(pl.program_id(1) == 0)
    def _():
        wg_bf_ref[...] = wg_ref[...].astype(BF16)
        wu_bf_ref[...] = wu_ref[...].astype(BF16)

    hn = hn_ref[...]
    gate = _dot(hn, wg_bf_ref[...])
    up = _dot(hn, wu_bf_ref[...])
    o_ref[...] = (_silu(gate) * up).astype(BF16)


def _swiglu(hn, wg, wu, *, tm, tn):
    m, k = hn.shape
    n = wg.shape[1]
    return pl.pallas_call(
        _swiglu_kernel,
        out_shape=jax.ShapeDtypeStruct((m, n), BF16),
        grid=(n // tn, m // tm),
        in_specs=[
            pl.BlockSpec((tm, k), lambda j, i: (i, 0)),
            pl.BlockSpec((k, tn), lambda j, i: (0, j)),
            pl.BlockSpec((k, tn), lambda j, i: (0, j)),
        ],
        out_specs=pl.BlockSpec((tm, tn), lambda j, i: (i, j)),
        scratch_shapes=[pltpu.VMEM((k, tn), BF16), pltpu.VMEM((k, tn), BF16)],
        compiler_params=pltpu.CompilerParams(
            dimension_semantics=("parallel", "arbitrary"),
            vmem_limit_bytes=V7X_VMEM_LIMIT_BYTES),
        name="ffn_up",
    )(hn, wg, wu)


def _out_proj_kernel(ys_ref, yp_ref, w_ref, x_ref, nw_ref, h_ref, hn_ref):
    k = ys_ref.shape[1]
    h = x_ref[...] + _dot(ys_ref[...], w_ref[:k, :]) + _dot(yp_ref[...], w_ref[k:, :])
    h_ref[...] = h
    hn_ref[...] = _rms_normalize(h, nw_ref[...]).astype(BF16)


def _resident(shape):
    return pl.BlockSpec(shape, lambda i: (0,) * len(shape), pipeline_mode=pl.Buffered(1))


def _out_proj(ys, yp, w, x, nw, *, tm):
    m, k = ys.shape
    n = w.shape[1]
    row_spec = lambda width: pl.BlockSpec((tm, width), lambda i: (i, 0))
    return pl.pallas_call(
        _out_proj_kernel,
        out_shape=(jax.ShapeDtypeStruct((m, n), F32), jax.ShapeDtypeStruct((m, n), BF16)),
        grid=(m // tm,),
        in_specs=[row_spec(k), row_spec(k), _resident(w.shape), row_spec(n), _resident((1, n))],
        out_specs=(row_spec(n), row_spec(n)),
        compiler_params=pltpu.CompilerParams(
            dimension_semantics=("parallel",),
            vmem_limit_bytes=V7X_VMEM_LIMIT_BYTES),
        name="out_proj",
    )(ys, yp, w, x, nw)


def _down_kernel(a_ref, w_ref, h_ref, nw_ref, o_ref, *, final_norm):
    h = h_ref[...] + _dot(a_ref[...], w_ref[...])
    o_ref[...] = _rms_normalize(h, nw_ref[...]) if final_norm else h


def _down_proj(act, w, h, nw, *, tm, final_norm):
    m, k = act.shape
    n = w.shape[1]
    row_spec = lambda width: pl.BlockSpec((tm, width), lambda i: (i, 0))
    return pl.pallas_call(
        functools.partial(_down_kernel, final_norm=final_norm),
        out_shape=jax.ShapeDtypeStruct((m, n), F32),
        grid=(m // tm,),
        in_specs=[row_spec(k), _resident(w.shape), row_spec(n), _resident((1, n))],
        out_specs=row_spec(n),
        compiler_params=pltpu.CompilerParams(
            dimension_semantics=("parallel",),
            vmem_limit_bytes=V7X_VMEM_LIMIT_BYTES),
        name="ffn_down",
    )(act, w, h, nw)


def _pool_kernel(*refs, tl):
    n_groups = len(POOL_WINDOWS)
    u_refs, halo_refs = refs[:n_groups], refs[n_groups:2 * n_groups]
    pw_ref, ps_ref, o_ref = refs[2 * n_groups:]
    i = pl.program_id(0)
    t = i * tl + lax.broadcasted_iota(jnp.int32, (tl, POOL_GROUP_DIM), 0)
    for g, window in enumerate(POOL_WINDOWS):
        cols = slice(g * POOL_GROUP_DIM, (g + 1) * POOL_GROUP_DIM)
        u = u_refs[g][...]
        halo = jnp.where(i == 0, 0.0, halo_refs[g][...])
        s = jnp.concatenate([halo, u], axis=0)
        shift = 1
        while shift < window:
            s = s + pltpu.roll(s, shift, axis=0)
            shift *= 2
        count = jnp.minimum(t + 1, window).astype(F32)
        pooled = s[POOL_HALO:] / count - u
        y = _dot(pooled.astype(BF16), pw_ref[g].astype(BF16)) * ps_ref[:, cols]
        o_ref[:, cols] = y.astype(BF16)


def _pool_mixer(proj, pool_w, pool_scale, *, tl):
    m = proj.shape[0]
    n_groups = len(POOL_WINDOWS)
    gd = POOL_GROUP_DIM
    assert COL_U % gd == 0 and tl % POOL_HALO == 0
    halo_blocks_per_tile = tl // POOL_HALO
    u_specs = [pl.BlockSpec((tl, gd), lambda i, g=g: (i, COL_U // gd + g)) for g in range(n_groups)]
    halo_specs = [
        pl.BlockSpec((POOL_HALO, gd),
                     lambda i, g=g: (jnp.maximum(i * halo_blocks_per_tile - 1, 0), COL_U // gd + g))
        for g in range(n_groups)]
    return pl.pallas_call(
        functools.partial(_pool_kernel, tl=tl),
        out_shape=jax.ShapeDtypeStruct((m, D_POOL), BF16),
        grid=(m // tl,),
        in_specs=u_specs + halo_specs + [
            pl.BlockSpec((n_groups, gd, gd), lambda i: (0, 0, 0)),
            pl.BlockSpec((1, D_POOL), lambda i: (0, 0)),
        ],
        out_specs=pl.BlockSpec((tl, D_POOL), lambda i: (i, 0)),
        compiler_params=pltpu.CompilerParams(
            dimension_semantics=("parallel",),
            vmem_limit_bytes=V7X_VMEM_LIMIT_BYTES),
        name="pool_mixer",
    )(*([proj] * (2 * n_groups)), pool_w, pool_scale)


assert N_REPLICAS * HEADS_PER_GROUP <= V7X_LANES


def _expansion_matrix():
    lane = jnp.arange(V7X_LANES)
    col = jnp.arange(N_EXPANDED * GROUP_WIDTH)
    replica, head = lane // HEADS_PER_GROUP, lane % HEADS_PER_GROUP
    quantity = replica // 3
    hit = ((quantity[:, None] == col[None, :] // GROUP_WIDTH)
           & (head[:, None] == (col[None, :] % GROUP_WIDTH) // HEAD_DIM)
           & (replica[:, None] < N_REPLICAS))
    return hit.astype(BF16)


def _ssd_group(z, xs, bm, cm, dt_raw, dtb, alog, dskip, nw, state_ref, expand, tril, causal):
    q = xs.shape[0]
    x = dt_raw + dtb
    dt = jnp.maximum(x, 0.0) + jnp.log1p(jnp.exp(-jnp.abs(x)))
    a = -jnp.exp(alog)
    a_cum = _dot3_left(tril, dt * a) * LOG2_E
    a_cum_t = a_cum.T
    a_last = a_cum[q - 1:q, :]

    replica = lax.broadcasted_iota(jnp.int32, (1, V7X_LANES), 1) // HEADS_PER_GROUP
    packed = jnp.where(replica < 3, dt,
                       jnp.where(replica < 6, jnp.exp2(a_cum), jnp.exp2(a_last - a_cum)))
    hi = packed.astype(BF16)
    r1 = packed - hi.astype(F32)
    r2 = r1 - r1.astype(BF16).astype(F32)
    level = replica % 3
    split = jnp.where(level == 0, packed, jnp.where(level == 1, r1, r2)).astype(BF16)
    expanded = _dot(split, expand)
    dt_b = expanded[:, :GROUP_WIDTH]
    decay_in_b = expanded[:, GROUP_WIDTH:2 * GROUP_WIDTH]
    decay_out_b = expanded[:, 2 * GROUP_WIDTH:]

    xdt = xs * dt_b
    bm_bf = bm.astype(BF16)
    cm_bf = cm.astype(BF16)
    cb = _dot_nt(cm_bf, bm_bf)

    xdt_bf = xdt.astype(BF16)
    lane = lax.broadcasted_iota(jnp.int32, (1, V7X_LANES), 1)
    keep = [jnp.where((lane // HEAD_DIM) == side, 1.0, 0.0).astype(BF16) for side in range(2)]
    pairs = []
    for pair in range(HEADS_PER_GROUP // 2):
        xdt_pair = xdt_bf[:, pair * V7X_LANES:(pair + 1) * V7X_LANES]
        acc = jnp.zeros((q, V7X_LANES), F32)
        for side in range(2):
            r = 2 * pair + side
            seg = a_cum[:, r:r + 1] - a_cum_t[r:r + 1, :]
            decay = jnp.exp2(jnp.where(causal, seg, -jnp.inf))
            m_r = (cb * decay).astype(BF16)
            acc = acc + _dot(m_r, xdt_pair * keep[side])
        pairs.append(acc)
    y_diag = jnp.concatenate(pairs, axis=-1)

    state = state_ref[...]
    y_off = _dot(cm_bf, state.astype(BF16)) * decay_in_b
    xw = (xdt * decay_out_b).astype(BF16)
    new_state = lax.dot_general(bm_bf, xw, (((0,), (0,)), ((), ())), preferred_element_type=F32)
    state_ref[...] = state * decay_in_b[q - 1:q, :] + new_state

    y = y_diag + y_off + dskip * xs
    g = y * _silu(z)
    return _rms_normalize(g, nw).astype(BF16)


def _ssd_kernel(*refs, groups):
    z_refs = refs[:groups]
    (xs_ref, b_ref, c_ref, dt_ref, dtb_ref, alog_ref, dskip_ref, nw_ref,
     expand_ref, o_ref, state_ref) = refs[groups:]
    q = SSD_CHUNK

    @pl.when(pl.program_id(1) == 0)
    def _():
        state_ref[...] = jnp.zeros_like(state_ref)

    row_i = lax.broadcasted_iota(jnp.int32, (q, q), 0)
    col_i = lax.broadcasted_iota(jnp.int32, (q, q), 1)
    causal = row_i >= col_i
    tril = jnp.where(causal, 1.0, 0.0).astype(BF16)
    expand = expand_ref[...]
    for g in range(groups):
        wide = slice(g * GROUP_WIDTH, (g + 1) * GROUP_WIDTH)
        narrow = slice(g * D_STATE, (g + 1) * D_STATE)
        o_ref[:, wide] = _ssd_group(
            z_refs[g][...], xs_ref[:, wide], b_ref[:, narrow], c_ref[:, narrow], dt_ref[:, narrow],
            dtb_ref[:, narrow], alog_ref[:, narrow], dskip_ref[:, wide], nw_ref[:, wide],
            state_ref.at[g], expand, tril, causal)


def _ssd_mixer(proj, dtb_c, alog_c, dskip_full, norm_w, *, groups):
    m = proj.shape[0]
    q = SSD_CHUNK
    gw = groups * GROUP_WIDTH
    ds = groups * D_STATE
    assert V7X_LANES == D_STATE and N_GROUPS % groups == 0

    def col(start, width):
        assert start % width == 0
        return start // width

    row_spec = lambda width, block0: pl.BlockSpec((q, width), lambda g, c: (c, block0 + g))
    par_spec = lambda width: pl.BlockSpec((1, width), lambda g, c: (0, g))
    z_specs = [
        pl.BlockSpec((q, GROUP_WIDTH),
                     lambda g, c, k=k: (c, col(COL_Z, GROUP_WIDTH) + g * groups + k))
        for k in range(groups)]
    expand = _expansion_matrix()
    return pl.pallas_call(
        functools.partial(_ssd_kernel, groups=groups),
        out_shape=jax.ShapeDtypeStruct((m, D_SSM), BF16),
        grid=(N_GROUPS // groups, m // q),
        in_specs=z_specs + [
            row_spec(gw, col(COL_XS, gw)),
            row_spec(ds, col(COL_B, ds)),
            row_spec(ds, col(COL_C, ds)),
            row_spec(ds, col(COL_DT, ds)),
            par_spec(ds),
            par_spec(ds),
            par_spec(gw),
            par_spec(gw),
            pl.BlockSpec(expand.shape, lambda g, c: (0, 0)),
        ],
        out_specs=pl.BlockSpec((q, gw), lambda g, c: (c, g)),
        scratch_shapes=[pltpu.VMEM((groups, D_STATE, GROUP_WIDTH), F32)],
        compiler_params=pltpu.CompilerParams(
            dimension_semantics=("parallel", "arbitrary"),
            vmem_limit_bytes=V7X_VMEM_LIMIT_BYTES),
        name="ssd_mixer",
    )(*([proj] * (groups + 4)), dtb_c, alog_c, dskip_full, norm_w, expand)


def _pad_heads_to_lanes(v):
    v = jnp.tile(v.reshape(N_GROUPS, HEADS_PER_GROUP), (1, N_REPLICAS))
    v = jnp.pad(v, ((0, 0), (0, V7X_LANES - N_REPLICAS * HEADS_PER_GROUP)))
    return v.reshape(1, N_GROUPS * V7X_LANES)


def kernel(x, attn_norm_w, w_in, conv_w, conv_b, dt_bias, a_log, d_skip, ssd_norm_w, pool_w,
           pool_scale, w_out, ffn_norm_w, w_gate, w_up, w_down, final_norm_w):
    bsz, seqlen, d_model = x.shape
    depth = w_in.shape[0]
    h = x.reshape(bsz * seqlen, d_model)
    assert bsz == 1 and seqlen % 2048 == 0 and d_model == D_MODEL

    for i in range(depth):
        w_cat = _w_in_prep(w_in[i].T, tk=256)
        proj = _in_proj(h, attn_norm_w[i][None, :], w_cat, conv_w[i], conv_b[i][None, :],
                        tm=1024, tn=1536)

        y_ssd = _ssd_mixer(
            proj, _pad_heads_to_lanes(dt_bias[i]), _pad_heads_to_lanes(a_log[i]),
            jnp.repeat(d_skip[i], HEAD_DIM)[None, :], ssd_norm_w[i][None, :], groups=4)
        y_pool = _pool_mixer(proj, pool_w[i], pool_scale[i][None, :], tl=512)

        h, hn = _out_proj(y_ssd, y_pool, w_out[i].astype(BF16), h, ffn_norm_w[i][None, :], tm=512)

        act = _swiglu(hn, w_gate[i], w_up[i], tm=2048, tn=512)
        last = i == depth - 1
        h = _down_proj(act, w_down[i].astype(BF16), h, final_norm_w[None, :],
                       tm=512, final_norm=last)

    return h.reshape(bsz, seqlen, d_model)
```

```python
import functools

import jax
import jax.numpy as jnp
from jax import lax
from jax.experimental import pallas as pl
from jax.experimental.pallas import tpu as pltpu

F32 = jnp.float32
BF16 = jnp.bfloat16

NORM_EPS = 1e-5
LOG2_E = 1.4426950408889634
D_MODEL = 2048
D_SSM = 2048
HEAD_DIM = 64
N_HEADS = D_SSM // HEAD_DIM
N_GROUPS = 4
HEADS_PER_GROUP = N_HEADS // N_GROUPS
GROUP_WIDTH = D_SSM // N_GROUPS
D_STATE = 128
CONV_WIDTH = 4
D_POOL = 2048
POOL_WINDOWS = (2, 4, 8, 16)
POOL_GROUP_DIM = D_POOL // len(POOL_WINDOWS)
D_MIX = D_SSM + D_POOL

V7X_LANES = 128
V7X_SUBLANES = 8
V7X_VMEM_LIMIT_BYTES = 56 * 1024 * 1024

PROJ_BLOCKS = 3
D_CONV = D_SSM + 2 * N_GROUPS * D_STATE
D_REST = D_SSM + D_POOL + N_GROUPS * V7X_LANES
CONV_PER_BLOCK = D_CONV // PROJ_BLOCKS
REST_PER_BLOCK = D_REST // PROJ_BLOCKS
PROJ_BLOCK = CONV_PER_BLOCK + REST_PER_BLOCK
D_PROJ = PROJ_BLOCKS * PROJ_BLOCK
GRANULE = 512


def _conv_col(c):
    return (c // CONV_PER_BLOCK) * PROJ_BLOCK + c % CONV_PER_BLOCK


def _rest_col(r):
    return (r // REST_PER_BLOCK) * PROJ_BLOCK + CONV_PER_BLOCK + r % REST_PER_BLOCK


CONV_XS, CONV_B, CONV_C = 0, D_SSM, D_SSM + N_GROUPS * D_STATE
REST_Z, REST_U, REST_DT = 0, D_SSM, D_SSM + D_POOL

SSD_CHUNK = 256
N_EXPANDED = 3
N_REPLICAS = 3 * N_EXPANDED
HALO_ROWS = V7X_SUBLANES
POOL_HALO = 16


def _sigmoid(x):
    return 1.0 / (1.0 + jnp.exp(-x))


def _silu(x):
    return x * _sigmoid(x)


def _rms_normalize(x, w):
    ms = jnp.mean(x * x, axis=-1, keepdims=True)
    return x * lax.rsqrt(ms + NORM_EPS) * w


def _split3(x):
    hi = x.astype(BF16)
    r1 = x - hi.astype(F32)
    mid = r1.astype(BF16)
    lo = (r1 - mid.astype(F32)).astype(BF16)
    return hi, mid, lo


def _dot(a, b):
    return jnp.dot(a, b, preferred_element_type=F32)


def _dot_nt(a, b):
    return lax.dot_general(a, b, (((1,), (1,)), ((), ())), preferred_element_type=F32)


def _dot3(x, rhs_bf16):
    hi, mid, lo = _split3(x)
    return _dot(hi, rhs_bf16) + _dot(mid, rhs_bf16) + _dot(lo, rhs_bf16)


def _dot3_left(lhs_bf16, x):
    hi, mid, lo = _split3(x)
    return _dot(lhs_bf16, hi) + _dot(lhs_bf16, mid) + _dot(lhs_bf16, lo)


SRC_XBC = D_SSM
SRC_DT = SRC_XBC + D_SSM + 2 * N_GROUPS * D_STATE
SRC_U = SRC_DT + N_HEADS
D_IN_PROJ = SRC_U + D_POOL


def _w_in_prep_kernel(w_ref, o_ref):
    def copy(dst, src):
        o_ref[dst:dst + GRANULE, :] = w_ref[src:src + GRANULE, :].astype(BF16)

    for c in range(0, D_CONV, GRANULE):
        copy(_conv_col(c), SRC_XBC + c)
    for c in range(0, D_SSM, GRANULE):
        copy(_rest_col(REST_Z + c), c)
    for c in range(0, D_POOL, GRANULE):
        copy(_rest_col(REST_U + c), SRC_U + c)
    zeros = jnp.zeros((V7X_LANES - N_REPLICAS * HEADS_PER_GROUP, o_ref.shape[1]), F32)
    for g in range(N_GROUPS):
        row0 = _rest_col(REST_DT) + g * V7X_LANES
        src0 = SRC_DT + g * HEADS_PER_GROUP
        heads = w_ref[src0:src0 + HEADS_PER_GROUP, :]
        block = jnp.concatenate([heads] * N_REPLICAS + [zeros], axis=0)
        o_ref[row0:row0 + V7X_LANES, :] = block.astype(BF16)


def _w_in_prep(w_t, *, tk):
    n, k = w_t.shape
    assert n == D_IN_PROJ
    return pl.pallas_call(
        _w_in_prep_kernel,
        out_shape=jax.ShapeDtypeStruct((D_PROJ, k), BF16),
        grid=(k // tk,),
        in_specs=[pl.BlockSpec((n, tk), lambda i: (0, i))],
        out_specs=pl.BlockSpec((D_PROJ, tk), lambda i: (0, i)),
        compiler_params=pltpu.CompilerParams(
            dimension_semantics=("parallel",),
            vmem_limit_bytes=V7X_VMEM_LIMIT_BYTES),
        name="w_in_prep",
    )(w_t)


CONV_CHUNK = 256


def _conv_silu(ext_ref, w_ref, b_ref, cols, row0):
    tm = ext_ref.shape[0] - HALO_ROWS
    cur = ext_ref[pl.ds(pl.multiple_of(row0, HALO_ROWS), tm), :]
    acc = cur * w_ref[CONV_WIDTH - 1:CONV_WIDTH, cols] + b_ref[:, cols]
    for back in range(1, CONV_WIDTH):
        tap = w_ref[CONV_WIDTH - 1 - back:CONV_WIDTH - back, cols]
        acc = acc + ext_ref[HALO_ROWS - back:HALO_ROWS - back + tm, :] * tap
    return _silu(acc)


def _in_proj_kernel(x_ref, nw_ref, w_ref, cw_ref, cb_ref, o_ref, hn_ref, halo_ref, *ext_refs):
    i = pl.program_id(0)
    j = pl.program_id(1)
    tm = o_ref.shape[0]

    @pl.when(j == 0)
    def _():
        hn_ref[...] = _rms_normalize(x_ref[...], nw_ref[...]).astype(BF16)

    @pl.when((i == 0) & (j == 0))
    def _():
        halo_ref[...] = jnp.zeros_like(halo_ref)

    piece = lambda c: slice(c * CONV_CHUNK, (c + 1) * CONV_CHUNK)
    for c, ext_ref in enumerate(ext_refs):
        ext_ref[:HALO_ROWS, :] = halo_ref[j, :, piece(c)]
        ext_ref[HALO_ROWS:, :] = _dot_nt(hn_ref[...], w_ref[piece(c), :])
    for c, ext_ref in enumerate(ext_refs):
        cols = piece(c)
        o_ref[:, cols] = _conv_silu(ext_ref, cw_ref, cb_ref, cols, HALO_ROWS + jnp.minimum(j, 0))
        halo_ref[j, :, cols] = ext_ref[tm:, :]
    o_ref[:, CONV_PER_BLOCK:] = _dot_nt(hn_ref[...], w_ref[CONV_PER_BLOCK:, :])


def _in_proj(x, nw, w_t, conv_w, conv_b, *, tm):
    m, k = x.shape
    assert w_t.shape[0] == D_PROJ and CONV_PER_BLOCK % CONV_CHUNK == 0
    return pl.pallas_call(
        _in_proj_kernel,
        out_shape=jax.ShapeDtypeStruct((m, D_PROJ), F32),
        grid=(m // tm, PROJ_BLOCKS),
        in_specs=[
            pl.BlockSpec((tm, k), lambda i, j: (i, 0)),
            pl.BlockSpec((1, k), lambda i, j: (0, 0)),
            pl.BlockSpec((PROJ_BLOCK, k), lambda i, j: (j, 0)),
            pl.BlockSpec((CONV_WIDTH, CONV_PER_BLOCK), lambda i, j: (0, j)),
            pl.BlockSpec((1, CONV_PER_BLOCK), lambda i, j: (0, j)),
        ],
        out_specs=pl.BlockSpec((tm, PROJ_BLOCK), lambda i, j: (i, j)),
        scratch_shapes=[pltpu.VMEM((tm, k), BF16),
                        pltpu.VMEM((PROJ_BLOCKS, HALO_ROWS, CONV_PER_BLOCK), F32)]
        + [pltpu.VMEM((HALO_ROWS + tm, CONV_CHUNK), F32)] * (CONV_PER_BLOCK // CONV_CHUNK),
        compiler_params=pltpu.CompilerParams(
            dimension_semantics=("arbitrary", "arbitrary"),
            vmem_limit_bytes=V7X_VMEM_LIMIT_BYTES),
        name="in_proj",
    )(x, nw, w_t, conv_w, conv_b)


def _swiglu_kernel(hn_ref, wg_ref, wu_ref, o_ref, wg_bf_ref, wu_bf_ref):
    @pl.when(pl.program_id(1) == 0)
    def _():
        wg_bf_ref[...] = wg_ref[...].astype(BF16)
        wu_bf_ref[...] = wu_ref[...].astype(BF16)

    hn = hn_ref[...]
    gate = _dot(hn, wg_bf_ref[...])
    up = _dot(hn, wu_bf_ref[...])
    o_ref[...] = (_silu(gate) * up).astype(BF16)


def _swiglu(hn, wg, wu, *, tm, tn):
    m, k = hn.shape
    n = wg.shape[1]
    return pl.pallas_call(
        _swiglu_kernel,
        out_shape=jax.ShapeDtypeStruct((m, n), BF16),
        grid=(n // tn, m // tm),
        in_specs=[
            pl.BlockSpec((tm, k), lambda j, i: (i, 0)),
            pl.BlockSpec((k, tn), lambda j, i: (0, j)),
            pl.BlockSpec((k, tn), lambda j, i: (0, j)),
        ],
        out_specs=pl.BlockSpec((tm, tn), lambda j, i: (i, j)),
        scratch_shapes=[pltpu.VMEM((k, tn), BF16), pltpu.VMEM((k, tn), BF16)],
        compiler_params=pltpu.CompilerParams(
            dimension_semantics=("parallel", "arbitrary"),
            vmem_limit_bytes=V7X_VMEM_LIMIT_BYTES),
        name="ffn_up",
    )(hn, wg, wu)


def _out_proj_kernel(ys_ref, yp_ref, w_ref, x_ref, nw_ref, h_ref, hn_ref):
    k = ys_ref.shape[1]
    h = x_ref[...] + _dot(ys_ref[...], w_ref[:k, :]) + _dot(yp_ref[...], w_ref[k:, :])
    h_ref[...] = h
    hn_ref[...] = _rms_normalize(h, nw_ref[...]).astype(BF16)


def _resident(shape):
    return pl.BlockSpec(shape, lambda i: (0,) * len(shape), pipeline_mode=pl.Buffered(1))


def _out_proj(ys, yp, w, x, nw, *, tm):
    m, k = ys.shape
    n = w.shape[1]
    row_spec = lambda width: pl.BlockSpec((tm, width), lambda i: (i, 0))
    return pl.pallas_call(
        _out_proj_kernel,
        out_shape=(jax.ShapeDtypeStruct((m, n), F32), jax.ShapeDtypeStruct((m, n), BF16)),
        grid=(m // tm,),
        in_specs=[row_spec(k), row_spec(k), _resident(w.shape), row_spec(n), _resident((1, n))],
        out_specs=(row_spec(n), row_spec(n)),
        compiler_params=pltpu.CompilerParams(
            dimension_semantics=("parallel",),
            vmem_limit_bytes=V7X_VMEM_LIMIT_BYTES),
        name="out_proj",
    )(ys, yp, w, x, nw)


def _down_kernel(a_ref, w_ref, h_ref, nw_ref, o_ref, *, final_norm):
    h = h_ref[...] + _dot(a_ref[...], w_ref[...])
    o_ref[...] = _rms_normalize(h, nw_ref[...]) if final_norm else h


def _down_proj(act, w, h, nw, *, tm, final_norm):
    m, k = act.shape
    n = w.shape[1]
    row_spec = lambda width: pl.BlockSpec((tm, width), lambda i: (i, 0))
    return pl.pallas_call(
        functools.partial(_down_kernel, final_norm=final_norm),
        out_shape=jax.ShapeDtypeStruct((m, n), F32),
        grid=(m // tm,),
        in_specs=[row_spec(k), _resident(w.shape), row_spec(n), _resident((1, n))],
        out_specs=row_spec(n),
        compiler_params=pltpu.CompilerParams(
            dimension_semantics=("parallel",),
            vmem_limit_bytes=V7X_VMEM_LIMIT_BYTES),
        name="ffn_down",
    )(act, w, h, nw)


def _pool_kernel(*refs, tl):
    n_groups = len(POOL_WINDOWS)
    u_refs, halo_refs = refs[:n_groups], refs[n_groups:2 * n_groups]
    pw_ref, ps_ref, o_ref = refs[2 * n_groups:]
    i = pl.program_id(0)
    t = i * tl + lax.broadcasted_iota(jnp.int32, (tl, POOL_GROUP_DIM), 0)
    for g, window in enumerate(POOL_WINDOWS):
        cols = slice(g * POOL_GROUP_DIM, (g + 1) * POOL_GROUP_DIM)
        u = u_refs[g][...]
        halo = jnp.where(i == 0, 0.0, halo_refs[g][...])
        s = jnp.concatenate([halo, u], axis=0)
        shift = 1
        while shift < window:
            s = s + pltpu.roll(s, shift, axis=0)
            shift *= 2
        count = jnp.minimum(t + 1, window).astype(F32)
        pooled = s[POOL_HALO:] / count - u
        y = _dot(pooled.astype(BF16), pw_ref[g].astype(BF16)) * ps_ref[:, cols]
        o_ref[:, cols] = y.astype(BF16)


def _pool_mixer(proj, pool_w, pool_scale, *, tl):
    m = proj.shape[0]
    n_groups = len(POOL_WINDOWS)
    gd = POOL_GROUP_DIM
    assert gd == GRANULE and tl % POOL_HALO == 0
    halo_blocks_per_tile = tl // POOL_HALO
    u_block = [_rest_col(REST_U + g * gd) // gd for g in range(n_groups)]
    u_specs = [pl.BlockSpec((tl, gd), lambda i, g=g: (i, u_block[g])) for g in range(n_groups)]
    halo_specs = [
        pl.BlockSpec((POOL_HALO, gd),
                     lambda i, g=g: (jnp.maximum(i * halo_blocks_per_tile - 1, 0), u_block[g]))
        for g in range(n_groups)]
    return pl.pallas_call(
        functools.partial(_pool_kernel, tl=tl),
        out_shape=jax.ShapeDtypeStruct((m, D_POOL), BF16),
        grid=(m // tl,),
        in_specs=u_specs + halo_specs + [
            pl.BlockSpec((n_groups, gd, gd), lambda i: (0, 0, 0)),
            pl.BlockSpec((1, D_POOL), lambda i: (0, 0)),
        ],
        out_specs=pl.BlockSpec((tl, D_POOL), lambda i: (i, 0)),
        compiler_params=pltpu.CompilerParams(
            dimension_semantics=("parallel",),
            vmem_limit_bytes=V7X_VMEM_LIMIT_BYTES),
        name="pool_mixer",
    )(*([proj] * (2 * n_groups)), pool_w, pool_scale)


assert N_REPLICAS * HEADS_PER_GROUP <= V7X_LANES


def _expansion_matrix():
    lane = jnp.arange(V7X_LANES)
    col = jnp.arange(N_EXPANDED * GROUP_WIDTH)
    replica, head = lane // HEADS_PER_GROUP, lane % HEADS_PER_GROUP
    quantity = replica // 3
    hit = ((quantity[:, None] == col[None, :] // GROUP_WIDTH)
           & (head[:, None] == (col[None, :] % GROUP_WIDTH) // HEAD_DIM)
           & (replica[:, None] < N_REPLICAS))
    return hit.astype(BF16)


def _ssd_group(z, xs, bm, cm, dt_raw, dtb, alog, dskip, nw, state_ref, expand, tril, causal):
    q = xs.shape[0]
    x = dt_raw + dtb
    dt = jnp.maximum(x, 0.0) + jnp.log1p(jnp.exp(-jnp.abs(x)))
    a = -jnp.exp(alog)
    a_cum = _dot3_left(tril, dt * a) * LOG2_E
    a_cum_t = a_cum.T
    a_last = a_cum[q - 1:q, :]

    replica = lax.broadcasted_iota(jnp.int32, (1, V7X_LANES), 1) // HEADS_PER_GROUP
    packed = jnp.where(replica < 3, dt,
                       jnp.where(replica < 6, jnp.exp2(a_cum), jnp.exp2(a_last - a_cum)))
    hi = packed.astype(BF16)
    r1 = packed - hi.astype(F32)
    r2 = r1 - r1.astype(BF16).astype(F32)
    level = replica % 3
    split = jnp.where(level == 0, packed, jnp.where(level == 1, r1, r2)).astype(BF16)
    expanded = _dot(split, expand)
    dt_b = expanded[:, :GROUP_WIDTH]
    decay_in_b = expanded[:, GROUP_WIDTH:2 * GROUP_WIDTH]
    decay_out_b = expanded[:, 2 * GROUP_WIDTH:]

    xdt = xs * dt_b
    bm_bf = bm.astype(BF16)
    cm_bf = cm.astype(BF16)
    cb = _dot_nt(cm_bf, bm_bf)

    xdt_bf = xdt.astype(BF16)
    lane = lax.broadcasted_iota(jnp.int32, (1, V7X_LANES), 1)
    keep = [jnp.where((lane // HEAD_DIM) == side, 1.0, 0.0).astype(BF16) for side in range(2)]
    pairs = []
    for pair in range(HEADS_PER_GROUP // 2):
        xdt_pair = xdt_bf[:, pair * V7X_LANES:(pair + 1) * V7X_LANES]
        acc = jnp.zeros((q, V7X_LANES), F32)
        for side in range(2):
            r = 2 * pair + side
            seg = a_cum[:, r:r + 1] - a_cum_t[r:r + 1, :]
            decay = jnp.exp2(jnp.where(causal, seg, -jnp.inf))
            m_r = (cb * decay).astype(BF16)
            acc = acc + _dot(m_r, xdt_pair * keep[side])
        pairs.append(acc)
    y_diag = jnp.concatenate(pairs, axis=-1)

    state = state_ref[...]
    y_off = _dot(cm_bf, state.astype(BF16)) * decay_in_b
    xw = (xdt * decay_out_b).astype(BF16)
    new_state = lax.dot_general(bm_bf, xw, (((0,), (0,)), ((), ())), preferred_element_type=F32)
    state_ref[...] = state * decay_in_b[q - 1:q, :] + new_state

    y = y_diag + y_off + dskip * xs
    g = y * _silu(z)
    return _rms_normalize(g, nw).astype(BF16)


def _ssd_kernel(*refs):
    groups = N_GROUPS
    z_refs, xs_refs = refs[:groups], refs[groups:2 * groups]
    (b_ref, c_ref, dt_ref, dtb_ref, alog_ref, dskip_ref, nw_ref,
     expand_ref, o_ref, state_ref) = refs[2 * groups:]
    q = SSD_CHUNK

    @pl.when(pl.program_id(0) == 0)
    def _():
        state_ref[...] = jnp.zeros_like(state_ref)

    row_i = lax.broadcasted_iota(jnp.int32, (q, q), 0)
    col_i = lax.broadcasted_iota(jnp.int32, (q, q), 1)
    causal = row_i >= col_i
    tril = jnp.where(causal, 1.0, 0.0).astype(BF16)
    expand = expand_ref[...]
    for g in range(groups):
        wide = slice(g * GROUP_WIDTH, (g + 1) * GROUP_WIDTH)
        narrow = slice(g * D_STATE, (g + 1) * D_STATE)
        o_ref[:, wide] = _ssd_group(
            z_refs[g][...], xs_refs[g][...], b_ref[:, narrow], c_ref[:, narrow], dt_ref[:, narrow],
            dtb_ref[:, narrow], alog_ref[:, narrow], dskip_ref[:, wide], nw_ref[:, wide],
            state_ref.at[g], expand, tril, causal)


def _ssd_mixer(proj, dtb_c, alog_c, dskip_full, norm_w):
    m = proj.shape[0]
    q = SSD_CHUNK
    ds = N_GROUPS * D_STATE
    assert V7X_LANES == D_STATE and GROUP_WIDTH == GRANULE

    def block(start, width):
        assert start % width == 0
        return start // width

    row_spec = lambda width, start: pl.BlockSpec((q, width), lambda c: (c, block(start, width)))
    whole = lambda arr: pl.BlockSpec(arr.shape, lambda c: (0,) * arr.ndim)
    expand = _expansion_matrix()
    z_specs = [row_spec(GROUP_WIDTH, _rest_col(REST_Z + g * GROUP_WIDTH)) for g in range(N_GROUPS)]
    xs_specs = [row_spec(GROUP_WIDTH, _conv_col(CONV_XS + g * GROUP_WIDTH)) for g in range(N_GROUPS)]
    return pl.pallas_call(
        _ssd_kernel,
        out_shape=jax.ShapeDtypeStruct((m, D_SSM), BF16),
        grid=(m // q,),
        in_specs=z_specs + xs_specs + [
            row_spec(ds, _conv_col(CONV_B)),
            row_spec(ds, _conv_col(CONV_C)),
            row_spec(ds, _rest_col(REST_DT)),
            whole(dtb_c), whole(alog_c), whole(dskip_full), whole(norm_w), whole(expand),
        ],
        out_specs=pl.BlockSpec((q, D_SSM), lambda c: (c, 0)),
        scratch_shapes=[pltpu.VMEM((N_GROUPS, D_STATE, GROUP_WIDTH), F32)],
        compiler_params=pltpu.CompilerParams(
            dimension_semantics=("arbitrary",),
            vmem_limit_bytes=V7X_VMEM_LIMIT_BYTES),
        name="ssd_mixer",
    )(*([proj] * (2 * N_GROUPS + 3)), dtb_c, alog_c, dskip_full, norm_w, expand)


def _pad_heads_to_lanes(v):
    v = jnp.tile(v.reshape(N_GROUPS, HEADS_PER_GROUP), (1, N_REPLICAS))
    v = jnp.pad(v, ((0, 0), (0, V7X_LANES - N_REPLICAS * HEADS_PER_GROUP)))
    return v.reshape(1, N_GROUPS * V7X_LANES)


def kernel(x, attn_norm_w, w_in, conv_w, conv_b, dt_bias, a_log, d_skip, ssd_norm_w, pool_w,
           pool_scale, w_out, ffn_norm_w, w_gate, w_up, w_down, final_norm_w):
    bsz, seqlen, d_model = x.shape
    depth = w_in.shape[0]
    h = x.reshape(bsz * seqlen, d_model)
    assert bsz == 1 and seqlen % 2048 == 0 and d_model == D_MODEL

    for i in range(depth):
        w_cat = _w_in_prep(w_in[i].T, tk=256)
        proj = _in_proj(h, attn_norm_w[i][None, :], w_cat, conv_w[i], conv_b[i][None, :], tm=512)

        y_ssd = _ssd_mixer(
            proj, _pad_heads_to_lanes(dt_bias[i]), _pad_heads_to_lanes(a_log[i]),
            jnp.repeat(d_skip[i], HEAD_DIM)[None, :], ssd_norm_w[i][None, :])
        y_pool = _pool_mixer(proj, pool_w[i], pool_scale[i][None, :], tl=1024)

        h, hn = _out_proj(y_ssd, y_pool, w_out[i].astype(BF16), h, ffn_norm_w[i][None, :], tm=512)

        act = _swiglu(hn, w_gate[i], w_up[i], tm=2048, tn=512)
        last = i == depth - 1
        h = _down_proj(act, w_down[i].astype(BF16), h, final_norm_w[None, :],
                       tm=512, final_norm=last)

    return h.reshape(bsz, seqlen, d_model)
```

```python
import functools

import jax
import jax.numpy as jnp
from jax import lax
from jax.experimental import pallas as pl
from jax.experimental.pallas import tpu as pltpu

F32 = jnp.float32
BF16 = jnp.bfloat16

NORM_EPS = 1e-5
LOG2_E = 1.4426950408889634
D_MODEL = 2048
D_SSM = 2048
HEAD_DIM = 64
N_HEADS = D_SSM // HEAD_DIM
N_GROUPS = 4
HEADS_PER_GROUP = N_HEADS // N_GROUPS
GROUP_WIDTH = D_SSM // N_GROUPS
D_STATE = 128
CONV_WIDTH = 4
D_POOL = 2048
POOL_WINDOWS = (2, 4, 8, 16)
POOL_GROUP_DIM = D_POOL // len(POOL_WINDOWS)
D_MIX = D_SSM + D_POOL

V7X_LANES = 128
V7X_SUBLANES = 8
V7X_VMEM_LIMIT_BYTES = 60 * 1024 * 1024

COL_XS = 0
COL_B = COL_XS + D_SSM
COL_C = COL_B + N_GROUPS * D_STATE
D_CONV = COL_C + N_GROUPS * D_STATE
COL_Z = D_CONV
COL_U = COL_Z + D_SSM
COL_DT = COL_U + D_POOL
D_PROJ = COL_DT + N_GROUPS * V7X_LANES

SSD_CHUNK = 256
N_EXPANDED = 3
N_REPLICAS = 3 * N_EXPANDED
HALO_ROWS = V7X_SUBLANES
POOL_HALO = 16


def _sigmoid(x):
    return 1.0 / (1.0 + jnp.exp(-x))


def _silu(x):
    return x * _sigmoid(x)


def _rms_normalize(x, w):
    ms = jnp.mean(x * x, axis=-1, keepdims=True)
    return x * lax.rsqrt(ms + NORM_EPS) * w


def _split3(x):
    hi = x.astype(BF16)
    r1 = x - hi.astype(F32)
    mid = r1.astype(BF16)
    lo = (r1 - mid.astype(F32)).astype(BF16)
    return hi, mid, lo


def _dot(a, b):
    return jnp.dot(a, b, preferred_element_type=F32)


def _dot_nt(a, b):
    return lax.dot_general(a, b, (((1,), (1,)), ((), ())), preferred_element_type=F32)


def _dot3_left(lhs_bf16, x):
    hi, mid, lo = _split3(x)
    return _dot(lhs_bf16, hi) + _dot(lhs_bf16, mid) + _dot(lhs_bf16, lo)


SRC_XBC = D_SSM
SRC_DT = SRC_XBC + D_SSM + 2 * N_GROUPS * D_STATE
SRC_U = SRC_DT + N_HEADS
D_IN_PROJ = SRC_U + D_POOL


def _w_in_prep_kernel(w_ref, o_ref):
    o_ref[COL_Z:COL_Z + D_SSM, :] = w_ref[:D_SSM, :].astype(BF16)
    o_ref[COL_U:COL_U + D_POOL, :] = w_ref[SRC_U:SRC_U + D_POOL, :].astype(BF16)
    o_ref[COL_XS:D_CONV, :] = w_ref[SRC_XBC:SRC_DT, :].astype(BF16)
    zeros = jnp.zeros((V7X_LANES - N_REPLICAS * HEADS_PER_GROUP, o_ref.shape[1]), F32)
    for g in range(N_GROUPS):
        row0 = COL_DT + g * V7X_LANES
        src0 = SRC_DT + g * HEADS_PER_GROUP
        heads = w_ref[src0:src0 + HEADS_PER_GROUP, :]
        block = jnp.concatenate([heads] * N_REPLICAS + [zeros], axis=0)
        o_ref[row0:row0 + V7X_LANES, :] = block.astype(BF16)


def _w_in_prep(w_t, *, tk):
    n, k = w_t.shape
    assert n == D_IN_PROJ
    return pl.pallas_call(
        _w_in_prep_kernel,
        out_shape=jax.ShapeDtypeStruct((D_PROJ, k), BF16),
        grid=(k // tk,),
        in_specs=[pl.BlockSpec((n, tk), lambda i: (0, i))],
        out_specs=pl.BlockSpec((D_PROJ, tk), lambda i: (0, i)),
        compiler_params=pltpu.CompilerParams(
            dimension_semantics=("parallel",),
            vmem_limit_bytes=V7X_VMEM_LIMIT_BYTES),
        name="w_in_prep",
    )(w_t)


def _side_cast_specs(weights, n_slabs, step_of):
    specs = []
    for w in weights:
        rows = w.shape[0] // n_slabs
        assert rows * n_slabs == w.shape[0] and rows % (2 * V7X_SUBLANES) == 0
        specs.append(pl.BlockSpec(
            (rows, w.shape[1]), lambda *g: (jnp.minimum(step_of(*g), n_slabs - 1), 0)))
    return specs


CONV_CHUNK = 256


def _conv_silu(r, halo, w_ref, b_ref, cols):
    ext = jnp.concatenate([halo, r], axis=0)
    acc = r * w_ref[CONV_WIDTH - 1:CONV_WIDTH, cols] + b_ref[:, cols]
    for back in range(1, CONV_WIDTH):
        tap = w_ref[CONV_WIDTH - 1 - back:CONV_WIDTH - back, cols]
        acc = acc + pltpu.roll(ext, back, axis=0)[HALO_ROWS:] * tap
    return _silu(acc)


def _in_proj_kernel(x_ref, nw_ref, w_ref, cw_ref, cb_ref, *rest, n_conv_blocks, n_side):
    side_in, rest = rest[:n_side], rest[n_side:]
    o_ref, side_out = rest[0], rest[1:1 + n_side]
    hn_ref, halo_ref, raw_ref = rest[1 + n_side:]
    i = pl.program_id(0)
    j = pl.program_id(1)
    tm, tn = o_ref.shape

    for src_ref, dst_ref in zip(side_in, side_out):
        dst_ref[...] = src_ref[...].astype(BF16)

    @pl.when(j == 0)
    def _():
        hn_ref[...] = _rms_normalize(x_ref[...], nw_ref[...]).astype(BF16)

    @pl.when((i == 0) & (j == 0))
    def _():
        halo_ref[...] = jnp.zeros_like(halo_ref)

    @pl.when(j < n_conv_blocks)
    def _():
        n_pieces = tn // CONV_CHUNK
        piece = lambda c: slice(c * CONV_CHUNK, (c + 1) * CONV_CHUNK)
        slot = lambda c: lax.rem(j + c, 2)
        raw_ref[slot(0)] = _dot_nt(hn_ref[...], w_ref[piece(0), :])
        for c in range(n_pieces):
            cols = piece(c)
            if c + 1 < n_pieces:
                raw_ref[slot(c + 1)] = _dot_nt(hn_ref[...], w_ref[piece(c + 1), :])
            r = raw_ref[slot(c)]
            o_ref[:, cols] = _conv_silu(r, halo_ref[j, :, cols], cw_ref, cb_ref, cols)
            halo_ref[j, :, cols] = r[tm - HALO_ROWS:]

    @pl.when(j >= n_conv_blocks)
    def _():
        o_ref[...] = _dot_nt(hn_ref[...], w_ref[...])


def _in_proj(x, nw, w_t, conv_w, conv_b, side_weights, *, tm, tn, side_slabs):
    m, k = x.shape
    n = w_t.shape[0]
    assert D_CONV % tn == 0 and tn % CONV_CHUNK == 0
    n_conv_blocks = D_CONV // tn
    n_col_blocks = n // tn
    assert (m // tm) * n_col_blocks >= side_slabs
    conv_block = lambda i, j: (0, jnp.minimum(j, n_conv_blocks - 1))
    side_specs = _side_cast_specs(side_weights, side_slabs, lambda i, j: i * n_col_blocks + j)
    return pl.pallas_call(
        functools.partial(_in_proj_kernel, n_conv_blocks=n_conv_blocks, n_side=len(side_weights)),
        out_shape=(jax.ShapeDtypeStruct((m, n), F32),
                   *[jax.ShapeDtypeStruct(w.shape, BF16) for w in side_weights]),
        grid=(m // tm, n_col_blocks),
        in_specs=[
            pl.BlockSpec((tm, k), lambda i, j: (i, 0)),
            pl.BlockSpec((1, k), lambda i, j: (0, 0)),
            pl.BlockSpec((tn, k), lambda i, j: (j, 0)),
            pl.BlockSpec((CONV_WIDTH, tn), conv_block),
            pl.BlockSpec((1, tn), conv_block),
        ] + side_specs,
        out_specs=(pl.BlockSpec((tm, tn), lambda i, j: (i, j)), *side_specs),
        scratch_shapes=[pltpu.VMEM((tm, k), BF16),
                        pltpu.VMEM((n_conv_blocks, HALO_ROWS, tn), F32),
                        pltpu.VMEM((2, tm, CONV_CHUNK), F32)],
        compiler_params=pltpu.CompilerParams(
            dimension_semantics=("arbitrary", "arbitrary"),
            vmem_limit_bytes=V7X_VMEM_LIMIT_BYTES),
        name="in_proj",
    )(x, nw, w_t, conv_w, conv_b, *side_weights)


def _swiglu_kernel(hn_ref, wg_ref, wu_ref, o_ref):
    hn = hn_ref[...]
    gate = _dot(hn, wg_ref[...])
    up = _dot(hn, wu_ref[...])
    o_ref[...] = (_silu(gate) * up).astype(BF16)


def _swiglu(hn, wg, wu, *, tm, tn):
    m, k = hn.shape
    n = wg.shape[1]
    return pl.pallas_call(
        _swiglu_kernel,
        out_shape=jax.ShapeDtypeStruct((m, n), BF16),
        grid=(n // tn, m // tm),
        in_specs=[
            pl.BlockSpec((tm, k), lambda j, i: (i, 0)),
            pl.BlockSpec((k, tn), lambda j, i: (0, j)),
            pl.BlockSpec((k, tn), lambda j, i: (0, j)),
        ],
        out_specs=pl.BlockSpec((tm, tn), lambda j, i: (i, j)),
        compiler_params=pltpu.CompilerParams(
            dimension_semantics=("parallel", "parallel"),
            vmem_limit_bytes=V7X_VMEM_LIMIT_BYTES),
        name="ffn_up",
    )(hn, wg, wu)


def _out_proj_kernel(ys_ref, yp_ref, w_ref, x_ref, nw_ref, h_ref, hn_ref):
    k = ys_ref.shape[1]
    h = x_ref[...] + _dot(ys_ref[...], w_ref[:k, :]) + _dot(yp_ref[...], w_ref[k:, :])
    h_ref[...] = h
    hn_ref[...] = _rms_normalize(h, nw_ref[...]).astype(BF16)


def _resident(shape):
    return pl.BlockSpec(shape, lambda i: (0,) * len(shape), pipeline_mode=pl.Buffered(1))


def _out_proj(ys, yp, w, x, nw, *, tm):
    m, k = ys.shape
    n = w.shape[1]
    row_spec = lambda width: pl.BlockSpec((tm, width), lambda i: (i, 0))
    return pl.pallas_call(
        _out_proj_kernel,
        out_shape=(jax.ShapeDtypeStruct((m, n), F32), jax.ShapeDtypeStruct((m, n), BF16)),
        grid=(m // tm,),
        in_specs=[row_spec(k), row_spec(k), _resident(w.shape), row_spec(n), _resident((1, n))],
        out_specs=(row_spec(n), row_spec(n)),
        compiler_params=pltpu.CompilerParams(
            dimension_semantics=("parallel",),
            vmem_limit_bytes=V7X_VMEM_LIMIT_BYTES),
        name="out_proj",
    )(ys, yp, w, x, nw)


def _down_kernel(a_ref, w_ref, h_ref, nw_ref, o_ref, *, final_norm):
    h = h_ref[...] + _dot(a_ref[...], w_ref[...])
    o_ref[...] = _rms_normalize(h, nw_ref[...]) if final_norm else h


def _down_proj(act, w, h, nw, *, tm, final_norm):
    m, k = act.shape
    n = w.shape[1]
    row_spec = lambda width: pl.BlockSpec((tm, width), lambda i: (i, 0))
    return pl.pallas_call(
        functools.partial(_down_kernel, final_norm=final_norm),
        out_shape=jax.ShapeDtypeStruct((m, n), F32),
        grid=(m // tm,),
        in_specs=[row_spec(k), _resident(w.shape), row_spec(n), _resident((1, n))],
        out_specs=row_spec(n),
        compiler_params=pltpu.CompilerParams(
            dimension_semantics=("parallel",),
            vmem_limit_bytes=V7X_VMEM_LIMIT_BYTES),
        name="ffn_down",
    )(act, w, h, nw)


def _pool_kernel(*refs, tl):
    n_groups = len(POOL_WINDOWS)
    u_refs, halo_refs = refs[:n_groups], refs[n_groups:2 * n_groups]
    pw_ref, ps_ref, o_ref = refs[2 * n_groups:]
    i = pl.program_id(0)
    t = i * tl + lax.broadcasted_iota(jnp.int32, (tl, POOL_GROUP_DIM), 0)
    for g, window in enumerate(POOL_WINDOWS):
        cols = slice(g * POOL_GROUP_DIM, (g + 1) * POOL_GROUP_DIM)
        u = u_refs[g][...]
        halo = jnp.where(i == 0, 0.0, halo_refs[g][...])
        s = jnp.concatenate([halo, u], axis=0)
        shift = 1
        while shift < window:
            s = s + pltpu.roll(s, shift, axis=0)
            shift *= 2
        count = jnp.minimum(t + 1, window).astype(F32)
        pooled = s[POOL_HALO:] / count - u
        y = _dot(pooled.astype(BF16), pw_ref[g].astype(BF16)) * ps_ref[:, cols]
        o_ref[:, cols] = y.astype(BF16)


def _pool_mixer(proj, pool_w, pool_scale, *, tl):
    m = proj.shape[0]
    n_groups = len(POOL_WINDOWS)
    gd = POOL_GROUP_DIM
    assert COL_U % gd == 0 and tl % POOL_HALO == 0
    halo_blocks_per_tile = tl // POOL_HALO
    u_specs = [pl.BlockSpec((tl, gd), lambda i, g=g: (i, COL_U // gd + g)) for g in range(n_groups)]
    halo_specs = [
        pl.BlockSpec((POOL_HALO, gd),
                     lambda i, g=g: (jnp.maximum(i * halo_blocks_per_tile - 1, 0), COL_U // gd + g))
        for g in range(n_groups)]
    return pl.pallas_call(
        functools.partial(_pool_kernel, tl=tl),
        out_shape=jax.ShapeDtypeStruct((m, D_POOL), BF16),
        grid=(m // tl,),
        in_specs=u_specs + halo_specs + [
            pl.BlockSpec((n_groups, gd, gd), lambda i: (0, 0, 0)),
            pl.BlockSpec((1, D_POOL), lambda i: (0, 0)),
        ],
        out_specs=pl.BlockSpec((tl, D_POOL), lambda i: (i, 0)),
        compiler_params=pltpu.CompilerParams(
            dimension_semantics=("parallel",),
            vmem_limit_bytes=V7X_VMEM_LIMIT_BYTES),
        name="pool_mixer",
    )(*([proj] * (2 * n_groups)), pool_w, pool_scale)


assert N_REPLICAS * HEADS_PER_GROUP <= V7X_LANES


def _expansion_matrix():
    lane = jnp.arange(V7X_LANES)
    col = jnp.arange(N_EXPANDED * GROUP_WIDTH)
    replica, head = lane // HEADS_PER_GROUP, lane % HEADS_PER_GROUP
    quantity = replica // 3
    hit = ((quantity[:, None] == col[None, :] // GROUP_WIDTH)
           & (head[:, None] == (col[None, :] % GROUP_WIDTH) // HEAD_DIM)
           & (replica[:, None] < N_REPLICAS))
    return hit.astype(BF16)


def _ssd_group(z, xs, bm, cm, dt_raw, dtb, alog, dskip, nw, state_ref, expand, tril, causal):
    q = xs.shape[0]
    x = dt_raw + dtb
    dt = jnp.maximum(x, 0.0) + jnp.log1p(jnp.exp(-jnp.abs(x)))
    a = -jnp.exp(alog)
    a_cum = _dot3_left(tril, dt * a) * LOG2_E
    a_cum_t = a_cum.T
    a_last = a_cum[q - 1:q, :]

    replica = lax.broadcasted_iota(jnp.int32, (1, V7X_LANES), 1) // HEADS_PER_GROUP
    packed = jnp.where(replica < 3, dt,
                       jnp.where(replica < 6, jnp.exp2(a_cum), jnp.exp2(a_last - a_cum)))
    hi = packed.astype(BF16)
    r1 = packed - hi.astype(F32)
    r2 = r1 - r1.astype(BF16).astype(F32)
    level = replica % 3
    split = jnp.where(level == 0, packed, jnp.where(level == 1, r1, r2)).astype(BF16)
    expanded = _dot(split, expand)
    dt_b = expanded[:, :GROUP_WIDTH]
    decay_in_b = expanded[:, GROUP_WIDTH:2 * GROUP_WIDTH]
    decay_out_b = expanded[:, 2 * GROUP_WIDTH:]

    xdt = xs * dt_b
    bm_bf = bm.astype(BF16)
    cm_bf = cm.astype(BF16)
    cb = _dot_nt(cm_bf, bm_bf)

    xdt_bf = xdt.astype(BF16)
    lane = lax.broadcasted_iota(jnp.int32, (1, V7X_LANES), 1)
    keep = [jnp.where((lane // HEAD_DIM) == side, 1.0, 0.0).astype(BF16) for side in range(2)]
    pairs = []
    for pair in range(HEADS_PER_GROUP // 2):
        xdt_pair = xdt_bf[:, pair * V7X_LANES:(pair + 1) * V7X_LANES]
        acc = jnp.zeros((q, V7X_LANES), F32)
        for side in range(2):
            r = 2 * pair + side
            seg = a_cum[:, r:r + 1] - a_cum_t[r:r + 1, :]
            decay = jnp.exp2(jnp.where(causal, seg, -jnp.inf))
            m_r = (cb * decay).astype(BF16)
            acc = acc + _dot(m_r, xdt_pair * keep[side])
        pairs.append(acc)
    y_diag = jnp.concatenate(pairs, axis=-1)

    state = state_ref[...]
    y_off = _dot(cm_bf, state.astype(BF16)) * decay_in_b
    xw = (xdt * decay_out_b).astype(BF16)
    new_state = lax.dot_general(bm_bf, xw, (((0,), (0,)), ((), ())), preferred_element_type=F32)
    state_ref[...] = state * decay_in_b[q - 1:q, :] + new_state

    y = y_diag + y_off + dskip * xs
    g = y * _silu(z)
    return _rms_normalize(g, nw).astype(BF16)


def _ssd_kernel(*refs, groups, n_side):
    z_refs = refs[:groups]
    (xs_ref, b_ref, c_ref, dt_ref, dtb_ref, alog_ref, dskip_ref, nw_ref,
     expand_ref) = refs[groups:groups + 9]
    side_in = refs[groups + 9:groups + 9 + n_side]
    o_ref = refs[groups + 9 + n_side]
    side_out = refs[groups + 10 + n_side:groups + 10 + 2 * n_side]
    state_ref = refs[-1]
    q = SSD_CHUNK

    for src_ref, dst_ref in zip(side_in, side_out):
        dst_ref[...] = src_ref[...].astype(BF16)

    @pl.when(pl.program_id(1) == 0)
    def _():
        state_ref[...] = jnp.zeros_like(state_ref)

    row_i = lax.broadcasted_iota(jnp.int32, (q, q), 0)
    col_i = lax.broadcasted_iota(jnp.int32, (q, q), 1)
    causal = row_i >= col_i
    tril = jnp.where(causal, 1.0, 0.0).astype(BF16)
    expand = expand_ref[...]
    for g in range(groups):
        wide = slice(g * GROUP_WIDTH, (g + 1) * GROUP_WIDTH)
        narrow = slice(g * D_STATE, (g + 1) * D_STATE)
        o_ref[:, wide] = _ssd_group(
            z_refs[g][...], xs_ref[:, wide], b_ref[:, narrow], c_ref[:, narrow], dt_ref[:, narrow],
            dtb_ref[:, narrow], alog_ref[:, narrow], dskip_ref[:, wide], nw_ref[:, wide],
            state_ref.at[g], expand, tril, causal)


def _ssd_mixer(proj, dtb_c, alog_c, dskip_full, norm_w, side_weights, *, groups):
    m = proj.shape[0]
    q = SSD_CHUNK
    gw = groups * GROUP_WIDTH
    ds = groups * D_STATE
    assert V7X_LANES == D_STATE and N_GROUPS % groups == 0

    def col(start, width):
        assert start % width == 0
        return start // width

    row_spec = lambda width, block0: pl.BlockSpec((q, width), lambda g, c: (c, block0 + g))
    par_spec = lambda width: pl.BlockSpec((1, width), lambda g, c: (0, g))
    z_specs = [
        pl.BlockSpec((q, GROUP_WIDTH),
                     lambda g, c, k=k: (c, col(COL_Z, GROUP_WIDTH) + g * groups + k))
        for k in range(groups)]
    expand = _expansion_matrix()
    assert groups == N_GROUPS or not side_weights
    side_specs = _side_cast_specs(side_weights, m // q, lambda g, c: c)
    return pl.pallas_call(
        functools.partial(_ssd_kernel, groups=groups, n_side=len(side_weights)),
        out_shape=(jax.ShapeDtypeStruct((m, D_SSM), BF16),
                   *[jax.ShapeDtypeStruct(w.shape, BF16) for w in side_weights]),
        grid=(N_GROUPS // groups, m // q),
        in_specs=z_specs + [
            row_spec(gw, col(COL_XS, gw)),
            row_spec(ds, col(COL_B, ds)),
            row_spec(ds, col(COL_C, ds)),
            row_spec(ds, col(COL_DT, ds)),
            par_spec(ds),
            par_spec(ds),
            par_spec(gw),
            par_spec(gw),
            pl.BlockSpec(expand.shape, lambda g, c: (0, 0)),
        ] + side_specs,
        out_specs=(pl.BlockSpec((q, gw), lambda g, c: (c, g)), *side_specs),
        scratch_shapes=[pltpu.VMEM((groups, D_STATE, GROUP_WIDTH), F32)],
        compiler_params=pltpu.CompilerParams(
            dimension_semantics=("parallel", "arbitrary"),
            vmem_limit_bytes=V7X_VMEM_LIMIT_BYTES),
        name="ssd_mixer",
    )(*([proj] * (groups + 4)), dtb_c, alog_c, dskip_full, norm_w, expand, *side_weights)


def _pad_heads_to_lanes(v):
    v = jnp.tile(v.reshape(N_GROUPS, HEADS_PER_GROUP), (1, N_REPLICAS))
    v = jnp.pad(v, ((0, 0), (0, V7X_LANES - N_REPLICAS * HEADS_PER_GROUP)))
    return v.reshape(1, N_GROUPS * V7X_LANES)


def kernel(x, attn_norm_w, w_in, conv_w, conv_b, dt_bias, a_log, d_skip, ssd_norm_w, pool_w,
           pool_scale, w_out, ffn_norm_w, w_gate, w_up, w_down, final_norm_w):
    bsz, seqlen, d_model = x.shape
    depth = w_in.shape[0]
    h = x.reshape(bsz * seqlen, d_model)
    assert bsz == 1 and seqlen % 2048 == 0 and d_model == D_MODEL

    for i in range(depth):
        w_cat = _w_in_prep(w_in[i].T, tk=256)
        proj, w_out_bf, w_down_bf = _in_proj(
            h, attn_norm_w[i][None, :], w_cat, conv_w[i], conv_b[i][None, :],
            (w_out[i], w_down[i]), tm=1024, tn=1536, side_slabs=32)

        y_ssd, w_gate_bf, w_up_bf = _ssd_mixer(
            proj, _pad_heads_to_lanes(dt_bias[i]), _pad_heads_to_lanes(a_log[i]),
            jnp.repeat(d_skip[i], HEAD_DIM)[None, :], ssd_norm_w[i][None, :],
            (w_gate[i], w_up[i]), groups=N_GROUPS)
        y_pool = _pool_mixer(proj, pool_w[i], pool_scale[i][None, :], tl=1024)

        h, hn = _out_proj(y_ssd, y_pool, w_out_bf, h, ffn_norm_w[i][None, :], tm=512)

        act = _swiglu(hn, w_gate_bf, w_up_bf, tm=2048, tn=512)
        last = i == depth - 1
        h = _down_proj(act, w_down_bf, h, final_norm_w[None, :], tm=512, final_norm=last)

    return h.reshape(bsz, seqlen, d_model)
```

```python
import functools

import jax
import jax.numpy as jnp
from jax import lax
from jax.experimental import pallas as pl
from jax.experimental.pallas import tpu as pltpu

F32 = jnp.float32
BF16 = jnp.bfloat16

NORM_EPS = 1e-5
LOG2_E = 1.4426950408889634
D_MODEL = 2048
D_SSM = 2048
HEAD_DIM = 64
N_HEADS = D_SSM // HEAD_DIM
N_GROUPS = 4
HEADS_PER_GROUP = N_HEADS // N_GROUPS
GROUP_WIDTH = D_SSM // N_GROUPS
D_STATE = 128
CONV_WIDTH = 4
D_POOL = 2048
POOL_WINDOWS = (2, 4, 8, 16)
POOL_GROUP_DIM = D_POOL // len(POOL_WINDOWS)
D_MIX = D_SSM + D_POOL

V7X_LANES = 128
V7X_SUBLANES = 8
V7X_VMEM_LIMIT_BYTES = 60 * 1024 * 1024

COL_XS = 0
COL_B = COL_XS + D_SSM
COL_C = COL_B + N_GROUPS * D_STATE
D_CONV = COL_C + N_GROUPS * D_STATE
COL_Z = D_CONV
COL_U = COL_Z + D_SSM
COL_DT = COL_U + D_POOL
D_PROJ = COL_DT + N_GROUPS * V7X_LANES

SSD_CHUNK = 256
N_EXPANDED = 3
N_REPLICAS = 3 * N_EXPANDED
HALO_ROWS = V7X_SUBLANES
POOL_HALO = 16


def _sigmoid(x):
    return 1.0 / (1.0 + jnp.exp(-x))


def _silu(x):
    return x * _sigmoid(x)


def _rms_normalize(x, w):
    ms = jnp.mean(x * x, axis=-1, keepdims=True)
    return x * lax.rsqrt(ms + NORM_EPS) * w


def _split3(x):
    hi = x.astype(BF16)
    r1 = x - hi.astype(F32)
    mid = r1.astype(BF16)
    lo = (r1 - mid.astype(F32)).astype(BF16)
    return hi, mid, lo


def _dot(a, b):
    return jnp.dot(a, b, preferred_element_type=F32)


def _dot_nt(a, b):
    return lax.dot_general(a, b, (((1,), (1,)), ((), ())), preferred_element_type=F32)


def _dot3_left(lhs_bf16, x):
    hi, mid, lo = _split3(x)
    return _dot(lhs_bf16, hi) + _dot(lhs_bf16, mid) + _dot(lhs_bf16, lo)


SRC_XBC = D_SSM
SRC_DT = SRC_XBC + D_SSM + 2 * N_GROUPS * D_STATE
SRC_U = SRC_DT + N_HEADS
D_IN_PROJ = SRC_U + D_POOL


def _w_in_prep_kernel(w_ref, o_ref):
    o_ref[COL_Z:COL_Z + D_SSM, :] = w_ref[:D_SSM, :].astype(BF16)
    o_ref[COL_U:COL_U + D_POOL, :] = w_ref[SRC_U:SRC_U + D_POOL, :].astype(BF16)
    o_ref[COL_XS:D_CONV, :] = w_ref[SRC_XBC:SRC_DT, :].astype(BF16)
    zeros = jnp.zeros((V7X_LANES - N_REPLICAS * HEADS_PER_GROUP, o_ref.shape[1]), F32)
    for g in range(N_GROUPS):
        row0 = COL_DT + g * V7X_LANES
        src0 = SRC_DT + g * HEADS_PER_GROUP
        heads = w_ref[src0:src0 + HEADS_PER_GROUP, :]
        block = jnp.concatenate([heads] * N_REPLICAS + [zeros], axis=0)
        o_ref[row0:row0 + V7X_LANES, :] = block.astype(BF16)


def _w_in_prep(w_t, *, tk):
    n, k = w_t.shape
    assert n == D_IN_PROJ
    return pl.pallas_call(
        _w_in_prep_kernel,
        out_shape=jax.ShapeDtypeStruct((D_PROJ, k), BF16),
        grid=(k // tk,),
        in_specs=[pl.BlockSpec((n, tk), lambda i: (0, i))],
        out_specs=pl.BlockSpec((D_PROJ, tk), lambda i: (0, i)),
        compiler_params=pltpu.CompilerParams(
            dimension_semantics=("parallel",),
            vmem_limit_bytes=V7X_VMEM_LIMIT_BYTES),
        name="w_in_prep",
    )(w_t)


def _side_cast_specs(weights, n_slabs, step_of):
    specs = []
    for w in weights:
        rows = w.shape[0] // n_slabs
        assert rows * n_slabs == w.shape[0] and rows % (2 * V7X_SUBLANES) == 0
        specs.append(pl.BlockSpec(
            (rows, w.shape[1]), lambda *g: (jnp.minimum(step_of(*g), n_slabs - 1), 0)))
    return specs


CONV_CHUNK = 256


def _conv_silu(r, halo, w_ref, b_ref, cols):
    ext = jnp.concatenate([halo, r], axis=0)
    acc = r * w_ref[CONV_WIDTH - 1:CONV_WIDTH, cols] + b_ref[:, cols]
    for back in range(1, CONV_WIDTH):
        tap = w_ref[CONV_WIDTH - 1 - back:CONV_WIDTH - back, cols]
        acc = acc + pltpu.roll(ext, back, axis=0)[HALO_ROWS:] * tap
    return _silu(acc)


def _in_proj_kernel(x_ref, nw_ref, w_ref, cw_ref, cb_ref, *rest, n_conv_blocks, n_side):
    side_in, rest = rest[:n_side], rest[n_side:]
    o_ref, side_out = rest[0], rest[1:1 + n_side]
    hn_ref, halo_ref, raw_ref = rest[1 + n_side:]
    i = pl.program_id(0)
    j = pl.program_id(1)
    tm, tn = o_ref.shape

    for src_ref, dst_ref in zip(side_in, side_out):
        dst_ref[...] = src_ref[...].astype(BF16)

    @pl.when(j == 0)
    def _():
        hn_ref[...] = _rms_normalize(x_ref[...], nw_ref[...]).astype(BF16)

    @pl.when((i == 0) & (j == 0))
    def _():
        halo_ref[...] = jnp.zeros_like(halo_ref)

    @pl.when(j < n_conv_blocks)
    def _():
        n_pieces = tn // CONV_CHUNK
        piece = lambda c: slice(c * CONV_CHUNK, (c + 1) * CONV_CHUNK)
        slot = lambda c: lax.rem(j + c, 2)
        raw_ref[slot(0)] = _dot_nt(hn_ref[...], w_ref[piece(0), :])
        for c in range(n_pieces):
            cols = piece(c)
            if c + 1 < n_pieces:
                raw_ref[slot(c + 1)] = _dot_nt(hn_ref[...], w_ref[piece(c + 1), :])
            r = raw_ref[slot(c)]
            o_ref[:, cols] = _conv_silu(r, halo_ref[j, :, cols], cw_ref, cb_ref, cols)
            halo_ref[j, :, cols] = r[tm - HALO_ROWS:]

    @pl.when(j >= n_conv_blocks)
    def _():
        o_ref[...] = _dot_nt(hn_ref[...], w_ref[...])


def _in_proj(x, nw, w_t, conv_w, conv_b, side_weights, *, tm, tn, side_slabs):
    m, k = x.shape
    n = w_t.shape[0]
    assert D_CONV % tn == 0 and tn % CONV_CHUNK == 0
    n_conv_blocks = D_CONV // tn
    n_col_blocks = n // tn
    assert (m // tm) * n_col_blocks >= side_slabs
    conv_block = lambda i, j: (0, jnp.minimum(j, n_conv_blocks - 1))
    side_specs = _side_cast_specs(side_weights, side_slabs, lambda i, j: i * n_col_blocks + j)
    return pl.pallas_call(
        functools.partial(_in_proj_kernel, n_conv_blocks=n_conv_blocks, n_side=len(side_weights)),
        out_shape=(jax.ShapeDtypeStruct((m, n), F32),
                   *[jax.ShapeDtypeStruct(w.shape, BF16) for w in side_weights]),
        grid=(m // tm, n_col_blocks),
        in_specs=[
            pl.BlockSpec((tm, k), lambda i, j: (i, 0)),
            pl.BlockSpec((1, k), lambda i, j: (0, 0)),
            pl.BlockSpec((tn, k), lambda i, j: (j, 0)),
            pl.BlockSpec((CONV_WIDTH, tn), conv_block),
            pl.BlockSpec((1, tn), conv_block),
        ] + side_specs,
        out_specs=(pl.BlockSpec((tm, tn), lambda i, j: (i, j)), *side_specs),
        scratch_shapes=[pltpu.VMEM((tm, k), BF16),
                        pltpu.VMEM((n_conv_blocks, HALO_ROWS, tn), F32),
                        pltpu.VMEM((2, tm, CONV_CHUNK), F32)],
        compiler_params=pltpu.CompilerParams(
            dimension_semantics=("arbitrary", "arbitrary"),
            vmem_limit_bytes=V7X_VMEM_LIMIT_BYTES),
        name="in_proj",
    )(x, nw, w_t, conv_w, conv_b, *side_weights)


def _swiglu_kernel(hn_ref, wg_ref, wu_ref, *rest):
    n_side = (len(rest) - 1) // 2
    side_in, o_ref, side_out = rest[:n_side], rest[n_side], rest[n_side + 1:]
    for src_ref, dst_ref in zip(side_in, side_out):
        dst_ref[...] = src_ref[...].astype(BF16)
    half = o_ref.shape[1] // 2
    for cols in (slice(0, half), slice(half, 2 * half)):
        gate = _dot(hn_ref[...], wg_ref[:, cols])
        up = _dot(hn_ref[...], wu_ref[:, cols])
        o_ref[:, cols] = (_silu(gate) * up).astype(BF16)


def _swiglu(hn, wg, wu, side_weights, *, tm, tn):
    m, k = hn.shape
    n = wg.shape[1]
    n_row_tiles = m // tm
    side_specs = _side_cast_specs(side_weights, (n // tn) * n_row_tiles,
                                  lambda j, i: j * n_row_tiles + i)
    return pl.pallas_call(
        _swiglu_kernel,
        out_shape=(jax.ShapeDtypeStruct((m, n), BF16),
                   *[jax.ShapeDtypeStruct(w.shape, BF16) for w in side_weights]),
        grid=(n // tn, n_row_tiles),
        in_specs=[
            pl.BlockSpec((tm, k), lambda j, i: (i, 0)),
            pl.BlockSpec((k, tn), lambda j, i: (0, j)),
            pl.BlockSpec((k, tn), lambda j, i: (0, j)),
        ] + side_specs,
        out_specs=(pl.BlockSpec((tm, tn), lambda j, i: (i, j)), *side_specs),
        compiler_params=pltpu.CompilerParams(
            dimension_semantics=("arbitrary", "arbitrary"),
            vmem_limit_bytes=V7X_VMEM_LIMIT_BYTES),
        name="ffn_up",
    )(hn, wg, wu, *side_weights)


def _out_proj_kernel(ys_ref, yp_ref, w_ref, x_ref, nw_ref, h_ref, hn_ref):
    k = ys_ref.shape[1]
    h = x_ref[...] + _dot(ys_ref[...], w_ref[:k, :]) + _dot(yp_ref[...], w_ref[k:, :])
    h_ref[...] = h
    hn_ref[...] = _rms_normalize(h, nw_ref[...]).astype(BF16)


def _resident(shape):
    return pl.BlockSpec(shape, lambda i: (0,) * len(shape), pipeline_mode=pl.Buffered(1))


def _out_proj(ys, yp, w, x, nw, *, tm):
    m, k = ys.shape
    n = w.shape[1]
    row_spec = lambda width: pl.BlockSpec((tm, width), lambda i: (i, 0))
    return pl.pallas_call(
        _out_proj_kernel,
        out_shape=(jax.ShapeDtypeStruct((m, n), F32), jax.ShapeDtypeStruct((m, n), BF16)),
        grid=(m // tm,),
        in_specs=[row_spec(k), row_spec(k), _resident(w.shape), row_spec(n), _resident((1, n))],
        out_specs=(row_spec(n), row_spec(n)),
        compiler_params=pltpu.CompilerParams(
            dimension_semantics=("parallel",),
            vmem_limit_bytes=V7X_VMEM_LIMIT_BYTES),
        name="out_proj",
    )(ys, yp, w, x, nw)


def _down_kernel(a_ref, w_ref, h_ref, nw_ref, o_ref, *, final_norm):
    h = h_ref[...] + _dot(a_ref[...], w_ref[...])
    o_ref[...] = _rms_normalize(h, nw_ref[...]) if final_norm else h


def _down_proj(act, w, h, nw, *, tm, final_norm):
    m, k = act.shape
    n = w.shape[1]
    row_spec = lambda width: pl.BlockSpec((tm, width), lambda i: (i, 0))
    return pl.pallas_call(
        functools.partial(_down_kernel, final_norm=final_norm),
        out_shape=jax.ShapeDtypeStruct((m, n), F32),
        grid=(m // tm,),
        in_specs=[row_spec(k), _resident(w.shape), row_spec(n), _resident((1, n))],
        out_specs=row_spec(n),
        compiler_params=pltpu.CompilerParams(
            dimension_semantics=("parallel",),
            vmem_limit_bytes=V7X_VMEM_LIMIT_BYTES),
        name="ffn_down",
    )(act, w, h, nw)


def _pool_kernel(*refs, tl):
    n_groups = len(POOL_WINDOWS)
    u_refs, halo_refs = refs[:n_groups], refs[n_groups:2 * n_groups]
    pw_ref, ps_ref, o_ref = refs[2 * n_groups:]
    i = pl.program_id(0)
    t = i * tl + lax.broadcasted_iota(jnp.int32, (tl, POOL_GROUP_DIM), 0)
    for g, window in enumerate(POOL_WINDOWS):
        cols = slice(g * POOL_GROUP_DIM, (g + 1) * POOL_GROUP_DIM)
        u = u_refs[g][...]
        halo = jnp.where(i == 0, 0.0, halo_refs[g][...])
        s = jnp.concatenate([halo, u], axis=0)
        shift = 1
        while shift < window:
            s = s + pltpu.roll(s, shift, axis=0)
            shift *= 2
        count = jnp.minimum(t + 1, window).astype(F32)
        pooled = s[POOL_HALO:] / count - u
        y = _dot(pooled.astype(BF16), pw_ref[g].astype(BF16)) * ps_ref[:, cols]
        o_ref[:, cols] = y.astype(BF16)


def _pool_mixer(proj, pool_w, pool_scale, *, tl):
    m = proj.shape[0]
    n_groups = len(POOL_WINDOWS)
    gd = POOL_GROUP_DIM
    assert COL_U % gd == 0 and tl % POOL_HALO == 0
    halo_blocks_per_tile = tl // POOL_HALO
    u_specs = [pl.BlockSpec((tl, gd), lambda i, g=g: (i, COL_U // gd + g)) for g in range(n_groups)]
    halo_specs = [
        pl.BlockSpec((POOL_HALO, gd),
                     lambda i, g=g: (jnp.maximum(i * halo_blocks_per_tile - 1, 0), COL_U // gd + g))
        for g in range(n_groups)]
    return pl.pallas_call(
        functools.partial(_pool_kernel, tl=tl),
        out_shape=jax.ShapeDtypeStruct((m, D_POOL), BF16),
        grid=(m // tl,),
        in_specs=u_specs + halo_specs + [
            pl.BlockSpec((n_groups, gd, gd), lambda i: (0, 0, 0)),
            pl.BlockSpec((1, D_POOL), lambda i: (0, 0)),
        ],
        out_specs=pl.BlockSpec((tl, D_POOL), lambda i: (i, 0)),
        compiler_params=pltpu.CompilerParams(
            dimension_semantics=("parallel",),
            vmem_limit_bytes=V7X_VMEM_LIMIT_BYTES),
        name="pool_mixer",
    )(*([proj] * (2 * n_groups)), pool_w, pool_scale)


assert N_REPLICAS * HEADS_PER_GROUP <= V7X_LANES


def _expansion_matrix():
    lane = jnp.arange(V7X_LANES)
    col = jnp.arange(N_EXPANDED * GROUP_WIDTH)
    replica, head = lane // HEADS_PER_GROUP, lane % HEADS_PER_GROUP
    quantity = replica // 3
    hit = ((quantity[:, None] == col[None, :] // GROUP_WIDTH)
           & (head[:, None] == (col[None, :] % GROUP_WIDTH) // HEAD_DIM)
           & (replica[:, None] < N_REPLICAS))
    return hit.astype(BF16)


def _ssd_group(z, xs, bm, cm, dt_raw, dtb, alog, dskip, nw, state_ref, expand, tril, causal):
    q = xs.shape[0]
    x = dt_raw + dtb
    dt = jnp.maximum(x, 0.0) + jnp.log1p(jnp.exp(-jnp.abs(x)))
    a = -jnp.exp(alog)
    a_cum = _dot3_left(tril, dt * a) * LOG2_E
    a_cum_t = a_cum.T
    a_last = a_cum[q - 1:q, :]

    replica = lax.broadcasted_iota(jnp.int32, (1, V7X_LANES), 1) // HEADS_PER_GROUP
    packed = jnp.where(replica < 3, dt,
                       jnp.where(replica < 6, jnp.exp2(a_cum), jnp.exp2(a_last - a_cum)))
    hi = packed.astype(BF16)
    r1 = packed - hi.astype(F32)
    r2 = r1 - r1.astype(BF16).astype(F32)
    level = replica % 3
    split = jnp.where(level == 0, packed, jnp.where(level == 1, r1, r2)).astype(BF16)
    expanded = _dot(split, expand)
    dt_b = expanded[:, :GROUP_WIDTH]
    decay_in_b = expanded[:, GROUP_WIDTH:2 * GROUP_WIDTH]
    decay_out_b = expanded[:, 2 * GROUP_WIDTH:]

    xdt = xs * dt_b
    bm_bf = bm.astype(BF16)
    cm_bf = cm.astype(BF16)
    cb = _dot_nt(cm_bf, bm_bf)

    xdt_bf = xdt.astype(BF16)
    lane = lax.broadcasted_iota(jnp.int32, (1, V7X_LANES), 1)
    keep = [jnp.where((lane // HEAD_DIM) == side, 1.0, 0.0).astype(BF16) for side in range(2)]
    pairs = []
    for pair in range(HEADS_PER_GROUP // 2):
        xdt_pair = xdt_bf[:, pair * V7X_LANES:(pair + 1) * V7X_LANES]
        acc = jnp.zeros((q, V7X_LANES), F32)
        for side in range(2):
            r = 2 * pair + side
            seg = a_cum[:, r:r + 1] - a_cum_t[r:r + 1, :]
            decay = jnp.exp2(jnp.where(causal, seg, -jnp.inf))
            m_r = (cb * decay).astype(BF16)
            acc = acc + _dot(m_r, xdt_pair * keep[side])
        pairs.append(acc)
    y_diag = jnp.concatenate(pairs, axis=-1)

    state = state_ref[...]
    y_off = _dot(cm_bf, state.astype(BF16)) * decay_in_b
    xw = (xdt * decay_out_b).astype(BF16)
    new_state = lax.dot_general(bm_bf, xw, (((0,), (0,)), ((), ())), preferred_element_type=F32)
    state_ref[...] = state * decay_in_b[q - 1:q, :] + new_state

    y = y_diag + y_off + dskip * xs
    g = y * _silu(z)
    return _rms_normalize(g, nw).astype(BF16)


def _ssd_kernel(*refs, groups, n_side):
    z_refs = refs[:groups]
    (xs_ref, b_ref, c_ref, dt_ref, dtb_ref, alog_ref, dskip_ref, nw_ref,
     expand_ref) = refs[groups:groups + 9]
    side_in = refs[groups + 9:groups + 9 + n_side]
    o_ref = refs[groups + 9 + n_side]
    side_out = refs[groups + 10 + n_side:groups + 10 + 2 * n_side]
    state_ref = refs[-1]
    q = SSD_CHUNK

    for src_ref, dst_ref in zip(side_in, side_out):
        dst_ref[...] = src_ref[...].astype(BF16)

    @pl.when(pl.program_id(1) == 0)
    def _():
        state_ref[...] = jnp.zeros_like(state_ref)

    row_i = lax.broadcasted_iota(jnp.int32, (q, q), 0)
    col_i = lax.broadcasted_iota(jnp.int32, (q, q), 1)
    causal = row_i >= col_i
    tril = jnp.where(causal, 1.0, 0.0).astype(BF16)
    expand = expand_ref[...]
    for g in range(groups):
        wide = slice(g * GROUP_WIDTH, (g + 1) * GROUP_WIDTH)
        narrow = slice(g * D_STATE, (g + 1) * D_STATE)
        o_ref[:, wide] = _ssd_group(
            z_refs[g][...], xs_ref[:, wide], b_ref[:, narrow], c_ref[:, narrow], dt_ref[:, narrow],
            dtb_ref[:, narrow], alog_ref[:, narrow], dskip_ref[:, wide], nw_ref[:, wide],
            state_ref.at[g], expand, tril, causal)


def _ssd_mixer(proj, dtb_c, alog_c, dskip_full, norm_w, side_weights, *, groups):
    m = proj.shape[0]
    q = SSD_CHUNK
    gw = groups * GROUP_WIDTH
    ds = groups * D_STATE
    assert V7X_LANES == D_STATE and N_GROUPS % groups == 0

    def col(start, width):
        assert start % width == 0
        return start // width

    row_spec = lambda width, block0: pl.BlockSpec((q, width), lambda g, c: (c, block0 + g))
    par_spec = lambda width: pl.BlockSpec((1, width), lambda g, c: (0, g))
    z_specs = [
        pl.BlockSpec((q, GROUP_WIDTH),
                     lambda g, c, k=k: (c, col(COL_Z, GROUP_WIDTH) + g * groups + k))
        for k in range(groups)]
    expand = _expansion_matrix()
    assert groups == N_GROUPS or not side_weights
    side_specs = _side_cast_specs(side_weights, m // q, lambda g, c: c)
    return pl.pallas_call(
        functools.partial(_ssd_kernel, groups=groups, n_side=len(side_weights)),
        out_shape=(jax.ShapeDtypeStruct((m, D_SSM), BF16),
                   *[jax.ShapeDtypeStruct(w.shape, BF16) for w in side_weights]),
        grid=(N_GROUPS // groups, m // q),
        in_specs=z_specs + [
            row_spec(gw, col(COL_XS, gw)),
            row_spec(ds, col(COL_B, ds)),
            row_spec(ds, col(COL_C, ds)),
            row_spec(ds, col(COL_DT, ds)),
            par_spec(ds),
            par_spec(ds),
            par_spec(gw),
            par_spec(gw),
            pl.BlockSpec(expand.shape, lambda g, c: (0, 0)),
        ] + side_specs,
        out_specs=(pl.BlockSpec((q, gw), lambda g, c: (c, g)), *side_specs),
        scratch_shapes=[pltpu.VMEM((groups, D_STATE, GROUP_WIDTH), F32)],
        compiler_params=pltpu.CompilerParams(
            dimension_semantics=("parallel", "arbitrary"),
            vmem_limit_bytes=V7X_VMEM_LIMIT_BYTES),
        name="ssd_mixer",
    )(*([proj] * (groups + 4)), dtb_c, alog_c, dskip_full, norm_w, expand, *side_weights)


def _pad_heads_to_lanes(v):
    v = jnp.tile(v.reshape(N_GROUPS, HEADS_PER_GROUP), (1, N_REPLICAS))
    v = jnp.pad(v, ((0, 0), (0, V7X_LANES - N_REPLICAS * HEADS_PER_GROUP)))
    return v.reshape(1, N_GROUPS * V7X_LANES)


def kernel(x, attn_norm_w, w_in, conv_w, conv_b, dt_bias, a_log, d_skip, ssd_norm_w, pool_w,
           pool_scale, w_out, ffn_norm_w, w_gate, w_up, w_down, final_norm_w):
    bsz, seqlen, d_model = x.shape
    depth = w_in.shape[0]
    h = x.reshape(bsz * seqlen, d_model)
    assert bsz == 1 and seqlen % 2048 == 0 and d_model == D_MODEL

    for i in range(depth):
        w_cat = _w_in_prep(w_in[i].T, tk=256)
        proj, w_out_bf = _in_proj(
            h, attn_norm_w[i][None, :], w_cat, conv_w[i], conv_b[i][None, :],
            (w_out[i],), tm=1024, tn=1536, side_slabs=32)

        y_ssd, w_gate_bf, w_up_bf = _ssd_mixer(
            proj, _pad_heads_to_lanes(dt_bias[i]), _pad_heads_to_lanes(a_log[i]),
            jnp.repeat(d_skip[i], HEAD_DIM)[None, :], ssd_norm_w[i][None, :],
            (w_gate[i], w_up[i]), groups=N_GROUPS)
        y_pool = _pool_mixer(proj, pool_w[i], pool_scale[i][None, :], tl=1024)

        h, hn = _out_proj(y_ssd, y_pool, w_out_bf, h, ffn_norm_w[i][None, :], tm=512)

        act, w_down_bf = _swiglu(hn, w_gate_bf, w_up_bf, (w_down[i],), tm=2048, tn=512)
        last = i == depth - 1
        h = _down_proj(act, w_down_bf, h, final_norm_w[None, :], tm=512, final_norm=last)

    return h.reshape(bsz, seqlen, d_model)
```

```python
import functools

import jax
import jax.numpy as jnp
from jax import lax
from jax.experimental import pallas as pl
from jax.experimental.pallas import tpu as pltpu

F32 = jnp.float32
BF16 = jnp.bfloat16

NORM_EPS = 1e-5
LOG2_E = 1.4426950408889634
D_MODEL = 2048
D_SSM = 2048
HEAD_DIM = 64
N_HEADS = D_SSM // HEAD_DIM
N_GROUPS = 4
HEADS_PER_GROUP = N_HEADS // N_GROUPS
GROUP_WIDTH = D_SSM // N_GROUPS
D_STATE = 128
CONV_WIDTH = 4
D_POOL = 2048
POOL_WINDOWS = (2, 4, 8, 16)
POOL_GROUP_DIM = D_POOL // len(POOL_WINDOWS)
D_MIX = D_SSM + D_POOL

V7X_LANES = 128
V7X_SUBLANES = 8
V7X_VMEM_LIMIT_BYTES = 60 * 1024 * 1024

COL_XS = 0
COL_B = COL_XS + D_SSM
COL_C = COL_B + N_GROUPS * D_STATE
D_CONV = COL_C + N_GROUPS * D_STATE
COL_Z = D_CONV
COL_U = COL_Z + D_SSM
COL_DT = COL_U + D_POOL
D_PROJ = COL_DT + N_GROUPS * V7X_LANES

SSD_CHUNK = 256
N_EXPANDED = 3
N_REPLICAS = 3 * N_EXPANDED
HALO_ROWS = V7X_SUBLANES
POOL_HALO = 16


def _sigmoid(x):
    return 1.0 / (1.0 + jnp.exp(-x))


def _silu(x):
    return x * _sigmoid(x)


def _rms_normalize(x, w):
    ms = jnp.mean(x * x, axis=-1, keepdims=True)
    return x * lax.rsqrt(ms + NORM_EPS) * w


def _split3(x):
    hi = x.astype(BF16)
    r1 = x - hi.astype(F32)
    mid = r1.astype(BF16)
    lo = (r1 - mid.astype(F32)).astype(BF16)
    return hi, mid, lo


def _dot(a, b):
    return jnp.dot(a, b, preferred_element_type=F32)


def _dot_nt(a, b):
    return lax.dot_general(a, b, (((1,), (1,)), ((), ())), preferred_element_type=F32)


def _dot3_left(lhs_bf16, x):
    hi, mid, lo = _split3(x)
    return _dot(lhs_bf16, hi) + _dot(lhs_bf16, mid) + _dot(lhs_bf16, lo)


SRC_XBC = D_SSM
SRC_DT = SRC_XBC + D_SSM + 2 * N_GROUPS * D_STATE
SRC_U = SRC_DT + N_HEADS
D_IN_PROJ = SRC_U + D_POOL


def _w_in_prep_kernel(w_ref, o_ref):
    o_ref[COL_Z:COL_Z + D_SSM, :] = w_ref[:D_SSM, :].astype(BF16)
    o_ref[COL_U:COL_U + D_POOL, :] = w_ref[SRC_U:SRC_U + D_POOL, :].astype(BF16)
    o_ref[COL_XS:D_CONV, :] = w_ref[SRC_XBC:SRC_DT, :].astype(BF16)
    zeros = jnp.zeros((V7X_LANES - N_REPLICAS * HEADS_PER_GROUP, o_ref.shape[1]), F32)
    for g in range(N_GROUPS):
        row0 = COL_DT + g * V7X_LANES
        src0 = SRC_DT + g * HEADS_PER_GROUP
        heads = w_ref[src0:src0 + HEADS_PER_GROUP, :]
        block = jnp.concatenate([heads] * N_REPLICAS + [zeros], axis=0)
        o_ref[row0:row0 + V7X_LANES, :] = block.astype(BF16)


def _w_in_prep(w_t, *, tk):
    n, k = w_t.shape
    assert n == D_IN_PROJ
    return pl.pallas_call(
        _w_in_prep_kernel,
        out_shape=jax.ShapeDtypeStruct((D_PROJ, k), BF16),
        grid=(k // tk,),
        in_specs=[pl.BlockSpec((n, tk), lambda i: (0, i))],
        out_specs=pl.BlockSpec((D_PROJ, tk), lambda i: (0, i)),
        compiler_params=pltpu.CompilerParams(
            dimension_semantics=("parallel",),
            vmem_limit_bytes=V7X_VMEM_LIMIT_BYTES),
        name="w_in_prep",
    )(w_t)


def _side_cast_specs(weights, n_slabs, step_of):
    specs = []
    for w in weights:
        rows = w.shape[0] // n_slabs
        assert rows * n_slabs == w.shape[0] and rows % (2 * V7X_SUBLANES) == 0
        specs.append(pl.BlockSpec(
            (rows, w.shape[1]), lambda *g: (jnp.minimum(step_of(*g), n_slabs - 1), 0)))
    return specs


CONV_CHUNK = 256


def _conv_silu(r, halo, w_ref, b_ref, cols):
    ext = jnp.concatenate([halo, r], axis=0)
    acc = r * w_ref[CONV_WIDTH - 1:CONV_WIDTH, cols] + b_ref[:, cols]
    for back in range(1, CONV_WIDTH):
        tap = w_ref[CONV_WIDTH - 1 - back:CONV_WIDTH - back, cols]
        acc = acc + pltpu.roll(ext, back, axis=0)[HALO_ROWS:] * tap
    return _silu(acc)


def _in_proj_kernel(x_ref, nw_ref, w_ref, cw_ref, cb_ref, *rest, n_conv_blocks, n_side):
    side_in, rest = rest[:n_side], rest[n_side:]
    o_ref, side_out = rest[0], rest[1:1 + n_side]
    hn_ref, halo_ref, raw_ref = rest[1 + n_side:]
    i = pl.program_id(0)
    j = pl.program_id(1)
    tm, tn = o_ref.shape

    for src_ref, dst_ref in zip(side_in, side_out):
        dst_ref[...] = src_ref[...].astype(BF16)

    @pl.when(j == 0)
    def _():
        hn_ref[...] = _rms_normalize(x_ref[...], nw_ref[...]).astype(BF16)

    @pl.when((i == 0) & (j == 0))
    def _():
        halo_ref[...] = jnp.zeros_like(halo_ref)

    @pl.when(j < n_conv_blocks)
    def _():
        n_pieces = tn // CONV_CHUNK
        piece = lambda c: slice(c * CONV_CHUNK, (c + 1) * CONV_CHUNK)
        slot = lambda c: lax.rem(j + c, 2)
        raw_ref[slot(0)] = _dot_nt(hn_ref[...], w_ref[piece(0), :])
        for c in range(n_pieces):
            cols = piece(c)
            if c + 1 < n_pieces:
                raw_ref[slot(c + 1)] = _dot_nt(hn_ref[...], w_ref[piece(c + 1), :])
            r = raw_ref[slot(c)]
            o_ref[:, cols] = _conv_silu(r, halo_ref[j, :, cols], cw_ref, cb_ref, cols)
            halo_ref[j, :, cols] = r[tm - HALO_ROWS:]

    @pl.when(j >= n_conv_blocks)
    def _():
        o_ref[...] = _dot_nt(hn_ref[...], w_ref[...])


def _in_proj(x, nw, w_t, conv_w, conv_b, side_weights, *, tm, tn, side_slabs):
    m, k = x.shape
    n = w_t.shape[0]
    assert D_CONV % tn == 0 and tn % CONV_CHUNK == 0
    n_conv_blocks = D_CONV // tn
    n_col_blocks = n // tn
    assert (m // tm) * n_col_blocks >= side_slabs
    conv_block = lambda i, j: (0, jnp.minimum(j, n_conv_blocks - 1))
    side_specs = _side_cast_specs(side_weights, side_slabs, lambda i, j: i * n_col_blocks + j)
    return pl.pallas_call(
        functools.partial(_in_proj_kernel, n_conv_blocks=n_conv_blocks, n_side=len(side_weights)),
        out_shape=(jax.ShapeDtypeStruct((m, n), F32),
                   *[jax.ShapeDtypeStruct(w.shape, BF16) for w in side_weights]),
        grid=(m // tm, n_col_blocks),
        in_specs=[
            pl.BlockSpec((tm, k), lambda i, j: (i, 0)),
            pl.BlockSpec((1, k), lambda i, j: (0, 0)),
            pl.BlockSpec((tn, k), lambda i, j: (j, 0)),
            pl.BlockSpec((CONV_WIDTH, tn), conv_block),
            pl.BlockSpec((1, tn), conv_block),
        ] + side_specs,
        out_specs=(pl.BlockSpec((tm, tn), lambda i, j: (i, j)), *side_specs),
        scratch_shapes=[pltpu.VMEM((tm, k), BF16),
                        pltpu.VMEM((n_conv_blocks, HALO_ROWS, tn), F32),
                        pltpu.VMEM((2, tm, CONV_CHUNK), F32)],
        compiler_params=pltpu.CompilerParams(
            dimension_semantics=("arbitrary", "arbitrary"),
            vmem_limit_bytes=V7X_VMEM_LIMIT_BYTES),
        name="in_proj",
    )(x, nw, w_t, conv_w, conv_b, *side_weights)


def _swiglu_kernel(hn_ref, wg_ref, wu_ref, *rest):
    n_side = (len(rest) - 1) // 2
    side_in, o_ref, side_out = rest[:n_side], rest[n_side], rest[n_side + 1:]
    for src_ref, dst_ref in zip(side_in, side_out):
        dst_ref[...] = src_ref[...].astype(BF16)
    half = o_ref.shape[1] // 2
    for cols in (slice(0, half), slice(half, 2 * half)):
        gate = _dot(hn_ref[...], wg_ref[:, cols])
        up = _dot(hn_ref[...], wu_ref[:, cols])
        o_ref[:, cols] = (_silu(gate) * up).astype(BF16)


def _swiglu(hn, wg, wu, side_weights, *, tm, tn):
    m, k = hn.shape
    n = wg.shape[1]
    n_row_tiles = m // tm
    side_specs = _side_cast_specs(side_weights, (n // tn) * n_row_tiles,
                                  lambda j, i: j * n_row_tiles + i)
    return pl.pallas_call(
        _swiglu_kernel,
        out_shape=(jax.ShapeDtypeStruct((m, n), BF16),
                   *[jax.ShapeDtypeStruct(w.shape, BF16) for w in side_weights]),
        grid=(n // tn, n_row_tiles),
        in_specs=[
            pl.BlockSpec((tm, k), lambda j, i: (i, 0)),
            pl.BlockSpec((k, tn), lambda j, i: (0, j)),
            pl.BlockSpec((k, tn), lambda j, i: (0, j)),
        ] + side_specs,
        out_specs=(pl.BlockSpec((tm, tn), lambda j, i: (i, j)), *side_specs),
        compiler_params=pltpu.CompilerParams(
            dimension_semantics=("arbitrary", "arbitrary"),
            vmem_limit_bytes=V7X_VMEM_LIMIT_BYTES),
        name="ffn_up",
    )(hn, wg, wu, *side_weights)


def _out_proj_kernel(ys_ref, yp_ref, w_ref, x_ref, nw_ref, h_ref, hn_ref):
    k = ys_ref.shape[1]
    h = x_ref[...] + _dot(ys_ref[...], w_ref[:k, :]) + _dot(yp_ref[...], w_ref[k:, :])
    h_ref[...] = h
    hn_ref[...] = _rms_normalize(h, nw_ref[...]).astype(BF16)


def _resident(shape):
    return pl.BlockSpec(shape, lambda i: (0,) * len(shape), pipeline_mode=pl.Buffered(1))


def _out_proj(ys, yp, w, x, nw, *, tm):
    m, k = ys.shape
    n = w.shape[1]
    row_spec = lambda width: pl.BlockSpec((tm, width), lambda i: (i, 0))
    return pl.pallas_call(
        _out_proj_kernel,
        out_shape=(jax.ShapeDtypeStruct((m, n), F32), jax.ShapeDtypeStruct((m, n), BF16)),
        grid=(m // tm,),
        in_specs=[row_spec(k), row_spec(k), _resident(w.shape), row_spec(n), _resident((1, n))],
        out_specs=(row_spec(n), row_spec(n)),
        compiler_params=pltpu.CompilerParams(
            dimension_semantics=("parallel",),
            vmem_limit_bytes=V7X_VMEM_LIMIT_BYTES),
        name="out_proj",
    )(ys, yp, w, x, nw)


def _down_kernel(a_ref, w_ref, h_ref, nw_ref, o_ref, *, final_norm):
    h = h_ref[...] + _dot(a_ref[...], w_ref[...])
    o_ref[...] = _rms_normalize(h, nw_ref[...]) if final_norm else h


def _down_proj(act, w, h, nw, *, tm, final_norm):
    m, k = act.shape
    n = w.shape[1]
    row_spec = lambda width: pl.BlockSpec((tm, width), lambda i: (i, 0))
    return pl.pallas_call(
        functools.partial(_down_kernel, final_norm=final_norm),
        out_shape=jax.ShapeDtypeStruct((m, n), F32),
        grid=(m // tm,),
        in_specs=[row_spec(k), _resident(w.shape), row_spec(n), _resident((1, n))],
        out_specs=row_spec(n),
        compiler_params=pltpu.CompilerParams(
            dimension_semantics=("parallel",),
            vmem_limit_bytes=V7X_VMEM_LIMIT_BYTES),
        name="ffn_down",
    )(act, w, h, nw)


def _pool_tile(i, u_refs, halo_refs, pw_ref, ps_ref, o_ref):
    tl = o_ref.shape[0]
    t = i * tl + lax.broadcasted_iota(jnp.int32, (tl, POOL_GROUP_DIM), 0)
    for g, window in enumerate(POOL_WINDOWS):
        cols = slice(g * POOL_GROUP_DIM, (g + 1) * POOL_GROUP_DIM)
        u = u_refs[g][...]
        halo = jnp.where(i == 0, 0.0, halo_refs[g][...])
        s = jnp.concatenate([halo, u], axis=0)
        shift = 1
        while shift < window:
            s = s + pltpu.roll(s, shift, axis=0)
            shift *= 2
        count = jnp.minimum(t + 1, window).astype(F32)
        pooled = s[POOL_HALO:] / count - u
        y = _dot(pooled.astype(BF16), pw_ref[g].astype(BF16)) * ps_ref[:, cols]
        o_ref[:, cols] = y.astype(BF16)


def _pool_specs(tl, step_of):
    gd = POOL_GROUP_DIM
    assert COL_U % gd == 0 and tl % POOL_HALO == 0
    halo_blocks_per_tile = tl // POOL_HALO
    n_groups = len(POOL_WINDOWS)
    u_specs = [pl.BlockSpec((tl, gd), lambda *g, k=k: (step_of(*g), COL_U // gd + k))
               for k in range(n_groups)]
    halo_specs = [
        pl.BlockSpec((POOL_HALO, gd),
                     lambda *g, k=k: (jnp.maximum(step_of(*g) * halo_blocks_per_tile - 1, 0),
                                      COL_U // gd + k))
        for k in range(n_groups)]
    return u_specs + halo_specs


assert N_REPLICAS * HEADS_PER_GROUP <= V7X_LANES


def _expansion_matrix():
    lane = jnp.arange(V7X_LANES)
    col = jnp.arange(N_EXPANDED * GROUP_WIDTH)
    replica, head = lane // HEADS_PER_GROUP, lane % HEADS_PER_GROUP
    quantity = replica // 3
    hit = ((quantity[:, None] == col[None, :] // GROUP_WIDTH)
           & (head[:, None] == (col[None, :] % GROUP_WIDTH) // HEAD_DIM)
           & (replica[:, None] < N_REPLICAS))
    return hit.astype(BF16)


def _ssd_group(z, xs, bm, cm, dt_raw, dtb, alog, dskip, nw, state_ref, expand, tril, causal):
    q = xs.shape[0]
    x = dt_raw + dtb
    dt = jnp.maximum(x, 0.0) + jnp.log1p(jnp.exp(-jnp.abs(x)))
    a = -jnp.exp(alog)
    a_cum = _dot3_left(tril, dt * a) * LOG2_E
    a_cum_t = a_cum.T
    a_last = a_cum[q - 1:q, :]

    replica = lax.broadcasted_iota(jnp.int32, (1, V7X_LANES), 1) // HEADS_PER_GROUP
    packed = jnp.where(replica < 3, dt,
                       jnp.where(replica < 6, jnp.exp2(a_cum), jnp.exp2(a_last - a_cum)))
    hi = packed.astype(BF16)
    r1 = packed - hi.astype(F32)
    r2 = r1 - r1.astype(BF16).astype(F32)
    level = replica % 3
    split = jnp.where(level == 0, packed, jnp.where(level == 1, r1, r2)).astype(BF16)
    expanded = _dot(split, expand)
    dt_b = expanded[:, :GROUP_WIDTH]
    decay_in_b = expanded[:, GROUP_WIDTH:2 * GROUP_WIDTH]
    decay_out_b = expanded[:, 2 * GROUP_WIDTH:]

    xdt = xs * dt_b
    bm_bf = bm.astype(BF16)
    cm_bf = cm.astype(BF16)
    cb = _dot_nt(cm_bf, bm_bf)

    xdt_bf = xdt.astype(BF16)
    lane = lax.broadcasted_iota(jnp.int32, (1, V7X_LANES), 1)
    keep = [jnp.where((lane // HEAD_DIM) == side, 1.0, 0.0).astype(BF16) for side in range(2)]
    pairs = []
    for pair in range(HEADS_PER_GROUP // 2):
        xdt_pair = xdt_bf[:, pair * V7X_LANES:(pair + 1) * V7X_LANES]
        acc = jnp.zeros((q, V7X_LANES), F32)
        for side in range(2):
            r = 2 * pair + side
            seg = a_cum[:, r:r + 1] - a_cum_t[r:r + 1, :]
            decay = jnp.exp2(jnp.where(causal, seg, -jnp.inf))
            m_r = (cb * decay).astype(BF16)
            acc = acc + _dot(m_r, xdt_pair * keep[side])
        pairs.append(acc)
    y_diag = jnp.concatenate(pairs, axis=-1)

    state = state_ref[...]
    y_off = _dot(cm_bf, state.astype(BF16)) * decay_in_b
    xw = (xdt * decay_out_b).astype(BF16)
    new_state = lax.dot_general(bm_bf, xw, (((0,), (0,)), ((), ())), preferred_element_type=F32)
    state_ref[...] = state * decay_in_b[q - 1:q, :] + new_state

    y = y_diag + y_off + dskip * xs
    g = y * _silu(z)
    return _rms_normalize(g, nw).astype(BF16)


def _mixers_kernel(*refs, groups, n_side):
    refs = list(refs)
    take = lambda n: [refs.pop(0) for _ in range(n)]
    n_pool = len(POOL_WINDOWS)
    z_refs = take(groups)
    xs_ref, b_ref, c_ref, dt_ref, dtb_ref, alog_ref, dskip_ref, nw_ref, expand_ref = take(9)
    u_refs, halo_refs = take(n_pool), take(n_pool)
    pw_ref, ps_ref = take(2)
    side_in = take(n_side)
    o_ref, yp_ref = take(2)
    side_out = take(n_side)
    state_ref, = refs
    q = SSD_CHUNK

    _pool_tile(pl.program_id(1), u_refs, halo_refs, pw_ref, ps_ref, yp_ref)

    for src_ref, dst_ref in zip(side_in, side_out):
        dst_ref[...] = src_ref[...].astype(BF16)

    @pl.when(pl.program_id(1) == 0)
    def _():
        state_ref[...] = jnp.zeros_like(state_ref)

    row_i = lax.broadcasted_iota(jnp.int32, (q, q), 0)
    col_i = lax.broadcasted_iota(jnp.int32, (q, q), 1)
    causal = row_i >= col_i
    tril = jnp.where(causal, 1.0, 0.0).astype(BF16)
    expand = expand_ref[...]
    for g in range(groups):
        wide = slice(g * GROUP_WIDTH, (g + 1) * GROUP_WIDTH)
        narrow = slice(g * D_STATE, (g + 1) * D_STATE)
        o_ref[:, wide] = _ssd_group(
            z_refs[g][...], xs_ref[:, wide], b_ref[:, narrow], c_ref[:, narrow], dt_ref[:, narrow],
            dtb_ref[:, narrow], alog_ref[:, narrow], dskip_ref[:, wide], nw_ref[:, wide],
            state_ref.at[g], expand, tril, causal)


def _mixers(proj, dtb_c, alog_c, dskip_full, norm_w, pool_w, pool_scale, side_weights, *, groups):
    m = proj.shape[0]
    q = SSD_CHUNK
    gw = groups * GROUP_WIDTH
    ds = groups * D_STATE
    assert V7X_LANES == D_STATE and N_GROUPS % groups == 0

    def col(start, width):
        assert start % width == 0
        return start // width

    row_spec = lambda width, block0: pl.BlockSpec((q, width), lambda g, c: (c, block0 + g))
    par_spec = lambda width: pl.BlockSpec((1, width), lambda g, c: (0, g))
    z_specs = [
        pl.BlockSpec((q, GROUP_WIDTH),
                     lambda g, c, k=k: (c, col(COL_Z, GROUP_WIDTH) + g * groups + k))
        for k in range(groups)]
    expand = _expansion_matrix()
    assert groups == N_GROUPS
    side_specs = _side_cast_specs(side_weights, m // q, lambda g, c: c)
    return pl.pallas_call(
        functools.partial(_mixers_kernel, groups=groups, n_side=len(side_weights)),
        out_shape=(jax.ShapeDtypeStruct((m, D_SSM), BF16), jax.ShapeDtypeStruct((m, D_POOL), BF16),
                   *[jax.ShapeDtypeStruct(w.shape, BF16) for w in side_weights]),
        grid=(N_GROUPS // groups, m // q),
        in_specs=z_specs + [
            row_spec(gw, col(COL_XS, gw)),
            row_spec(ds, col(COL_B, ds)),
            row_spec(ds, col(COL_C, ds)),
            row_spec(ds, col(COL_DT, ds)),
            par_spec(ds),
            par_spec(ds),
            par_spec(gw),
            par_spec(gw),
            pl.BlockSpec(expand.shape, lambda g, c: (0, 0)),
        ] + _pool_specs(q, lambda g, c: c) + [
            pl.BlockSpec(pool_w.shape, lambda g, c: (0, 0, 0)),
            pl.BlockSpec(pool_scale.shape, lambda g, c: (0, 0)),
        ] + side_specs,
        out_specs=(pl.BlockSpec((q, gw), lambda g, c: (c, g)),
                   pl.BlockSpec((q, D_POOL), lambda g, c: (c, 0)), *side_specs),
        scratch_shapes=[pltpu.VMEM((groups, D_STATE, GROUP_WIDTH), F32)],
        compiler_params=pltpu.CompilerParams(
            dimension_semantics=("parallel", "arbitrary"),
            vmem_limit_bytes=V7X_VMEM_LIMIT_BYTES),
        name="mixers",
    )(*([proj] * (groups + 4)), dtb_c, alog_c, dskip_full, norm_w, expand,
      *([proj] * (2 * len(POOL_WINDOWS))), pool_w, pool_scale, *side_weights)


def _pad_heads_to_lanes(v):
    v = jnp.tile(v.reshape(N_GROUPS, HEADS_PER_GROUP), (1, N_REPLICAS))
    v = jnp.pad(v, ((0, 0), (0, V7X_LANES - N_REPLICAS * HEADS_PER_GROUP)))
    return v.reshape(1, N_GROUPS * V7X_LANES)


def kernel(x, attn_norm_w, w_in, conv_w, conv_b, dt_bias, a_log, d_skip, ssd_norm_w, pool_w,
           pool_scale, w_out, ffn_norm_w, w_gate, w_up, w_down, final_norm_w):
    bsz, seqlen, d_model = x.shape
    depth = w_in.shape[0]
    h = x.reshape(bsz * seqlen, d_model)
    assert bsz == 1 and seqlen % 2048 == 0 and d_model == D_MODEL

    for i in range(depth):
        w_cat = _w_in_prep(w_in[i].T, tk=256)
        proj, w_out_bf = _in_proj(
            h, attn_norm_w[i][None, :], w_cat, conv_w[i], conv_b[i][None, :],
            (w_out[i],), tm=1024, tn=1536, side_slabs=32)

        y_ssd, y_pool, w_gate_bf, w_up_bf = _mixers(
            proj, _pad_heads_to_lanes(dt_bias[i]), _pad_heads_to_lanes(a_log[i]),
            jnp.repeat(d_skip[i], HEAD_DIM)[None, :], ssd_norm_w[i][None, :],
            pool_w[i], pool_scale[i][None, :], (w_gate[i], w_up[i]), groups=N_GROUPS)

        h, hn = _out_proj(y_ssd, y_pool, w_out_bf, h, ffn_norm_w[i][None, :], tm=512)

        act, w_down_bf = _swiglu(hn, w_gate_bf, w_up_bf, (w_down[i],), tm=2048, tn=512)
        last = i == depth - 1
        h = _down_proj(act, w_down_bf, h, final_norm_w[None, :], tm=512, final_norm=last)

    return h.reshape(bsz, seqlen, d_model)
```

```python
import functools

import jax
import jax.numpy as jnp
from jax import lax
from jax.experimental import pallas as pl
from jax.experimental.pallas import tpu as pltpu

F32 = jnp.float32
BF16 = jnp.bfloat16

NORM_EPS = 1e-5
LOG2_E = 1.4426950408889634
D_MODEL = 2048
D_SSM = 2048
HEAD_DIM = 64
N_HEADS = D_SSM // HEAD_DIM
N_GROUPS = 4
HEADS_PER_GROUP = N_HEADS // N_GROUPS
GROUP_WIDTH = D_SSM // N_GROUPS
D_STATE = 128
CONV_WIDTH = 4
D_POOL = 2048
POOL_WINDOWS = (2, 4, 8, 16)
POOL_GROUP_DIM = D_POOL // len(POOL_WINDOWS)
D_MIX = D_SSM + D_POOL

V7X_LANES = 128
V7X_SUBLANES = 8
V7X_VMEM_LIMIT_BYTES = 60 * 1024 * 1024

COL_XS = 0
COL_B = COL_XS + D_SSM
COL_C = COL_B + N_GROUPS * D_STATE
D_CONV = COL_C + N_GROUPS * D_STATE
COL_Z = D_CONV
COL_U = COL_Z + D_SSM
COL_DT = COL_U + D_POOL
D_PROJ = COL_DT + N_GROUPS * V7X_LANES

SSD_CHUNK = 256
N_EXPANDED = 3
N_REPLICAS = 3 * N_EXPANDED
HALO_ROWS = V7X_SUBLANES
POOL_HALO = 16


def _sigmoid(x):
    return 1.0 / (1.0 + jnp.exp(-x))


def _silu(x):
    return x * _sigmoid(x)


def _rms_normalize(x, w):
    ms = jnp.mean(x * x, axis=-1, keepdims=True)
    return x * lax.rsqrt(ms + NORM_EPS) * w


def _split3(x):
    hi = x.astype(BF16)
    r1 = x - hi.astype(F32)
    mid = r1.astype(BF16)
    lo = (r1 - mid.astype(F32)).astype(BF16)
    return hi, mid, lo


def _dot(a, b):
    return jnp.dot(a, b, preferred_element_type=F32)


def _dot_nt(a, b):
    return lax.dot_general(a, b, (((1,), (1,)), ((), ())), preferred_element_type=F32)


def _dot3_left(lhs_bf16, x):
    hi, mid, lo = _split3(x)
    return _dot(lhs_bf16, hi) + _dot(lhs_bf16, mid) + _dot(lhs_bf16, lo)


SRC_XBC = D_SSM
SRC_DT = SRC_XBC + D_SSM + 2 * N_GROUPS * D_STATE
SRC_U = SRC_DT + N_HEADS
D_IN_PROJ = SRC_U + D_POOL


def _w_in_prep_kernel(w_ref, o_ref):
    o_ref[COL_Z:COL_Z + D_SSM, :] = w_ref[:D_SSM, :].astype(BF16)
    o_ref[COL_U:COL_U + D_POOL, :] = w_ref[SRC_U:SRC_U + D_POOL, :].astype(BF16)
    o_ref[COL_XS:D_CONV, :] = w_ref[SRC_XBC:SRC_DT, :].astype(BF16)
    zeros = jnp.zeros((V7X_LANES - N_REPLICAS * HEADS_PER_GROUP, o_ref.shape[1]), F32)
    for g in range(N_GROUPS):
        row0 = COL_DT + g * V7X_LANES
        src0 = SRC_DT + g * HEADS_PER_GROUP
        heads = w_ref[src0:src0 + HEADS_PER_GROUP, :]
        block = jnp.concatenate([heads] * N_REPLICAS + [zeros], axis=0)
        o_ref[row0:row0 + V7X_LANES, :] = block.astype(BF16)


def _w_in_prep(w_t, *, tk):
    n, k = w_t.shape
    assert n == D_IN_PROJ
    return pl.pallas_call(
        _w_in_prep_kernel,
        out_shape=jax.ShapeDtypeStruct((D_PROJ, k), BF16),
        grid=(k // tk,),
        in_specs=[pl.BlockSpec((n, tk), lambda i: (0, i))],
        out_specs=pl.BlockSpec((D_PROJ, tk), lambda i: (0, i)),
        compiler_params=pltpu.CompilerParams(
            dimension_semantics=("parallel",),
            vmem_limit_bytes=V7X_VMEM_LIMIT_BYTES),
        name="w_in_prep",
    )(w_t)


def _side_cast_specs(weights, n_slabs, step_of):
    specs = []
    for w in weights:
        rows = w.shape[0] // n_slabs
        assert rows * n_slabs == w.shape[0] and rows % (2 * V7X_SUBLANES) == 0
        specs.append(pl.BlockSpec(
            (rows, w.shape[1]), lambda *g: (jnp.minimum(step_of(*g), n_slabs - 1), 0)))
    return specs


CONV_CHUNK = 256


def _conv_silu(r, halo, w_ref, b_ref, cols):
    ext = jnp.concatenate([halo, r], axis=0)
    acc = r * w_ref[CONV_WIDTH - 1:CONV_WIDTH, cols] + b_ref[:, cols]
    for back in range(1, CONV_WIDTH):
        tap = w_ref[CONV_WIDTH - 1 - back:CONV_WIDTH - back, cols]
        acc = acc + pltpu.roll(ext, back, axis=0)[HALO_ROWS:] * tap
    return _silu(acc)


def _in_proj_kernel(x_ref, nw_ref, w_ref, cw_ref, cb_ref, *rest, n_conv_blocks, n_side):
    side_in, rest = rest[:n_side], rest[n_side:]
    o_ref, side_out = rest[0], rest[1:1 + n_side]
    hn_ref, halo_ref, raw_ref = rest[1 + n_side:]
    i = pl.program_id(0)
    j = pl.program_id(1)
    tm, tn = o_ref.shape

    for src_ref, dst_ref in zip(side_in, side_out):
        dst_ref[...] = src_ref[...].astype(BF16)

    @pl.when(j == 0)
    def _():
        hn_ref[...] = _rms_normalize(x_ref[...], nw_ref[...]).astype(BF16)

    @pl.when((i == 0) & (j == 0))
    def _():
        halo_ref[...] = jnp.zeros_like(halo_ref)

    @pl.when(j < n_conv_blocks)
    def _():
        n_pieces = tn // CONV_CHUNK
        piece = lambda c: slice(c * CONV_CHUNK, (c + 1) * CONV_CHUNK)
        slot = lambda c: lax.rem(j + c, 2)
        raw_ref[slot(0)] = _dot_nt(hn_ref[...], w_ref[piece(0), :])
        for c in range(n_pieces):
            cols = piece(c)
            if c + 1 < n_pieces:
                raw_ref[slot(c + 1)] = _dot_nt(hn_ref[...], w_ref[piece(c + 1), :])
            r = raw_ref[slot(c)]
            o_ref[:, cols] = _conv_silu(r, halo_ref[j, :, cols], cw_ref, cb_ref, cols).astype(BF16)
            halo_ref[j, :, cols] = r[tm - HALO_ROWS:]

    @pl.when(j >= n_conv_blocks)
    def _():
        o_ref[...] = _dot_nt(hn_ref[...], w_ref[...]).astype(BF16)


def _in_proj(x, nw, w_t, conv_w, conv_b, side_weights, *, tm, tn, side_slabs):
    m, k = x.shape
    n = w_t.shape[0]
    assert D_CONV % tn == 0 and tn % CONV_CHUNK == 0
    n_conv_blocks = D_CONV // tn
    n_col_blocks = n // tn
    assert (m // tm) * n_col_blocks >= side_slabs
    conv_block = lambda i, j: (0, jnp.minimum(j, n_conv_blocks - 1))
    side_specs = _side_cast_specs(side_weights, side_slabs, lambda i, j: i * n_col_blocks + j)
    return pl.pallas_call(
        functools.partial(_in_proj_kernel, n_conv_blocks=n_conv_blocks, n_side=len(side_weights)),
        out_shape=(jax.ShapeDtypeStruct((m, n), BF16),
                   *[jax.ShapeDtypeStruct(w.shape, BF16) for w in side_weights]),
        grid=(m // tm, n_col_blocks),
        in_specs=[
            pl.BlockSpec((tm, k), lambda i, j: (i, 0)),
            pl.BlockSpec((1, k), lambda i, j: (0, 0)),
            pl.BlockSpec((tn, k), lambda i, j: (j, 0)),
            pl.BlockSpec((CONV_WIDTH, tn), conv_block),
            pl.BlockSpec((1, tn), conv_block),
        ] + side_specs,
        out_specs=(pl.BlockSpec((tm, tn), lambda i, j: (i, j)), *side_specs),
        scratch_shapes=[pltpu.VMEM((tm, k), BF16),
                        pltpu.VMEM((n_conv_blocks, HALO_ROWS, tn), F32),
                        pltpu.VMEM((2, tm, CONV_CHUNK), F32)],
        compiler_params=pltpu.CompilerParams(
            dimension_semantics=("arbitrary", "arbitrary"),
            vmem_limit_bytes=V7X_VMEM_LIMIT_BYTES),
        name="in_proj",
    )(x, nw, w_t, conv_w, conv_b, *side_weights)


def _swiglu_kernel(hn_ref, wg_ref, wu_ref, *rest):
    n_side = (len(rest) - 1) // 2
    side_in, o_ref, side_out = rest[:n_side], rest[n_side], rest[n_side + 1:]
    for src_ref, dst_ref in zip(side_in, side_out):
        dst_ref[...] = src_ref[...].astype(BF16)
    half = o_ref.shape[1] // 2
    for cols in (slice(0, half), slice(half, 2 * half)):
        gate = _dot(hn_ref[...], wg_ref[:, cols])
        up = _dot(hn_ref[...], wu_ref[:, cols])
        o_ref[:, cols] = (_silu(gate) * up).astype(BF16)


def _swiglu(hn, wg, wu, side_weights, *, tm, tn):
    m, k = hn.shape
    n = wg.shape[1]
    n_row_tiles = m // tm
    side_specs = _side_cast_specs(side_weights, (n // tn) * n_row_tiles,
                                  lambda j, i: j * n_row_tiles + i)
    return pl.pallas_call(
        _swiglu_kernel,
        out_shape=(jax.ShapeDtypeStruct((m, n), BF16),
                   *[jax.ShapeDtypeStruct(w.shape, BF16) for w in side_weights]),
        grid=(n // tn, n_row_tiles),
        in_specs=[
            pl.BlockSpec((tm, k), lambda j, i: (i, 0)),
            pl.BlockSpec((k, tn), lambda j, i: (0, j)),
            pl.BlockSpec((k, tn), lambda j, i: (0, j)),
        ] + side_specs,
        out_specs=(pl.BlockSpec((tm, tn), lambda j, i: (i, j)), *side_specs),
        compiler_params=pltpu.CompilerParams(
            dimension_semantics=("arbitrary", "arbitrary"),
            vmem_limit_bytes=V7X_VMEM_LIMIT_BYTES),
        name="ffn_up",
    )(hn, wg, wu, *side_weights)


def _out_proj_kernel(ys_ref, yp_ref, w_ref, x_ref, nw_ref, h_ref, hn_ref):
    k = ys_ref.shape[1]
    h = x_ref[...] + _dot(ys_ref[...], w_ref[:k, :]) + _dot(yp_ref[...], w_ref[k:, :])
    h_ref[...] = h
    hn_ref[...] = _rms_normalize(h, nw_ref[...]).astype(BF16)


def _resident(shape):
    return pl.BlockSpec(shape, lambda i: (0,) * len(shape), pipeline_mode=pl.Buffered(1))


def _out_proj(ys, yp, w, x, nw, *, tm):
    m, k = ys.shape
    n = w.shape[1]
    row_spec = lambda width: pl.BlockSpec((tm, width), lambda i: (i, 0))
    return pl.pallas_call(
        _out_proj_kernel,
        out_shape=(jax.ShapeDtypeStruct((m, n), F32), jax.ShapeDtypeStruct((m, n), BF16)),
        grid=(m // tm,),
        in_specs=[row_spec(k), row_spec(k), _resident(w.shape), row_spec(n), _resident((1, n))],
        out_specs=(row_spec(n), row_spec(n)),
        compiler_params=pltpu.CompilerParams(
            dimension_semantics=("parallel",),
            vmem_limit_bytes=V7X_VMEM_LIMIT_BYTES),
        name="out_proj",
    )(ys, yp, w, x, nw)


def _down_kernel(a_ref, w_ref, h_ref, nw_ref, o_ref, *, final_norm):
    h = h_ref[...] + _dot(a_ref[...], w_ref[...])
    o_ref[...] = _rms_normalize(h, nw_ref[...]) if final_norm else h


def _down_proj(act, w, h, nw, *, tm, final_norm):
    m, k = act.shape
    n = w.shape[1]
    row_spec = lambda width: pl.BlockSpec((tm, width), lambda i: (i, 0))
    return pl.pallas_call(
        functools.partial(_down_kernel, final_norm=final_norm),
        out_shape=jax.ShapeDtypeStruct((m, n), F32),
        grid=(m // tm,),
        in_specs=[row_spec(k), _resident(w.shape), row_spec(n), _resident((1, n))],
        out_specs=row_spec(n),
        compiler_params=pltpu.CompilerParams(
            dimension_semantics=("parallel",),
            vmem_limit_bytes=V7X_VMEM_LIMIT_BYTES),
        name="ffn_down",
    )(act, w, h, nw)


def _pool_tile(i, u_refs, halo_refs, pw_ref, ps_ref, o_ref):
    tl = o_ref.shape[0]
    t = i * tl + lax.broadcasted_iota(jnp.int32, (tl, POOL_GROUP_DIM), 0)
    for g, window in enumerate(POOL_WINDOWS):
        cols = slice(g * POOL_GROUP_DIM, (g + 1) * POOL_GROUP_DIM)
        u = u_refs[g][...].astype(F32)
        halo = jnp.where(i == 0, 0.0, halo_refs[g][...].astype(F32))
        s = jnp.concatenate([halo, u], axis=0)
        shift = 1
        while shift < window:
            s = s + pltpu.roll(s, shift, axis=0)
            shift *= 2
        count = jnp.minimum(t + 1, window).astype(F32)
        pooled = s[POOL_HALO:] / count - u
        y = _dot(pooled.astype(BF16), pw_ref[g].astype(BF16)) * ps_ref[:, cols]
        o_ref[:, cols] = y.astype(BF16)


def _pool_specs(tl, step_of):
    gd = POOL_GROUP_DIM
    assert COL_U % gd == 0 and tl % POOL_HALO == 0
    halo_blocks_per_tile = tl // POOL_HALO
    n_groups = len(POOL_WINDOWS)
    u_specs = [pl.BlockSpec((tl, gd), lambda *g, k=k: (step_of(*g), COL_U // gd + k))
               for k in range(n_groups)]
    halo_specs = [
        pl.BlockSpec((POOL_HALO, gd),
                     lambda *g, k=k: (jnp.maximum(step_of(*g) * halo_blocks_per_tile - 1, 0),
                                      COL_U // gd + k))
        for k in range(n_groups)]
    return u_specs + halo_specs


assert N_REPLICAS * HEADS_PER_GROUP <= V7X_LANES


def _expansion_matrix():
    lane = jnp.arange(V7X_LANES)
    col = jnp.arange(N_EXPANDED * GROUP_WIDTH)
    replica, head = lane // HEADS_PER_GROUP, lane % HEADS_PER_GROUP
    quantity = replica // 3
    hit = ((quantity[:, None] == col[None, :] // GROUP_WIDTH)
           & (head[:, None] == (col[None, :] % GROUP_WIDTH) // HEAD_DIM)
           & (replica[:, None] < N_REPLICAS))
    return hit.astype(BF16)


def _ssd_group(z, xs, bm, cm, dt_raw, dtb, alog, dskip, nw, state_ref, expand, tril, causal):
    q = xs.shape[0]
    x = dt_raw + dtb
    dt = jnp.maximum(x, 0.0) + jnp.log1p(jnp.exp(-jnp.abs(x)))
    a = -jnp.exp(alog)
    a_cum = _dot3_left(tril, dt * a) * LOG2_E
    a_cum_t = a_cum.T
    a_last = a_cum[q - 1:q, :]

    replica = lax.broadcasted_iota(jnp.int32, (1, V7X_LANES), 1) // HEADS_PER_GROUP
    packed = jnp.where(replica < 3, dt,
                       jnp.where(replica < 6, jnp.exp2(a_cum), jnp.exp2(a_last - a_cum)))
    hi = packed.astype(BF16)
    r1 = packed - hi.astype(F32)
    r2 = r1 - r1.astype(BF16).astype(F32)
    level = replica % 3
    split = jnp.where(level == 0, packed, jnp.where(level == 1, r1, r2)).astype(BF16)
    expanded = _dot(split, expand)
    dt_b = expanded[:, :GROUP_WIDTH]
    decay_in_b = expanded[:, GROUP_WIDTH:2 * GROUP_WIDTH]
    decay_out_b = expanded[:, 2 * GROUP_WIDTH:]

    xdt = xs * dt_b
    bm_bf = bm.astype(BF16)
    cm_bf = cm.astype(BF16)
    cb = _dot_nt(cm_bf, bm_bf)

    xdt_bf = xdt.astype(BF16)
    lane = lax.broadcasted_iota(jnp.int32, (1, V7X_LANES), 1)
    keep = [jnp.where((lane // HEAD_DIM) == side, 1.0, 0.0).astype(BF16) for side in range(2)]
    pairs = []
    for pair in range(HEADS_PER_GROUP // 2):
        xdt_pair = xdt_bf[:, pair * V7X_LANES:(pair + 1) * V7X_LANES]
        acc = jnp.zeros((q, V7X_LANES), F32)
        for side in range(2):
            r = 2 * pair + side
            seg = a_cum[:, r:r + 1] - a_cum_t[r:r + 1, :]
            decay = jnp.exp2(jnp.where(causal, seg, -jnp.inf))
            m_r = (cb * decay).astype(BF16)
            acc = acc + _dot(m_r, xdt_pair * keep[side])
        pairs.append(acc)
    y_diag = jnp.concatenate(pairs, axis=-1)

    state = state_ref[...]
    y_off = _dot(cm_bf, state.astype(BF16)) * decay_in_b
    xw = (xdt * decay_out_b).astype(BF16)
    new_state = lax.dot_general(bm_bf, xw, (((0,), (0,)), ((), ())), preferred_element_type=F32)
    state_ref[...] = state * decay_in_b[q - 1:q, :] + new_state

    y = y_diag + y_off + dskip * xs
    g = y * _silu(z)
    return _rms_normalize(g, nw).astype(BF16)


def _mixers_kernel(*refs, groups, n_side):
    refs = list(refs)
    take = lambda n: [refs.pop(0) for _ in range(n)]
    n_pool = len(POOL_WINDOWS)
    z_refs = take(groups)
    xs_ref, b_ref, c_ref, dt_ref, dtb_ref, alog_ref, dskip_ref, nw_ref, expand_ref = take(9)
    u_refs, halo_refs = take(n_pool), take(n_pool)
    pw_ref, ps_ref = take(2)
    side_in = take(n_side)
    o_ref, yp_ref = take(2)
    side_out = take(n_side)
    state_ref, = refs
    q = SSD_CHUNK

    _pool_tile(pl.program_id(1), u_refs, halo_refs, pw_ref, ps_ref, yp_ref)

    for src_ref, dst_ref in zip(side_in, side_out):
        dst_ref[...] = src_ref[...].astype(BF16)

    @pl.when(pl.program_id(1) == 0)
    def _():
        state_ref[...] = jnp.zeros_like(state_ref)

    row_i = lax.broadcasted_iota(jnp.int32, (q, q), 0)
    col_i = lax.broadcasted_iota(jnp.int32, (q, q), 1)
    causal = row_i >= col_i
    tril = jnp.where(causal, 1.0, 0.0).astype(BF16)
    expand = expand_ref[...]
    for g in range(groups):
        wide = slice(g * GROUP_WIDTH, (g + 1) * GROUP_WIDTH)
        narrow = slice(g * D_STATE, (g + 1) * D_STATE)
        o_ref[:, wide] = _ssd_group(
            z_refs[g][...].astype(F32), xs_ref[:, wide].astype(F32), b_ref[:, narrow], c_ref[:, narrow],
            dt_ref[:, narrow].astype(F32),
            dtb_ref[:, narrow], alog_ref[:, narrow], dskip_ref[:, wide], nw_ref[:, wide],
            state_ref.at[g], expand, tril, causal)


def _mixers(proj, dtb_c, alog_c, dskip_full, norm_w, pool_w, pool_scale, side_weights, *, groups):
    m = proj.shape[0]
    q = SSD_CHUNK
    gw = groups * GROUP_WIDTH
    ds = groups * D_STATE
    assert V7X_LANES == D_STATE and N_GROUPS % groups == 0

    def col(start, width):
        assert start % width == 0
        return start // width

    row_spec = lambda width, block0: pl.BlockSpec((q, width), lambda g, c: (c, block0 + g))
    par_spec = lambda width: pl.BlockSpec((1, width), lambda g, c: (0, g))
    z_specs = [
        pl.BlockSpec((q, GROUP_WIDTH),
                     lambda g, c, k=k: (c, col(COL_Z, GROUP_WIDTH) + g * groups + k))
        for k in range(groups)]
    expand = _expansion_matrix()
    assert groups == N_GROUPS
    side_specs = _side_cast_specs(side_weights, m // q, lambda g, c: c)
    return pl.pallas_call(
        functools.partial(_mixers_kernel, groups=groups, n_side=len(side_weights)),
        out_shape=(jax.ShapeDtypeStruct((m, D_SSM), BF16), jax.ShapeDtypeStruct((m, D_POOL), BF16),
                   *[jax.ShapeDtypeStruct(w.shape, BF16) for w in side_weights]),
        grid=(N_GROUPS // groups, m // q),
        in_specs=z_specs + [
            row_spec(gw, col(COL_XS, gw)),
            row_spec(ds, col(COL_B, ds)),
            row_spec(ds, col(COL_C, ds)),
            row_spec(ds, col(COL_DT, ds)),
            par_spec(ds),
            par_spec(ds),
            par_spec(gw),
            par_spec(gw),
            pl.BlockSpec(expand.shape, lambda g, c: (0, 0)),
        ] + _pool_specs(q, lambda g, c: c) + [
            pl.BlockSpec(pool_w.shape, lambda g, c: (0, 0, 0)),
            pl.BlockSpec(pool_scale.shape, lambda g, c: (0, 0)),
        ] + side_specs,
        out_specs=(pl.BlockSpec((q, gw), lambda g, c: (c, g)),
                   pl.BlockSpec((q, D_POOL), lambda g, c: (c, 0)), *side_specs),
        scratch_shapes=[pltpu.VMEM((groups, D_STATE, GROUP_WIDTH), F32)],
        compiler_params=pltpu.CompilerParams(
            dimension_semantics=("parallel", "arbitrary"),
            vmem_limit_bytes=V7X_VMEM_LIMIT_BYTES),
        name="mixers",
    )(*([proj] * (groups + 4)), dtb_c, alog_c, dskip_full, norm_w, expand,
      *([proj] * (2 * len(POOL_WINDOWS))), pool_w, pool_scale, *side_weights)


def _pad_heads_to_lanes(v):
    v = jnp.tile(v.reshape(N_GROUPS, HEADS_PER_GROUP), (1, N_REPLICAS))
    v = jnp.pad(v, ((0, 0), (0, V7X_LANES - N_REPLICAS * HEADS_PER_GROUP)))
    return v.reshape(1, N_GROUPS * V7X_LANES)


def kernel(x, attn_norm_w, w_in, conv_w, conv_b, dt_bias, a_log, d_skip, ssd_norm_w, pool_w,
           pool_scale, w_out, ffn_norm_w, w_gate, w_up, w_down, final_norm_w):
    bsz, seqlen, d_model = x.shape
    depth = w_in.shape[0]
    h = x.reshape(bsz * seqlen, d_model)
    assert bsz == 1 and seqlen % 2048 == 0 and d_model == D_MODEL

    for i in range(depth):
        w_cat = _w_in_prep(w_in[i].T, tk=256)
        proj, w_out_bf = _in_proj(
            h, attn_norm_w[i][None, :], w_cat, conv_w[i], conv_b[i][None, :],
            (w_out[i],), tm=1024, tn=1536, side_slabs=32)

        y_ssd, y_pool, w_gate_bf, w_up_bf = _mixers(
            proj, _pad_heads_to_lanes(dt_bias[i]), _pad_heads_to_lanes(a_log[i]),
            jnp.repeat(d_skip[i], HEAD_DIM)[None, :], ssd_norm_w[i][None, :],
            pool_w[i], pool_scale[i][None, :], (w_gate[i], w_up[i]), groups=N_GROUPS)

        h, hn = _out_proj(y_ssd, y_pool, w_out_bf, h, ffn_norm_w[i][None, :], tm=512)

        act, w_down_bf = _swiglu(hn, w_gate_bf, w_up_bf, (w_down[i],), tm=2048, tn=512)
        last = i == depth - 1
        h = _down_proj(act, w_down_bf, h, final_norm_w[None, :], tm=512, final_norm=last)

    return h.reshape(bsz, seqlen, d_model)
```

```python
import functools

import jax
import jax.numpy as jnp
from jax import lax
from jax.experimental import pallas as pl
from jax.experimental.pallas import tpu as pltpu

F32 = jnp.float32
BF16 = jnp.bfloat16

NORM_EPS = 1e-5
LOG2_E = 1.4426950408889634
D_MODEL = 2048
D_SSM = 2048
HEAD_DIM = 64
N_HEADS = D_SSM // HEAD_DIM
N_GROUPS = 4
HEADS_PER_GROUP = N_HEADS // N_GROUPS
GROUP_WIDTH = D_SSM // N_GROUPS
D_STATE = 128
CONV_WIDTH = 4
D_POOL = 2048
POOL_WINDOWS = (2, 4, 8, 16)
POOL_GROUP_DIM = D_POOL // len(POOL_WINDOWS)
D_MIX = D_SSM + D_POOL

V7X_LANES = 128
V7X_SUBLANES = 8
V7X_VMEM_LIMIT_BYTES = 60 * 1024 * 1024

COL_XS = 0
COL_B = COL_XS + D_SSM
COL_C = COL_B + N_GROUPS * D_STATE
D_CONV = COL_C + N_GROUPS * D_STATE
COL_Z = D_CONV
COL_U = COL_Z + D_SSM
COL_DT = COL_U + D_POOL
D_PROJ = COL_DT + N_GROUPS * V7X_LANES

SSD_CHUNK = 256
N_EXPANDED = 3
N_REPLICAS = 3 * N_EXPANDED
HALO_ROWS = V7X_SUBLANES
POOL_HALO = 16


def _sigmoid(x):
    return 1.0 / (1.0 + jnp.exp(-x))


def _silu(x):
    return x * _sigmoid(x)


def _rms_normalize(x, w):
    ms = jnp.mean(x * x, axis=-1, keepdims=True)
    return x * lax.rsqrt(ms + NORM_EPS) * w


def _split3(x):
    hi = x.astype(BF16)
    r1 = x - hi.astype(F32)
    mid = r1.astype(BF16)
    lo = (r1 - mid.astype(F32)).astype(BF16)
    return hi, mid, lo


def _dot(a, b):
    return jnp.dot(a, b, preferred_element_type=F32)


def _dot_nt(a, b):
    return lax.dot_general(a, b, (((1,), (1,)), ((), ())), preferred_element_type=F32)


def _dot3_left(lhs_bf16, x):
    hi, mid, lo = _split3(x)
    return _dot(lhs_bf16, hi) + _dot(lhs_bf16, mid) + _dot(lhs_bf16, lo)


SRC_XBC = D_SSM
SRC_DT = SRC_XBC + D_SSM + 2 * N_GROUPS * D_STATE
SRC_U = SRC_DT + N_HEADS
D_IN_PROJ = SRC_U + D_POOL


def _w_in_prep_kernel(w_ref, o_ref):
    o_ref[COL_Z:COL_Z + D_SSM, :] = w_ref[:D_SSM, :].astype(BF16)
    o_ref[COL_U:COL_U + D_POOL, :] = w_ref[SRC_U:SRC_U + D_POOL, :].astype(BF16)
    o_ref[COL_XS:D_CONV, :] = w_ref[SRC_XBC:SRC_DT, :].astype(BF16)
    zeros = jnp.zeros((V7X_LANES - N_REPLICAS * HEADS_PER_GROUP, o_ref.shape[1]), F32)
    for g in range(N_GROUPS):
        row0 = COL_DT + g * V7X_LANES
        src0 = SRC_DT + g * HEADS_PER_GROUP
        heads = w_ref[src0:src0 + HEADS_PER_GROUP, :]
        block = jnp.concatenate([heads] * N_REPLICAS + [zeros], axis=0)
        o_ref[row0:row0 + V7X_LANES, :] = block.astype(BF16)


def _w_in_prep(w_t, *, tk):
    n, k = w_t.shape
    assert n == D_IN_PROJ
    return pl.pallas_call(
        _w_in_prep_kernel,
        out_shape=jax.ShapeDtypeStruct((D_PROJ, k), BF16),
        grid=(k // tk,),
        in_specs=[pl.BlockSpec((n, tk), lambda i: (0, i))],
        out_specs=pl.BlockSpec((D_PROJ, tk), lambda i: (0, i)),
        compiler_params=pltpu.CompilerParams(
            dimension_semantics=("parallel",),
            vmem_limit_bytes=V7X_VMEM_LIMIT_BYTES),
        name="w_in_prep",
    )(w_t)


def _side_cast_specs(weights, n_slabs, step_of):
    specs = []
    for w in weights:
        rows = w.shape[0] // n_slabs
        assert rows * n_slabs == w.shape[0] and rows % (2 * V7X_SUBLANES) == 0
        specs.append(pl.BlockSpec(
            (rows, w.shape[1]), lambda *g: (jnp.minimum(step_of(*g), n_slabs - 1), 0)))
    return specs


CONV_CHUNK = 256


def _conv_silu(r, halo, w_ref, b_ref, cols):
    ext = jnp.concatenate([halo, r], axis=0)
    acc = r * w_ref[CONV_WIDTH - 1:CONV_WIDTH, cols] + b_ref[:, cols]
    for back in range(1, CONV_WIDTH):
        tap = w_ref[CONV_WIDTH - 1 - back:CONV_WIDTH - back, cols]
        acc = acc + pltpu.roll(ext, back, axis=0)[HALO_ROWS:] * tap
    return _silu(acc)


def _in_proj_kernel(x_ref, nw_ref, w_ref, cw_ref, cb_ref, *rest, n_conv_blocks, n_side):
    side_in, rest = rest[:n_side], rest[n_side:]
    o_ref, side_out = rest[0], rest[1:1 + n_side]
    hn_ref, halo_ref, raw_ref = rest[1 + n_side:]
    i = pl.program_id(0)
    j = pl.program_id(1)
    tm, tn = o_ref.shape

    for src_ref, dst_ref in zip(side_in, side_out):
        dst_ref[...] = src_ref[...].astype(BF16)

    @pl.when(j == 0)
    def _():
        hn_ref[...] = _rms_normalize(x_ref[...], nw_ref[...]).astype(BF16)

    @pl.when((i == 0) & (j == 0))
    def _():
        halo_ref[...] = jnp.zeros_like(halo_ref)

    @pl.when(j < n_conv_blocks)
    def _():
        n_pieces = tn // CONV_CHUNK
        piece = lambda c: slice(c * CONV_CHUNK, (c + 1) * CONV_CHUNK)
        slot = lambda c: lax.rem(j + c, 2)
        raw_ref[slot(0)] = _dot_nt(hn_ref[...], w_ref[piece(0), :])
        for c in range(n_pieces):
            cols = piece(c)
            if c + 1 < n_pieces:
                raw_ref[slot(c + 1)] = _dot_nt(hn_ref[...], w_ref[piece(c + 1), :])
            r = raw_ref[slot(c)]
            o_ref[:, cols] = _conv_silu(r, halo_ref[j, :, cols], cw_ref, cb_ref, cols)
            halo_ref[j, :, cols] = r[tm - HALO_ROWS:]

    @pl.when(j >= n_conv_blocks)
    def _():
        o_ref[...] = _dot_nt(hn_ref[...], w_ref[...])


def _in_proj(x, nw, w_t, conv_w, conv_b, side_weights, *, tm, tn, side_slabs):
    m, k = x.shape
    n = w_t.shape[0]
    assert D_CONV % tn == 0 and tn % CONV_CHUNK == 0
    n_conv_blocks = D_CONV // tn
    n_col_blocks = n // tn
    assert (m // tm) * n_col_blocks >= side_slabs
    conv_block = lambda i, j: (0, jnp.minimum(j, n_conv_blocks - 1))
    side_specs = _side_cast_specs(side_weights, side_slabs, lambda i, j: i * n_col_blocks + j)
    return pl.pallas_call(
        functools.partial(_in_proj_kernel, n_conv_blocks=n_conv_blocks, n_side=len(side_weights)),
        out_shape=(jax.ShapeDtypeStruct((m, n), F32),
                   *[jax.ShapeDtypeStruct(w.shape, BF16) for w in side_weights]),
        grid=(m // tm, n_col_blocks),
        in_specs=[
            pl.BlockSpec((tm, k), lambda i, j: (i, 0)),
            pl.BlockSpec((1, k), lambda i, j: (0, 0)),
            pl.BlockSpec((tn, k), lambda i, j: (j, 0)),
            pl.BlockSpec((CONV_WIDTH, tn), conv_block),
            pl.BlockSpec((1, tn), conv_block),
        ] + side_specs,
        out_specs=(pl.BlockSpec((tm, tn), lambda i, j: (i, j)), *side_specs),
        scratch_shapes=[pltpu.VMEM((tm, k), BF16),
                        pltpu.VMEM((n_conv_blocks, HALO_ROWS, tn), F32),
                        pltpu.VMEM((2, tm, CONV_CHUNK), F32)],
        compiler_params=pltpu.CompilerParams(
            dimension_semantics=("arbitrary", "arbitrary"),
            vmem_limit_bytes=V7X_VMEM_LIMIT_BYTES),
        name="in_proj",
    )(x, nw, w_t, conv_w, conv_b, *side_weights)


def _swiglu_kernel(hn_ref, wg_ref, wu_ref, *rest):
    n_side = (len(rest) - 1) // 2
    side_in, o_ref, side_out = rest[:n_side], rest[n_side], rest[n_side + 1:]
    for src_ref, dst_ref in zip(side_in, side_out):
        dst_ref[...] = src_ref[...].astype(BF16)
    half = o_ref.shape[1] // 2
    for cols in (slice(0, half), slice(half, 2 * half)):
        gate = _dot(hn_ref[...], wg_ref[:, cols])
        up = _dot(hn_ref[...], wu_ref[:, cols])
        o_ref[:, cols] = (_silu(gate) * up).astype(BF16)


def _swiglu(hn, wg, wu, side_weights, *, tm, tn):
    m, k = hn.shape
    n = wg.shape[1]
    n_col_tiles = n // tn
    side_specs = _side_cast_specs(side_weights, (m // tm) * n_col_tiles,
                                  lambda i, j: i * n_col_tiles + j)
    return pl.pallas_call(
        _swiglu_kernel,
        out_shape=(jax.ShapeDtypeStruct((m, n), BF16),
                   *[jax.ShapeDtypeStruct(w.shape, BF16) for w in side_weights]),
        grid=(m // tm, n_col_tiles),
        in_specs=[
            pl.BlockSpec((tm, k), lambda i, j: (i, 0)),
            pl.BlockSpec((k, tn), lambda i, j: (0, j)),
            pl.BlockSpec((k, tn), lambda i, j: (0, j)),
        ] + side_specs,
        out_specs=(pl.BlockSpec((tm, tn), lambda i, j: (i, j)), *side_specs),
        compiler_params=pltpu.CompilerParams(
            dimension_semantics=("arbitrary", "arbitrary"),
            vmem_limit_bytes=V7X_VMEM_LIMIT_BYTES),
        name="ffn_up",
    )(hn, wg, wu, *side_weights)


def _out_proj_kernel(ys_ref, yp_ref, w_ref, x_ref, nw_ref, h_ref, hn_ref):
    k = ys_ref.shape[1]
    h = x_ref[...] + _dot(ys_ref[...], w_ref[:k, :]) + _dot(yp_ref[...], w_ref[k:, :])
    h_ref[...] = h
    hn_ref[...] = _rms_normalize(h, nw_ref[...]).astype(BF16)


def _resident(shape):
    return pl.BlockSpec(shape, lambda i: (0,) * len(shape), pipeline_mode=pl.Buffered(1))


def _out_proj(ys, yp, w, x, nw, *, tm):
    m, k = ys.shape
    n = w.shape[1]
    row_spec = lambda width: pl.BlockSpec((tm, width), lambda i: (i, 0))
    return pl.pallas_call(
        _out_proj_kernel,
        out_shape=(jax.ShapeDtypeStruct((m, n), F32), jax.ShapeDtypeStruct((m, n), BF16)),
        grid=(m // tm,),
        in_specs=[row_spec(k), row_spec(k), _resident(w.shape), row_spec(n), _resident((1, n))],
        out_specs=(row_spec(n), row_spec(n)),
        compiler_params=pltpu.CompilerParams(
            dimension_semantics=("parallel",),
            vmem_limit_bytes=V7X_VMEM_LIMIT_BYTES),
        name="out_proj",
    )(ys, yp, w, x, nw)


def _down_kernel(a_ref, w_ref, h_ref, nw_ref, o_ref, *, final_norm):
    h = h_ref[...] + _dot(a_ref[...], w_ref[...])
    o_ref[...] = _rms_normalize(h, nw_ref[...]) if final_norm else h


def _down_proj(act, w, h, nw, *, tm, final_norm):
    m, k = act.shape
    n = w.shape[1]
    row_spec = lambda width: pl.BlockSpec((tm, width), lambda i: (i, 0))
    return pl.pallas_call(
        functools.partial(_down_kernel, final_norm=final_norm),
        out_shape=jax.ShapeDtypeStruct((m, n), F32),
        grid=(m // tm,),
        in_specs=[row_spec(k), _resident(w.shape), row_spec(n), _resident((1, n))],
        out_specs=row_spec(n),
        compiler_params=pltpu.CompilerParams(
            dimension_semantics=("parallel",),
            vmem_limit_bytes=V7X_VMEM_LIMIT_BYTES),
        name="ffn_down",
    )(act, w, h, nw)


def _pool_tile(i, u_refs, halo_refs, pw_ref, ps_ref, o_ref):
    tl = o_ref.shape[0]
    t = i * tl + lax.broadcasted_iota(jnp.int32, (tl, POOL_GROUP_DIM), 0)
    for g, window in enumerate(POOL_WINDOWS):
        cols = slice(g * POOL_GROUP_DIM, (g + 1) * POOL_GROUP_DIM)
        u = u_refs[g][...]
        halo = jnp.where(i == 0, 0.0, halo_refs[g][...])
        s = jnp.concatenate([halo, u], axis=0)
        shift = 1
        while shift < window:
            s = s + pltpu.roll(s, shift, axis=0)
            shift *= 2
        count = jnp.minimum(t + 1, window).astype(F32)
        pooled = s[POOL_HALO:] / count - u
        y = _dot(pooled.astype(BF16), pw_ref[g].astype(BF16)) * ps_ref[:, cols]
        o_ref[:, cols] = y.astype(BF16)


def _pool_specs(tl, step_of):
    gd = POOL_GROUP_DIM
    assert COL_U % gd == 0 and tl % POOL_HALO == 0
    halo_blocks_per_tile = tl // POOL_HALO
    n_groups = len(POOL_WINDOWS)
    u_specs = [pl.BlockSpec((tl, gd), lambda *g, k=k: (step_of(*g), COL_U // gd + k))
               for k in range(n_groups)]
    halo_specs = [
        pl.BlockSpec((POOL_HALO, gd),
                     lambda *g, k=k: (jnp.maximum(step_of(*g) * halo_blocks_per_tile - 1, 0),
                                      COL_U // gd + k))
        for k in range(n_groups)]
    return u_specs + halo_specs


assert N_REPLICAS * HEADS_PER_GROUP <= V7X_LANES


def _expansion_matrix():
    lane = jnp.arange(V7X_LANES)
    col = jnp.arange(N_EXPANDED * GROUP_WIDTH)
    replica, head = lane // HEADS_PER_GROUP, lane % HEADS_PER_GROUP
    quantity = replica // 3
    hit = ((quantity[:, None] == col[None, :] // GROUP_WIDTH)
           & (head[:, None] == (col[None, :] % GROUP_WIDTH) // HEAD_DIM)
           & (replica[:, None] < N_REPLICAS))
    return hit.astype(BF16)


def _ssd_group(z, xs, bm, cm, dt_raw, dtb, alog, dskip, nw, state_ref, expand, tril, causal):
    q = xs.shape[0]
    x = dt_raw + dtb
    dt = jnp.maximum(x, 0.0) + jnp.log1p(jnp.exp(-jnp.abs(x)))
    a = -jnp.exp(alog)
    a_cum = _dot3_left(tril, dt * a) * LOG2_E
    a_cum_t = a_cum.T
    a_last = a_cum[q - 1:q, :]

    replica = lax.broadcasted_iota(jnp.int32, (1, V7X_LANES), 1) // HEADS_PER_GROUP
    packed = jnp.where(replica < 3, dt,
                       jnp.where(replica < 6, jnp.exp2(a_cum), jnp.exp2(a_last - a_cum)))
    hi = packed.astype(BF16)
    r1 = packed - hi.astype(F32)
    r2 = r1 - r1.astype(BF16).astype(F32)
    level = replica % 3
    split = jnp.where(level == 0, packed, jnp.where(level == 1, r1, r2)).astype(BF16)
    expanded = _dot(split, expand)
    dt_b = expanded[:, :GROUP_WIDTH]
    decay_in_b = expanded[:, GROUP_WIDTH:2 * GROUP_WIDTH]
    decay_out_b = expanded[:, 2 * GROUP_WIDTH:]

    xdt = xs * dt_b
    bm_bf = bm.astype(BF16)
    cm_bf = cm.astype(BF16)
    cb = _dot_nt(cm_bf, bm_bf)

    xdt_bf = xdt.astype(BF16)
    lane = lax.broadcasted_iota(jnp.int32, (1, V7X_LANES), 1)
    keep = [jnp.where((lane // HEAD_DIM) == side, 1.0, 0.0).astype(BF16) for side in range(2)]
    pairs = []
    for pair in range(HEADS_PER_GROUP // 2):
        xdt_pair = xdt_bf[:, pair * V7X_LANES:(pair + 1) * V7X_LANES]
        acc = jnp.zeros((q, V7X_LANES), F32)
        for side in range(2):
            r = 2 * pair + side
            seg = a_cum[:, r:r + 1] - a_cum_t[r:r + 1, :]
            decay = jnp.exp2(jnp.where(causal, seg, -jnp.inf))
            m_r = (cb * decay).astype(BF16)
            acc = acc + _dot(m_r, xdt_pair * keep[side])
        pairs.append(acc)
    y_diag = jnp.concatenate(pairs, axis=-1)

    state = state_ref[...]
    y_off = _dot(cm_bf, state.astype(BF16)) * decay_in_b
    xw = (xdt * decay_out_b).astype(BF16)
    new_state = lax.dot_general(bm_bf, xw, (((0,), (0,)), ((), ())), preferred_element_type=F32)
    state_ref[...] = state * decay_in_b[q - 1:q, :] + new_state

    y = y_diag + y_off + dskip * xs
    g = y * _silu(z)
    return _rms_normalize(g, nw).astype(BF16)


def _mixers_kernel(*refs, groups, n_side):
    refs = list(refs)
    take = lambda n: [refs.pop(0) for _ in range(n)]
    n_pool = len(POOL_WINDOWS)
    z_refs = take(groups)
    xs_ref, b_ref, c_ref, dt_ref, dtb_ref, alog_ref, dskip_ref, nw_ref, expand_ref = take(9)
    u_refs, halo_refs = take(n_pool), take(n_pool)
    pw_ref, ps_ref = take(2)
    side_in = take(n_side)
    o_ref, yp_ref = take(2)
    side_out = take(n_side)
    state_ref, = refs
    q = SSD_CHUNK

    _pool_tile(pl.program_id(1), u_refs, halo_refs, pw_ref, ps_ref, yp_ref)

    for src_ref, dst_ref in zip(side_in, side_out):
        dst_ref[...] = src_ref[...].astype(BF16)

    @pl.when(pl.program_id(1) == 0)
    def _():
        state_ref[...] = jnp.zeros_like(state_ref)

    row_i = lax.broadcasted_iota(jnp.int32, (q, q), 0)
    col_i = lax.broadcasted_iota(jnp.int32, (q, q), 1)
    causal = row_i >= col_i
    tril = jnp.where(causal, 1.0, 0.0).astype(BF16)
    expand = expand_ref[...]
    for g in range(groups):
        wide = slice(g * GROUP_WIDTH, (g + 1) * GROUP_WIDTH)
        narrow = slice(g * D_STATE, (g + 1) * D_STATE)
        o_ref[:, wide] = _ssd_group(
            z_refs[g][...], xs_ref[:, wide], b_ref[:, narrow], c_ref[:, narrow], dt_ref[:, narrow],
            dtb_ref[:, narrow], alog_ref[:, narrow], dskip_ref[:, wide], nw_ref[:, wide],
            state_ref.at[g], expand, tril, causal)


def _mixers(proj, dtb_c, alog_c, dskip_full, norm_w, pool_w, pool_scale, side_weights, *, groups):
    m = proj.shape[0]
    q = SSD_CHUNK
    gw = groups * GROUP_WIDTH
    ds = groups * D_STATE
    assert V7X_LANES == D_STATE and N_GROUPS % groups == 0

    def col(start, width):
        assert start % width == 0
        return start // width

    row_spec = lambda width, block0: pl.BlockSpec((q, width), lambda g, c: (c, block0 + g))
    par_spec = lambda width: pl.BlockSpec((1, width), lambda g, c: (0, g))
    z_specs = [
        pl.BlockSpec((q, GROUP_WIDTH),
                     lambda g, c, k=k: (c, col(COL_Z, GROUP_WIDTH) + g * groups + k))
        for k in range(groups)]
    expand = _expansion_matrix()
    assert groups == N_GROUPS
    side_specs = _side_cast_specs(side_weights, m // q, lambda g, c: c)
    return pl.pallas_call(
        functools.partial(_mixers_kernel, groups=groups, n_side=len(side_weights)),
        out_shape=(jax.ShapeDtypeStruct((m, D_SSM), BF16), jax.ShapeDtypeStruct((m, D_POOL), BF16),
                   *[jax.ShapeDtypeStruct(w.shape, BF16) for w in side_weights]),
        grid=(N_GROUPS // groups, m // q),
        in_specs=z_specs + [
            row_spec(gw, col(COL_XS, gw)),
            row_spec(ds, col(COL_B, ds)),
            row_spec(ds, col(COL_C, ds)),
            row_spec(ds, col(COL_DT, ds)),
            par_spec(ds),
            par_spec(ds),
            par_spec(gw),
            par_spec(gw),
            pl.BlockSpec(expand.shape, lambda g, c: (0, 0)),
        ] + _pool_specs(q, lambda g, c: c) + [
            pl.BlockSpec(pool_w.shape, lambda g, c: (0, 0, 0)),
            pl.BlockSpec(pool_scale.shape, lambda g, c: (0, 0)),
        ] + side_specs,
        out_specs=(pl.BlockSpec((q, gw), lambda g, c: (c, g)),
                   pl.BlockSpec((q, D_POOL), lambda g, c: (c, 0)), *side_specs),
        scratch_shapes=[pltpu.VMEM((groups, D_STATE, GROUP_WIDTH), F32)],
        compiler_params=pltpu.CompilerParams(
            dimension_semantics=("parallel", "arbitrary"),
            vmem_limit_bytes=V7X_VMEM_LIMIT_BYTES),
        name="mixers",
    )(*([proj] * (groups + 4)), dtb_c, alog_c, dskip_full, norm_w, expand,
      *([proj] * (2 * len(POOL_WINDOWS))), pool_w, pool_scale, *side_weights)


def _pad_heads_to_lanes(v):
    v = jnp.tile(v.reshape(N_GROUPS, HEADS_PER_GROUP), (1, N_REPLICAS))
    v = jnp.pad(v, ((0, 0), (0, V7X_LANES - N_REPLICAS * HEADS_PER_GROUP)))
    return v.reshape(1, N_GROUPS * V7X_LANES)


def kernel(x, attn_norm_w, w_in, conv_w, conv_b, dt_bias, a_log, d_skip, ssd_norm_w, pool_w,
           pool_scale, w_out, ffn_norm_w, w_gate, w_up, w_down, final_norm_w):
    bsz, seqlen, d_model = x.shape
    depth = w_in.shape[0]
    h = x.reshape(bsz * seqlen, d_model)
    assert bsz == 1 and seqlen % 2048 == 0 and d_model == D_MODEL

    for i in range(depth):
        w_cat = _w_in_prep(w_in[i].T, tk=256)
        proj, w_out_bf = _in_proj(
            h, attn_norm_w[i][None, :], w_cat, conv_w[i], conv_b[i][None, :],
            (w_out[i],), tm=1024, tn=1536, side_slabs=32)

        y_ssd, y_pool, w_gate_bf, w_up_bf = _mixers(
            proj, _pad_heads_to_lanes(dt_bias[i]), _pad_heads_to_lanes(a_log[i]),
            jnp.repeat(d_skip[i], HEAD_DIM)[None, :], ssd_norm_w[i][None, :],
            pool_w[i], pool_scale[i][None, :], (w_gate[i], w_up[i]), groups=N_GROUPS)

        h, hn = _out_proj(y_ssd, y_pool, w_out_bf, h, ffn_norm_w[i][None, :], tm=512)

        act, w_down_bf = _swiglu(hn, w_gate_bf, w_up_bf, (w_down[i],), tm=2048, tn=512)
        last = i == depth - 1
        h = _down_proj(act, w_down_bf, h, final_norm_w[None, :], tm=512, final_norm=last)

    return h.reshape(bsz, seqlen, d_model)
```

```python
import functools

import jax
import jax.numpy as jnp
from jax import lax
from jax.experimental import pallas as pl
from jax.experimental.pallas import tpu as pltpu

F32 = jnp.float32
BF16 = jnp.bfloat16

NORM_EPS = 1e-5
LOG2_E = 1.4426950408889634
D_MODEL = 2048
D_SSM = 2048
HEAD_DIM = 64
N_HEADS = D_SSM // HEAD_DIM
N_GROUPS = 4
HEADS_PER_GROUP = N_HEADS // N_GROUPS
GROUP_WIDTH = D_SSM // N_GROUPS
D_STATE = 128
CONV_WIDTH = 4
D_POOL = 2048
POOL_WINDOWS = (2, 4, 8, 16)
POOL_GROUP_DIM = D_POOL // len(POOL_WINDOWS)
D_MIX = D_SSM + D_POOL

V7X_LANES = 128
V7X_SUBLANES = 8
V7X_VMEM_LIMIT_BYTES = 60 * 1024 * 1024

COL_XS = 0
COL_B = COL_XS + D_SSM
COL_C = COL_B + N_GROUPS * D_STATE
D_CONV = COL_C + N_GROUPS * D_STATE
COL_Z = D_CONV
COL_U = COL_Z + D_SSM
COL_DT = COL_U + D_POOL
D_PROJ = COL_DT + N_GROUPS * V7X_LANES

SSD_CHUNK = 256
N_EXPANDED = 3
N_REPLICAS = 3 * N_EXPANDED
HALO_ROWS = V7X_SUBLANES
POOL_HALO = 16


def _sigmoid(x):
    return 1.0 / (1.0 + jnp.exp(-x))


def _silu(x):
    return x * _sigmoid(x)


def _rms_normalize(x, w):
    ms = jnp.mean(x * x, axis=-1, keepdims=True)
    return x * lax.rsqrt(ms + NORM_EPS) * w


def _split3(x):
    hi = x.astype(BF16)
    r1 = x - hi.astype(F32)
    mid = r1.astype(BF16)
    lo = (r1 - mid.astype(F32)).astype(BF16)
    return hi, mid, lo


def _dot(a, b):
    return jnp.dot(a, b, preferred_element_type=F32)


def _dot_nt(a, b):
    return lax.dot_general(a, b, (((1,), (1,)), ((), ())), preferred_element_type=F32)


def _dot3_left(lhs_bf16, x):
    hi, mid, lo = _split3(x)
    return _dot(lhs_bf16, hi) + _dot(lhs_bf16, mid) + _dot(lhs_bf16, lo)


SRC_XBC = D_SSM
SRC_DT = SRC_XBC + D_SSM + 2 * N_GROUPS * D_STATE
SRC_U = SRC_DT + N_HEADS
D_IN_PROJ = SRC_U + D_POOL


def _w_in_prep_kernel(w_ref, o_ref):
    o_ref[COL_Z:COL_Z + D_SSM, :] = w_ref[:D_SSM, :].astype(BF16)
    o_ref[COL_U:COL_U + D_POOL, :] = w_ref[SRC_U:SRC_U + D_POOL, :].astype(BF16)
    o_ref[COL_XS:D_CONV, :] = w_ref[SRC_XBC:SRC_DT, :].astype(BF16)
    zeros = jnp.zeros((V7X_LANES - N_REPLICAS * HEADS_PER_GROUP, o_ref.shape[1]), F32)
    for g in range(N_GROUPS):
        row0 = COL_DT + g * V7X_LANES
        src0 = SRC_DT + g * HEADS_PER_GROUP
        heads = w_ref[src0:src0 + HEADS_PER_GROUP, :]
        block = jnp.concatenate([heads] * N_REPLICAS + [zeros], axis=0)
        o_ref[row0:row0 + V7X_LANES, :] = block.astype(BF16)


def _w_in_prep(w_t, *, tk):
    n, k = w_t.shape
    assert n == D_IN_PROJ
    return pl.pallas_call(
        _w_in_prep_kernel,
        out_shape=jax.ShapeDtypeStruct((D_PROJ, k), BF16),
        grid=(k // tk,),
        in_specs=[pl.BlockSpec((n, tk), lambda i: (0, i))],
        out_specs=pl.BlockSpec((D_PROJ, tk), lambda i: (0, i)),
        compiler_params=pltpu.CompilerParams(
            dimension_semantics=("parallel",),
            vmem_limit_bytes=V7X_VMEM_LIMIT_BYTES),
        name="w_in_prep",
    )(w_t)


def _side_cast_specs(weights, n_slabs, step_of):
    specs = []
    for w in weights:
        rows = w.shape[0] // n_slabs
        assert rows * n_slabs == w.shape[0] and rows % (2 * V7X_SUBLANES) == 0
        specs.append(pl.BlockSpec(
            (rows, w.shape[1]), lambda *g: (jnp.minimum(step_of(*g), n_slabs - 1), 0)))
    return specs


CONV_CHUNK = 256
N_PARK_SLOTS = 1


def _conv_silu(r, halo, w_ref, b_ref, cols):
    ext = jnp.concatenate([halo, r], axis=0)
    acc = r * w_ref[CONV_WIDTH - 1:CONV_WIDTH, cols] + b_ref[:, cols]
    for back in range(1, CONV_WIDTH):
        tap = w_ref[CONV_WIDTH - 1 - back:CONV_WIDTH - back, cols]
        acc = acc + pltpu.roll(ext, back, axis=0)[HALO_ROWS:] * tap
    return _silu(acc)


def _in_proj_kernel(x_ref, nw_ref, w_ref, cw_ref, cb_ref, *rest, n_conv_blocks, n_side):
    side_in, rest = rest[:n_side], rest[n_side:]
    o_ref, side_out = rest[0], rest[1:1 + n_side]
    hn_ref, halo_ref, raw_ref = rest[1 + n_side:]
    i = pl.program_id(0)
    j = pl.program_id(1)
    tm, tn = o_ref.shape

    for src_ref, dst_ref in zip(side_in, side_out):
        dst_ref[...] = src_ref[...].astype(BF16)

    @pl.when(j == 0)
    def _():
        hn_ref[...] = _rms_normalize(x_ref[...], nw_ref[...]).astype(BF16)

    @pl.when((i == 0) & (j == 0))
    def _():
        halo_ref[...] = jnp.zeros_like(halo_ref)

    @pl.when(j < n_conv_blocks)
    def _():
        n_pieces = tn // CONV_CHUNK
        piece = lambda c: slice(c * CONV_CHUNK, (c + 1) * CONV_CHUNK)
        slot = lambda c: jnp.minimum(j, 0)
        raw_ref[slot(0)] = _dot_nt(hn_ref[...], w_ref[piece(0), :])
        for c in range(n_pieces):
            cols = piece(c)
            r = raw_ref[slot(c)]
            o_ref[:, cols] = _conv_silu(r, halo_ref[j, :, cols], cw_ref, cb_ref, cols)
            halo_ref[j, :, cols] = r[tm - HALO_ROWS:]
            if c + 1 < n_pieces:
                raw_ref[slot(c + 1)] = _dot_nt(hn_ref[...], w_ref[piece(c + 1), :])

    @pl.when(j >= n_conv_blocks)
    def _():
        o_ref[...] = _dot_nt(hn_ref[...], w_ref[...])


def _in_proj(x, nw, w_t, conv_w, conv_b, side_weights, *, tm, tn, side_slabs):
    m, k = x.shape
    n = w_t.shape[0]
    assert D_CONV % tn == 0 and tn % CONV_CHUNK == 0
    n_conv_blocks = D_CONV // tn
    n_col_blocks = n // tn
    assert (m // tm) * n_col_blocks >= side_slabs
    conv_block = lambda i, j: (0, jnp.minimum(j, n_conv_blocks - 1))
    side_specs = _side_cast_specs(side_weights, side_slabs, lambda i, j: i * n_col_blocks + j)
    return pl.pallas_call(
        functools.partial(_in_proj_kernel, n_conv_blocks=n_conv_blocks, n_side=len(side_weights)),
        out_shape=(jax.ShapeDtypeStruct((m, n), F32),
                   *[jax.ShapeDtypeStruct(w.shape, BF16) for w in side_weights]),
        grid=(m // tm, n_col_blocks),
        in_specs=[
            pl.BlockSpec((tm, k), lambda i, j: (i, 0)),
            pl.BlockSpec((1, k), lambda i, j: (0, 0)),
            pl.BlockSpec((tn, k), lambda i, j: (j, 0)),
            pl.BlockSpec((CONV_WIDTH, tn), conv_block),
            pl.BlockSpec((1, tn), conv_block),
        ] + side_specs,
        out_specs=(pl.BlockSpec((tm, tn), lambda i, j: (i, j)), *side_specs),
        scratch_shapes=[pltpu.VMEM((tm, k), BF16),
                        pltpu.VMEM((n_conv_blocks, HALO_ROWS, tn), F32),
                        pltpu.VMEM((N_PARK_SLOTS, tm, CONV_CHUNK), F32)],
        compiler_params=pltpu.CompilerParams(
            dimension_semantics=("arbitrary", "arbitrary"),
            vmem_limit_bytes=V7X_VMEM_LIMIT_BYTES),
        name="in_proj",
    )(x, nw, w_t, conv_w, conv_b, *side_weights)


def _swiglu_kernel(hn_ref, wg_ref, wu_ref, *rest):
    n_side = (len(rest) - 1) // 2
    side_in, o_ref, side_out = rest[:n_side], rest[n_side], rest[n_side + 1:]
    for src_ref, dst_ref in zip(side_in, side_out):
        dst_ref[...] = src_ref[...].astype(BF16)
    half = o_ref.shape[1] // 2
    for cols in (slice(0, half), slice(half, 2 * half)):
        gate = _dot(hn_ref[...], wg_ref[:, cols])
        up = _dot(hn_ref[...], wu_ref[:, cols])
        o_ref[:, cols] = (_silu(gate) * up).astype(BF16)


def _swiglu(hn, wg, wu, side_weights, *, tm, tn):
    m, k = hn.shape
    n = wg.shape[1]
    n_col_tiles = n // tn
    side_specs = _side_cast_specs(side_weights, (m // tm) * n_col_tiles,
                                  lambda i, j: i * n_col_tiles + j)
    return pl.pallas_call(
        _swiglu_kernel,
        out_shape=(jax.ShapeDtypeStruct((m, n), BF16),
                   *[jax.ShapeDtypeStruct(w.shape, BF16) for w in side_weights]),
        grid=(m // tm, n_col_tiles),
        in_specs=[
            pl.BlockSpec((tm, k), lambda i, j: (i, 0)),
            pl.BlockSpec((k, tn), lambda i, j: (0, j)),
            pl.BlockSpec((k, tn), lambda i, j: (0, j)),
        ] + side_specs,
        out_specs=(pl.BlockSpec((tm, tn), lambda i, j: (i, j)), *side_specs),
        compiler_params=pltpu.CompilerParams(
            dimension_semantics=("arbitrary", "arbitrary"),
            vmem_limit_bytes=V7X_VMEM_LIMIT_BYTES),
        name="ffn_up",
    )(hn, wg, wu, *side_weights)


def _out_proj_kernel(ys_ref, yp_ref, w_ref, x_ref, nw_ref, h_ref, hn_ref):
    k = ys_ref.shape[1]
    h = x_ref[...] + _dot(ys_ref[...], w_ref[:k, :]) + _dot(yp_ref[...], w_ref[k:, :])
    h_ref[...] = h
    hn_ref[...] = _rms_normalize(h, nw_ref[...]).astype(BF16)


def _resident(shape):
    return pl.BlockSpec(shape, lambda i: (0,) * len(shape), pipeline_mode=pl.Buffered(1))


def _out_proj(ys, yp, w, x, nw, *, tm):
    m, k = ys.shape
    n = w.shape[1]
    row_spec = lambda width: pl.BlockSpec((tm, width), lambda i: (i, 0))
    return pl.pallas_call(
        _out_proj_kernel,
        out_shape=(jax.ShapeDtypeStruct((m, n), F32), jax.ShapeDtypeStruct((m, n), BF16)),
        grid=(m // tm,),
        in_specs=[row_spec(k), row_spec(k), _resident(w.shape), row_spec(n), _resident((1, n))],
        out_specs=(row_spec(n), row_spec(n)),
        compiler_params=pltpu.CompilerParams(
            dimension_semantics=("parallel",),
            vmem_limit_bytes=V7X_VMEM_LIMIT_BYTES),
        name="out_proj",
    )(ys, yp, w, x, nw)


def _down_kernel(a_ref, w_ref, h_ref, nw_ref, o_ref, *, final_norm):
    h = h_ref[...] + _dot(a_ref[...], w_ref[...])
    o_ref[...] = _rms_normalize(h, nw_ref[...]) if final_norm else h


def _down_proj(act, w, h, nw, *, tm, final_norm):
    m, k = act.shape
    n = w.shape[1]
    row_spec = lambda width: pl.BlockSpec((tm, width), lambda i: (i, 0))
    return pl.pallas_call(
        functools.partial(_down_kernel, final_norm=final_norm),
        out_shape=jax.ShapeDtypeStruct((m, n), F32),
        grid=(m // tm,),
        in_specs=[row_spec(k), _resident(w.shape), row_spec(n), _resident((1, n))],
        out_specs=row_spec(n),
        compiler_params=pltpu.CompilerParams(
            dimension_semantics=("parallel",),
            vmem_limit_bytes=V7X_VMEM_LIMIT_BYTES),
        name="ffn_down",
    )(act, w, h, nw)


def _pool_tile(i, u_refs, halo_refs, pw_ref, ps_ref, o_ref):
    tl = o_ref.shape[0]
    t = i * tl + lax.broadcasted_iota(jnp.int32, (tl, POOL_GROUP_DIM), 0)
    for g, window in enumerate(POOL_WINDOWS):
        cols = slice(g * POOL_GROUP_DIM, (g + 1) * POOL_GROUP_DIM)
        u = u_refs[g][...]
        halo = jnp.where(i == 0, 0.0, halo_refs[g][...])
        s = jnp.concatenate([halo, u], axis=0)
        shift = 1
        while shift < window:
            s = s + pltpu.roll(s, shift, axis=0)
            shift *= 2
        count = jnp.minimum(t + 1, window).astype(F32)
        pooled = s[POOL_HALO:] / count - u
        y = _dot(pooled.astype(BF16), pw_ref[g].astype(BF16)) * ps_ref[:, cols]
        o_ref[:, cols] = y.astype(BF16)


def _pool_specs(tl, step_of):
    gd = POOL_GROUP_DIM
    assert COL_U % gd == 0 and tl % POOL_HALO == 0
    halo_blocks_per_tile = tl // POOL_HALO
    n_groups = len(POOL_WINDOWS)
    u_specs = [pl.BlockSpec((tl, gd), lambda *g, k=k: (step_of(*g), COL_U // gd + k))
               for k in range(n_groups)]
    halo_specs = [
        pl.BlockSpec((POOL_HALO, gd),
                     lambda *g, k=k: (jnp.maximum(step_of(*g) * halo_blocks_per_tile - 1, 0),
                                      COL_U // gd + k))
        for k in range(n_groups)]
    return u_specs + halo_specs


assert N_REPLICAS * HEADS_PER_GROUP <= V7X_LANES


def _expansion_matrix():
    lane = jnp.arange(V7X_LANES)
    col = jnp.arange(N_EXPANDED * GROUP_WIDTH)
    replica, head = lane // HEADS_PER_GROUP, lane % HEADS_PER_GROUP
    quantity = replica // 3
    hit = ((quantity[:, None] == col[None, :] // GROUP_WIDTH)
           & (head[:, None] == (col[None, :] % GROUP_WIDTH) // HEAD_DIM)
           & (replica[:, None] < N_REPLICAS))
    return hit.astype(BF16)


def _ssd_group(z, xs, bm, cm, dt_raw, dtb, alog, dskip, nw, state_ref, expand, tril, causal):
    q = xs.shape[0]
    x = dt_raw + dtb
    dt = jnp.maximum(x, 0.0) + jnp.log1p(jnp.exp(-jnp.abs(x)))
    a = -jnp.exp(alog)
    a_cum = _dot3_left(tril, dt * a) * LOG2_E
    a_cum_t = a_cum.T
    a_last = a_cum[q - 1:q, :]

    replica = lax.broadcasted_iota(jnp.int32, (1, V7X_LANES), 1) // HEADS_PER_GROUP
    packed = jnp.where(replica < 3, dt,
                       jnp.where(replica < 6, jnp.exp2(a_cum), jnp.exp2(a_last - a_cum)))
    hi = packed.astype(BF16)
    r1 = packed - hi.astype(F32)
    r2 = r1 - r1.astype(BF16).astype(F32)
    level = replica % 3
    split = jnp.where(level == 0, packed, jnp.where(level == 1, r1, r2)).astype(BF16)
    expanded = _dot(split, expand)
    dt_b = expanded[:, :GROUP_WIDTH]
    decay_in_b = expanded[:, GROUP_WIDTH:2 * GROUP_WIDTH]
    decay_out_b = expanded[:, 2 * GROUP_WIDTH:]

    xdt = xs * dt_b
    bm_bf = bm.astype(BF16)
    cm_bf = cm.astype(BF16)
    cb = _dot_nt(cm_bf, bm_bf)

    xdt_bf = xdt.astype(BF16)
    lane = lax.broadcasted_iota(jnp.int32, (1, V7X_LANES), 1)
    keep = [jnp.where((lane // HEAD_DIM) == side, 1.0, 0.0).astype(BF16) for side in range(2)]
    pairs = []
    for pair in range(HEADS_PER_GROUP // 2):
        xdt_pair = xdt_bf[:, pair * V7X_LANES:(pair + 1) * V7X_LANES]
        acc = jnp.zeros((q, V7X_LANES), F32)
        for side in range(2):
            r = 2 * pair + side
            seg = a_cum[:, r:r + 1] - a_cum_t[r:r + 1, :]
            decay = jnp.exp2(jnp.where(causal, seg, -jnp.inf))
            m_r = (cb * decay).astype(BF16)
            acc = acc + _dot(m_r, xdt_pair * keep[side])
        pairs.append(acc)
    y_diag = jnp.concatenate(pairs, axis=-1)

    state = state_ref[...]
    y_off = _dot(cm_bf, state.astype(BF16)) * decay_in_b
    xw = (xdt * decay_out_b).astype(BF16)
    new_state = lax.dot_general(bm_bf, xw, (((0,), (0,)), ((), ())), preferred_element_type=F32)
    state_ref[...] = state * decay_in_b[q - 1:q, :] + new_state

    y = y_diag + y_off + dskip * xs
    g = y * _silu(z)
    return _rms_normalize(g, nw).astype(BF16)


def _mixers_kernel(*refs, groups, n_side):
    refs = list(refs)
    take = lambda n: [refs.pop(0) for _ in range(n)]
    n_pool = len(POOL_WINDOWS)
    z_refs = take(groups)
    xs_ref, b_ref, c_ref, dt_ref, dtb_ref, alog_ref, dskip_ref, nw_ref, expand_ref = take(9)
    u_refs, halo_refs = take(n_pool), take(n_pool)
    pw_ref, ps_ref = take(2)
    side_in = take(n_side)
    o_ref, yp_ref = take(2)
    side_out = take(n_side)
    state_ref, = refs
    q = SSD_CHUNK

    _pool_tile(pl.program_id(1), u_refs, halo_refs, pw_ref, ps_ref, yp_ref)

    for src_ref, dst_ref in zip(side_in, side_out):
        dst_ref[...] = src_ref[...].astype(BF16)

    @pl.when(pl.program_id(1) == 0)
    def _():
        state_ref[...] = jnp.zeros_like(state_ref)

    row_i = lax.broadcasted_iota(jnp.int32, (q, q), 0)
    col_i = lax.broadcasted_iota(jnp.int32, (q, q), 1)
    causal = row_i >= col_i
    tril = jnp.where(causal, 1.0, 0.0).astype(BF16)
    expand = expand_ref[...]
    for g in range(groups):
        wide = slice(g * GROUP_WIDTH, (g + 1) * GROUP_WIDTH)
        narrow = slice(g * D_STATE, (g + 1) * D_STATE)
        o_ref[:, wide] = _ssd_group(
            z_refs[g][...], xs_ref[:, wide], b_ref[:, narrow], c_ref[:, narrow], dt_ref[:, narrow],
            dtb_ref[:, narrow], alog_ref[:, narrow], dskip_ref[:, wide], nw_ref[:, wide],
            state_ref.at[g], expand, tril, causal)


def _mixers(proj, dtb_c, alog_c, dskip_full, norm_w, pool_w, pool_scale, side_weights, *, groups):
    m = proj.shape[0]
    q = SSD_CHUNK
    gw = groups * GROUP_WIDTH
    ds = groups * D_STATE
    assert V7X_LANES == D_STATE and N_GROUPS % groups == 0

    def col(start, width):
        assert start % width == 0
        return start // width

    row_spec = lambda width, block0: pl.BlockSpec((q, width), lambda g, c: (c, block0 + g))
    par_spec = lambda width: pl.BlockSpec((1, width), lambda g, c: (0, g))
    z_specs = [
        pl.BlockSpec((q, GROUP_WIDTH),
                     lambda g, c, k=k: (c, col(COL_Z, GROUP_WIDTH) + g * groups + k))
        for k in range(groups)]
    expand = _expansion_matrix()
    assert groups == N_GROUPS
    side_specs = _side_cast_specs(side_weights, m // q, lambda g, c: c)
    return pl.pallas_call(
        functools.partial(_mixers_kernel, groups=groups, n_side=len(side_weights)),
        out_shape=(jax.ShapeDtypeStruct((m, D_SSM), BF16), jax.ShapeDtypeStruct((m, D_POOL), BF16),
                   *[jax.ShapeDtypeStruct(w.shape, BF16) for w in side_weights]),
        grid=(N_GROUPS // groups, m // q),
        in_specs=z_specs + [
            row_spec(gw, col(COL_XS, gw)),
            row_spec(ds, col(COL_B, ds)),
            row_spec(ds, col(COL_C, ds)),
            row_spec(ds, col(COL_DT, ds)),
            par_spec(ds),
            par_spec(ds),
            par_spec(gw),
            par_spec(gw),
            pl.BlockSpec(expand.shape, lambda g, c: (0, 0)),
        ] + _pool_specs(q, lambda g, c: c) + [
            pl.BlockSpec(pool_w.shape, lambda g, c: (0, 0, 0)),
            pl.BlockSpec(pool_scale.shape, lambda g, c: (0, 0)),
        ] + side_specs,
        out_specs=(pl.BlockSpec((q, gw), lambda g, c: (c, g)),
                   pl.BlockSpec((q, D_POOL), lambda g, c: (c, 0)), *side_specs),
        scratch_shapes=[pltpu.VMEM((groups, D_STATE, GROUP_WIDTH), F32)],
        compiler_params=pltpu.CompilerParams(
            dimension_semantics=("parallel", "arbitrary"),
            vmem_limit_bytes=V7X_VMEM_LIMIT_BYTES),
        name="mixers",
    )(*([proj] * (groups + 4)), dtb_c, alog_c, dskip_full, norm_w, expand,
      *([proj] * (2 * len(POOL_WINDOWS))), pool_w, pool_scale, *side_weights)


def _pad_heads_to_lanes(v):
    v = jnp.tile(v.reshape(N_GROUPS, HEADS_PER_GROUP), (1, N_REPLICAS))
    v = jnp.pad(v, ((0, 0), (0, V7X_LANES - N_REPLICAS * HEADS_PER_GROUP)))
    return v.reshape(1, N_GROUPS * V7X_LANES)


def kernel(x, attn_norm_w, w_in, conv_w, conv_b, dt_bias, a_log, d_skip, ssd_norm_w, pool_w,
           pool_scale, w_out, ffn_norm_w, w_gate, w_up, w_down, final_norm_w):
    bsz, seqlen, d_model = x.shape
    depth = w_in.shape[0]
    h = x.reshape(bsz * seqlen, d_model)
    assert bsz == 1 and seqlen % 2048 == 0 and d_model == D_MODEL

    for i in range(depth):
        w_cat = _w_in_prep(w_in[i].T, tk=256)
        proj, w_out_bf = _in_proj(
            h, attn_norm_w[i][None, :], w_cat, conv_w[i], conv_b[i][None, :],
            (w_out[i],), tm=1024, tn=1536, side_slabs=32)

        y_ssd, y_pool, w_gate_bf, w_up_bf = _mixers(
            proj, _pad_heads_to_lanes(dt_bias[i]), _pad_heads_to_lanes(a_log[i]),
            jnp.repeat(d_skip[i], HEAD_DIM)[None, :], ssd_norm_w[i][None, :],
            pool_w[i], pool_scale[i][None, :], (w_gate[i], w_up[i]), groups=N_GROUPS)

        h, hn = _out_proj(y_ssd, y_pool, w_out_bf, h, ffn_norm_w[i][None, :], tm=512)

        act, w_down_bf = _swiglu(hn, w_gate_bf, w_up_bf, (w_down[i],), tm=2048, tn=512)
        last = i == depth - 1
        h = _down_proj(act, w_down_bf, h, final_norm_w[None, :], tm=512, final_norm=last)

    return h.reshape(bsz, seqlen, d_model)
```

```python
import functools

import jax
import jax.numpy as jnp
from jax import lax
from jax.experimental import pallas as pl
from jax.experimental.pallas import tpu as pltpu

F32 = jnp.float32
BF16 = jnp.bfloat16

NORM_EPS = 1e-5
LOG2_E = 1.4426950408889634
D_MODEL = 2048
D_SSM = 2048
HEAD_DIM = 64
N_HEADS = D_SSM // HEAD_DIM
N_GROUPS = 4
HEADS_PER_GROUP = N_HEADS // N_GROUPS
GROUP_WIDTH = D_SSM // N_GROUPS
D_STATE = 128
CONV_WIDTH = 4
D_POOL = 2048
POOL_WINDOWS = (2, 4, 8, 16)
POOL_GROUP_DIM = D_POOL // len(POOL_WINDOWS)
D_MIX = D_SSM + D_POOL

V7X_LANES = 128
V7X_SUBLANES = 8
V7X_VMEM_LIMIT_BYTES = 60 * 1024 * 1024

COL_XS = 0
COL_B = COL_XS + D_SSM
COL_C = COL_B + N_GROUPS * D_STATE
D_CONV = COL_C + N_GROUPS * D_STATE
COL_Z = D_CONV
COL_U = COL_Z + D_SSM
COL_DT = COL_U + D_POOL
D_PROJ = COL_DT + N_GROUPS * V7X_LANES

SSD_CHUNK = 256
N_EXPANDED = 3
N_REPLICAS = 3 * N_EXPANDED
HALO_ROWS = V7X_SUBLANES
POOL_HALO = 16


def _sigmoid(x):
    return 1.0 / (1.0 + jnp.exp(-x))


def _silu(x):
    return x * _sigmoid(x)


def _rms_normalize(x, w):
    ms = jnp.mean(x * x, axis=-1, keepdims=True)
    return x * lax.rsqrt(ms + NORM_EPS) * w


def _split3(x):
    hi = x.astype(BF16)
    r1 = x - hi.astype(F32)
    mid = r1.astype(BF16)
    lo = (r1 - mid.astype(F32)).astype(BF16)
    return hi, mid, lo


def _dot(a, b):
    return jnp.dot(a, b, preferred_element_type=F32)


def _dot_nt(a, b):
    return lax.dot_general(a, b, (((1,), (1,)), ((), ())), preferred_element_type=F32)


def _dot3_left(lhs_bf16, x):
    hi, mid, lo = _split3(x)
    return _dot(lhs_bf16, hi) + _dot(lhs_bf16, mid) + _dot(lhs_bf16, lo)


SRC_XBC = D_SSM
SRC_DT = SRC_XBC + D_SSM + 2 * N_GROUPS * D_STATE
SRC_U = SRC_DT + N_HEADS
D_IN_PROJ = SRC_U + D_POOL


def _w_in_prep_kernel(w_ref, o_ref):
    o_ref[COL_Z:COL_Z + D_SSM, :] = w_ref[:D_SSM, :].astype(BF16)
    o_ref[COL_U:COL_U + D_POOL, :] = w_ref[SRC_U:SRC_U + D_POOL, :].astype(BF16)
    o_ref[COL_XS:D_CONV, :] = w_ref[SRC_XBC:SRC_DT, :].astype(BF16)
    zeros = jnp.zeros((V7X_LANES - N_REPLICAS * HEADS_PER_GROUP, o_ref.shape[1]), F32)
    for g in range(N_GROUPS):
        row0 = COL_DT + g * V7X_LANES
        src0 = SRC_DT + g * HEADS_PER_GROUP
        heads = w_ref[src0:src0 + HEADS_PER_GROUP, :]
        block = jnp.concatenate([heads] * N_REPLICAS + [zeros], axis=0)
        o_ref[row0:row0 + V7X_LANES, :] = block.astype(BF16)


def _w_in_prep(w_t, *, tk):
    n, k = w_t.shape
    assert n == D_IN_PROJ
    return pl.pallas_call(
        _w_in_prep_kernel,
        out_shape=jax.ShapeDtypeStruct((D_PROJ, k), BF16),
        grid=(k // tk,),
        in_specs=[pl.BlockSpec((n, tk), lambda i: (0, i))],
        out_specs=pl.BlockSpec((D_PROJ, tk), lambda i: (0, i)),
        compiler_params=pltpu.CompilerParams(
            dimension_semantics=("parallel",),
            vmem_limit_bytes=V7X_VMEM_LIMIT_BYTES),
        name="w_in_prep",
    )(w_t)


def _side_cast_specs(weights, n_slabs, step_of):
    specs = []
    for w in weights:
        rows = w.shape[0] // n_slabs
        assert rows * n_slabs == w.shape[0] and rows % (2 * V7X_SUBLANES) == 0
        specs.append(pl.BlockSpec(
            (rows, w.shape[1]), lambda *g: (jnp.minimum(step_of(*g), n_slabs - 1), 0)))
    return specs


CONV_CHUNK = 256
N_PARK_SLOTS = 1


def _conv_silu(r, halo, w_ref, b_ref, cols):
    ext = jnp.concatenate([halo, r], axis=0)
    acc = r * w_ref[CONV_WIDTH - 1:CONV_WIDTH, cols] + b_ref[:, cols]
    for back in range(1, CONV_WIDTH):
        tap = w_ref[CONV_WIDTH - 1 - back:CONV_WIDTH - back, cols]
        acc = acc + pltpu.roll(ext, back, axis=0)[HALO_ROWS:] * tap
    return _silu(acc)


def _in_proj_kernel(x_ref, nw_ref, w_ref, cw_ref, cb_ref, *rest, n_conv_blocks, n_side):
    side_in, rest = rest[:n_side], rest[n_side:]
    o_ref, side_out = rest[0], rest[1:1 + n_side]
    hn_ref, halo_ref, raw_ref = rest[1 + n_side:]
    i = pl.program_id(0)
    j = pl.program_id(1)
    tm, tn = o_ref.shape

    for src_ref, dst_ref in zip(side_in, side_out):
        dst_ref[...] = src_ref[...].astype(BF16)

    @pl.when(j == 0)
    def _():
        hn_ref[...] = _rms_normalize(x_ref[...], nw_ref[...]).astype(BF16)

    @pl.when((i == 0) & (j == 0))
    def _():
        halo_ref[...] = jnp.zeros_like(halo_ref)

    @pl.when(j < n_conv_blocks)
    def _():
        n_pieces = tn // CONV_CHUNK
        piece = lambda c: slice(c * CONV_CHUNK, (c + 1) * CONV_CHUNK)
        slot = lambda c: jnp.minimum(j, 0)
        raw_ref[slot(0)] = _dot_nt(hn_ref[...], w_ref[piece(0), :])
        for c in range(n_pieces):
            cols = piece(c)
            r = raw_ref[slot(c)]
            o_ref[:, cols] = _conv_silu(r, halo_ref[j, :, cols], cw_ref, cb_ref, cols)
            halo_ref[j, :, cols] = r[tm - HALO_ROWS:]
            if c + 1 < n_pieces:
                raw_ref[slot(c + 1)] = _dot_nt(hn_ref[...], w_ref[piece(c + 1), :])

    @pl.when(j >= n_conv_blocks)
    def _():
        o_ref[...] = _dot_nt(hn_ref[...], w_ref[...])


def _in_proj(x, nw, w_t, conv_w, conv_b, side_weights, *, tm, tn, side_slabs):
    m, k = x.shape
    n = w_t.shape[0]
    assert D_CONV % tn == 0 and tn % CONV_CHUNK == 0
    n_conv_blocks = D_CONV // tn
    n_col_blocks = n // tn
    assert (m // tm) * n_col_blocks >= side_slabs
    conv_block = lambda i, j: (0, jnp.minimum(j, n_conv_blocks - 1))
    side_specs = _side_cast_specs(side_weights, side_slabs, lambda i, j: i * n_col_blocks + j)
    return pl.pallas_call(
        functools.partial(_in_proj_kernel, n_conv_blocks=n_conv_blocks, n_side=len(side_weights)),
        out_shape=(jax.ShapeDtypeStruct((m, n), F32),
                   *[jax.ShapeDtypeStruct(w.shape, BF16) for w in side_weights]),
        grid=(m // tm, n_col_blocks),
        in_specs=[
            pl.BlockSpec((tm, k), lambda i, j: (i, 0)),
            pl.BlockSpec((1, k), lambda i, j: (0, 0)),
            pl.BlockSpec((tn, k), lambda i, j: (j, 0)),
            pl.BlockSpec((CONV_WIDTH, tn), conv_block),
            pl.BlockSpec((1, tn), conv_block),
        ] + side_specs,
        out_specs=(pl.BlockSpec((tm, tn), lambda i, j: (i, j)), *side_specs),
        scratch_shapes=[pltpu.VMEM((tm, k), BF16),
                        pltpu.VMEM((n_conv_blocks, HALO_ROWS, tn), F32),
                        pltpu.VMEM((N_PARK_SLOTS, tm, CONV_CHUNK), F32)],
        compiler_params=pltpu.CompilerParams(
            dimension_semantics=("arbitrary", "arbitrary"),
            vmem_limit_bytes=V7X_VMEM_LIMIT_BYTES),
        name="in_proj",
    )(x, nw, w_t, conv_w, conv_b, *side_weights)


def _swiglu_kernel(hn_ref, wg_ref, wu_ref, *rest):
    n_side = (len(rest) - 1) // 2
    side_in, o_ref, side_out = rest[:n_side], rest[n_side], rest[n_side + 1:]
    for src_ref, dst_ref in zip(side_in, side_out):
        dst_ref[...] = src_ref[...].astype(BF16)
    half = o_ref.shape[1] // 2
    for cols in (slice(0, half), slice(half, 2 * half)):
        gate = _dot(hn_ref[...], wg_ref[:, cols])
        up = _dot(hn_ref[...], wu_ref[:, cols])
        o_ref[:, cols] = (_silu(gate) * up).astype(BF16)


def _swiglu(hn, wg, wu, side_weights, *, tm, tn):
    m, k = hn.shape
    n = wg.shape[1]
    n_col_tiles = n // tn
    side_specs = _side_cast_specs(side_weights, (m // tm) * n_col_tiles,
                                  lambda i, j: i * n_col_tiles + j)
    return pl.pallas_call(
        _swiglu_kernel,
        out_shape=(jax.ShapeDtypeStruct((m, n), BF16),
                   *[jax.ShapeDtypeStruct(w.shape, BF16) for w in side_weights]),
        grid=(m // tm, n_col_tiles),
        in_specs=[
            pl.BlockSpec((tm, k), lambda i, j: (i, 0)),
            pl.BlockSpec((k, tn), lambda i, j: (0, j)),
            pl.BlockSpec((k, tn), lambda i, j: (0, j)),
        ] + side_specs,
        out_specs=(pl.BlockSpec((tm, tn), lambda i, j: (i, j)), *side_specs),
        compiler_params=pltpu.CompilerParams(
            dimension_semantics=("arbitrary", "arbitrary"),
            vmem_limit_bytes=V7X_VMEM_LIMIT_BYTES),
        name="ffn_up",
    )(hn, wg, wu, *side_weights)


def _out_proj_kernel(ys_ref, yp_ref, w_ref, x_ref, nw_ref, h_ref, hn_ref):
    k = ys_ref.shape[1]
    h = x_ref[...] + _dot(ys_ref[...], w_ref[:k, :]) + _dot(yp_ref[...], w_ref[k:, :])
    h_ref[...] = h
    hn_ref[...] = _rms_normalize(h, nw_ref[...]).astype(BF16)


def _resident(shape):
    return pl.BlockSpec(shape, lambda i: (0,) * len(shape), pipeline_mode=pl.Buffered(1))


def _out_proj(ys, yp, w, x, nw, *, tm):
    m, k = ys.shape
    n = w.shape[1]
    row_spec = lambda width: pl.BlockSpec((tm, width), lambda i: (i, 0))
    return pl.pallas_call(
        _out_proj_kernel,
        out_shape=(jax.ShapeDtypeStruct((m, n), F32), jax.ShapeDtypeStruct((m, n), BF16)),
        grid=(m // tm,),
        in_specs=[row_spec(k), row_spec(k), _resident(w.shape), row_spec(n), _resident((1, n))],
        out_specs=(row_spec(n), row_spec(n)),
        compiler_params=pltpu.CompilerParams(
            dimension_semantics=("parallel",),
            vmem_limit_bytes=V7X_VMEM_LIMIT_BYTES),
        name="out_proj",
    )(ys, yp, w, x, nw)


def _down_kernel(a_ref, w_ref, h_ref, nw_ref, o_ref, *, final_norm):
    h = h_ref[...] + _dot(a_ref[...], w_ref[...])
    o_ref[...] = _rms_normalize(h, nw_ref[...]) if final_norm else h


def _down_proj(act, w, h, nw, *, tm, final_norm):
    m, k = act.shape
    n = w.shape[1]
    row_spec = lambda width: pl.BlockSpec((tm, width), lambda i: (i, 0))
    return pl.pallas_call(
        functools.partial(_down_kernel, final_norm=final_norm),
        out_shape=jax.ShapeDtypeStruct((m, n), F32),
        grid=(m // tm,),
        in_specs=[row_spec(k), _resident(w.shape), row_spec(n), _resident((1, n))],
        out_specs=row_spec(n),
        compiler_params=pltpu.CompilerParams(
            dimension_semantics=("parallel",),
            vmem_limit_bytes=V7X_VMEM_LIMIT_BYTES),
        name="ffn_down",
    )(act, w, h, nw)


def _pool_tile(i, u_refs, halo_refs, pw_ref, ps_ref, o_ref):
    tl = o_ref.shape[0]
    t = i * tl + lax.broadcasted_iota(jnp.int32, (tl, POOL_GROUP_DIM), 0)
    for g, window in enumerate(POOL_WINDOWS):
        cols = slice(g * POOL_GROUP_DIM, (g + 1) * POOL_GROUP_DIM)
        u = u_refs[g][...]
        halo = jnp.where(i == 0, 0.0, halo_refs[g][...])
        s = jnp.concatenate([halo, u], axis=0)
        shift = 1
        while shift < window:
            s = s + pltpu.roll(s, shift, axis=0)
            shift *= 2
        count = jnp.minimum(t + 1, window).astype(F32)
        pooled = s[POOL_HALO:] / count - u
        y = _dot(pooled.astype(BF16), pw_ref[g].astype(BF16)) * ps_ref[:, cols]
        o_ref[:, cols] = y.astype(BF16)


def _pool_specs(tl, step_of):
    gd = POOL_GROUP_DIM
    assert COL_U % gd == 0 and tl % POOL_HALO == 0
    halo_blocks_per_tile = tl // POOL_HALO
    n_groups = len(POOL_WINDOWS)
    u_specs = [pl.BlockSpec((tl, gd), lambda *g, k=k: (step_of(*g), COL_U // gd + k))
               for k in range(n_groups)]
    halo_specs = [
        pl.BlockSpec((POOL_HALO, gd),
                     lambda *g, k=k: (jnp.maximum(step_of(*g) * halo_blocks_per_tile - 1, 0),
                                      COL_U // gd + k))
        for k in range(n_groups)]
    return u_specs + halo_specs


assert N_REPLICAS * HEADS_PER_GROUP <= V7X_LANES


def _expansion_matrix():
    lane = jnp.arange(V7X_LANES)
    col = jnp.arange(N_EXPANDED * GROUP_WIDTH)
    replica, head = lane // HEADS_PER_GROUP, lane % HEADS_PER_GROUP
    quantity = replica // 3
    hit = ((quantity[:, None] == col[None, :] // GROUP_WIDTH)
           & (head[:, None] == (col[None, :] % GROUP_WIDTH) // HEAD_DIM)
           & (replica[:, None] < N_REPLICAS))
    return hit.astype(BF16)


def _ssd_group(z, xs, bm, cm, dt_raw, dtb, alog, dskip, nw, state_ref, expand, tril, causal):
    q = xs.shape[0]
    x = dt_raw + dtb
    dt = jnp.maximum(x, 0.0) + jnp.log1p(jnp.exp(-jnp.abs(x)))
    a = -jnp.exp(alog)
    a_cum = _dot3_left(tril, dt * a) * LOG2_E
    a_cum_t = a_cum.T
    a_last = a_cum[q - 1:q, :]

    replica = lax.broadcasted_iota(jnp.int32, (1, V7X_LANES), 1) // HEADS_PER_GROUP
    packed = jnp.where(replica < 3, dt,
                       jnp.where(replica < 6, jnp.exp2(a_cum), jnp.exp2(a_last - a_cum)))
    hi = packed.astype(BF16)
    r1 = packed - hi.astype(F32)
    r2 = r1 - r1.astype(BF16).astype(F32)
    level = replica % 3
    split = jnp.where(level == 0, packed, jnp.where(level == 1, r1, r2)).astype(BF16)
    expanded = _dot(split, expand)
    dt_b = expanded[:, :GROUP_WIDTH]
    decay_in_b = expanded[:, GROUP_WIDTH:2 * GROUP_WIDTH]
    decay_out_b = expanded[:, 2 * GROUP_WIDTH:]

    xdt = xs * dt_b
    bm_bf = bm.astype(BF16)
    cm_bf = cm.astype(BF16)
    cb = _dot_nt(cm_bf, bm_bf)

    xdt_bf = xdt.astype(BF16)
    lane = lax.broadcasted_iota(jnp.int32, (1, V7X_LANES), 1)
    keep = [jnp.where((lane // HEAD_DIM) == side, 1.0, 0.0).astype(BF16) for side in range(2)]
    pairs = []
    for pair in range(HEADS_PER_GROUP // 2):
        xdt_pair = xdt_bf[:, pair * V7X_LANES:(pair + 1) * V7X_LANES]
        acc = jnp.zeros((q, V7X_LANES), F32)
        for side in range(2):
            r = 2 * pair + side
            seg = a_cum[:, r:r + 1] - a_cum_t[r:r + 1, :]
            decay = jnp.exp2(jnp.where(causal, seg, -jnp.inf))
            m_r = (cb * decay).astype(BF16)
            acc = acc + _dot(m_r, xdt_pair * keep[side])
        pairs.append(acc)
    y_diag = jnp.concatenate(pairs, axis=-1)

    state = state_ref[...]
    y_off = _dot(cm_bf, state.astype(BF16)) * decay_in_b
    xw = (xdt * decay_out_b).astype(BF16)
    new_state = lax.dot_general(bm_bf, xw, (((0,), (0,)), ((), ())), preferred_element_type=F32)
    state_ref[...] = state * decay_in_b[q - 1:q, :] + new_state

    y = y_diag + y_off + dskip * xs
    g = y * _silu(z)
    return _rms_normalize(g, nw).astype(BF16)


def _mixers_kernel(*refs, groups, n_side):
    refs = list(refs)
    take = lambda n: [refs.pop(0) for _ in range(n)]
    n_pool = len(POOL_WINDOWS)
    z_refs = take(groups)
    xs_ref, b_ref, c_ref, dt_ref, dtb_ref, alog_ref, dskip_ref, nw_ref, expand_ref = take(9)
    u_refs, halo_refs = take(n_pool), take(n_pool)
    pw_ref, ps_ref = take(2)
    side_in = take(n_side)
    o_ref, yp_ref = take(2)
    side_out = take(n_side)
    state_ref, = refs
    q = SSD_CHUNK

    _pool_tile(pl.program_id(1), u_refs, halo_refs, pw_ref, ps_ref, yp_ref)

    for src_ref, dst_ref in zip(side_in, side_out):
        dst_ref[...] = src_ref[...].astype(BF16)

    @pl.when(pl.program_id(1) == 0)
    def _():
        state_ref[...] = jnp.zeros_like(state_ref)

    row_i = lax.broadcasted_iota(jnp.int32, (q, q), 0)
    col_i = lax.broadcasted_iota(jnp.int32, (q, q), 1)
    causal = row_i >= col_i
    tril = jnp.where(causal, 1.0, 0.0).astype(BF16)
    expand = expand_ref[...]
    for g in range(groups):
        wide = slice(g * GROUP_WIDTH, (g + 1) * GROUP_WIDTH)
        narrow = slice(g * D_STATE, (g + 1) * D_STATE)
        o_ref[:, wide] = _ssd_group(
            z_refs[g][...], xs_ref[:, wide], b_ref[:, narrow], c_ref[:, narrow], dt_ref[:, narrow],
            dtb_ref[:, narrow], alog_ref[:, narrow], dskip_ref[:, wide], nw_ref[:, wide],
            state_ref.at[g], expand, tril, causal)


def _mixers(proj, dtb_c, alog_c, dskip_full, norm_w, pool_w, pool_scale, side_weights, *, groups):
    m = proj.shape[0]
    q = SSD_CHUNK
    gw = groups * GROUP_WIDTH
    ds = groups * D_STATE
    assert V7X_LANES == D_STATE and N_GROUPS % groups == 0

    def col(start, width):
        assert start % width == 0
        return start // width

    row_spec = lambda width, block0: pl.BlockSpec((q, width), lambda g, c: (c, block0 + g))
    par_spec = lambda width: pl.BlockSpec((1, width), lambda g, c: (0, g))
    z_specs = [
        pl.BlockSpec((q, GROUP_WIDTH),
                     lambda g, c, k=k: (c, col(COL_Z, GROUP_WIDTH) + g * groups + k))
        for k in range(groups)]
    expand = _expansion_matrix()
    assert groups == N_GROUPS
    side_specs = _side_cast_specs(side_weights, m // q, lambda g, c: c)
    return pl.pallas_call(
        functools.partial(_mixers_kernel, groups=groups, n_side=len(side_weights)),
        out_shape=(jax.ShapeDtypeStruct((m, D_SSM), BF16), jax.ShapeDtypeStruct((m, D_POOL), BF16),
                   *[jax.ShapeDtypeStruct(w.shape, BF16) for w in side_weights]),
        grid=(N_GROUPS // groups, m // q),
        in_specs=z_specs + [
            row_spec(gw, col(COL_XS, gw)),
            row_spec(ds, col(COL_B, ds)),
            row_spec(ds, col(COL_C, ds)),
            row_spec(ds, col(COL_DT, ds)),
            par_spec(ds),
            par_spec(ds),
            par_spec(gw),
            par_spec(gw),
            pl.BlockSpec(expand.shape, lambda g, c: (0, 0)),
        ] + _pool_specs(q, lambda g, c: c) + [
            pl.BlockSpec(pool_w.shape, lambda g, c: (0, 0, 0)),
            pl.BlockSpec(pool_scale.shape, lambda g, c: (0, 0)),
        ] + side_specs,
        out_specs=(pl.BlockSpec((q, gw), lambda g, c: (c, g)),
                   pl.BlockSpec((q, D_POOL), lambda g, c: (c, 0)), *side_specs),
        scratch_shapes=[pltpu.VMEM((groups, D_STATE, GROUP_WIDTH), F32)],
        compiler_params=pltpu.CompilerParams(
            dimension_semantics=("parallel", "arbitrary"),
            vmem_limit_bytes=V7X_VMEM_LIMIT_BYTES),
        name="mixers",
    )(*([proj] * (groups + 4)), dtb_c, alog_c, dskip_full, norm_w, expand,
      *([proj] * (2 * len(POOL_WINDOWS))), pool_w, pool_scale, *side_weights)


def _pad_heads_to_lanes(v):
    v = jnp.tile(v.reshape(N_GROUPS, HEADS_PER_GROUP), (1, N_REPLICAS))
    v = jnp.pad(v, ((0, 0), (0, V7X_LANES - N_REPLICAS * HEADS_PER_GROUP)))
    return v.reshape(1, N_GROUPS * V7X_LANES)


def kernel(x, attn_norm_w, w_in, conv_w, conv_b, dt_bias, a_log, d_skip, ssd_norm_w, pool_w,
           pool_scale, w_out, ffn_norm_w, w_gate, w_up, w_down, final_norm_w):
    bsz, seqlen, d_model = x.shape
    depth = w_in.shape[0]
    h = x.reshape(bsz * seqlen, d_model)
    assert bsz == 1 and seqlen % 2048 == 0 and d_model == D_MODEL

    for i in range(depth):
        w_cat = _w_in_prep(w_in[i].T, tk=256)
        proj, w_out_bf = _in_proj(
            h, attn_norm_w[i][None, :], w_cat, conv_w[i], conv_b[i][None, :],
            (w_out[i],), tm=1024, tn=1536, side_slabs=32)

        y_ssd, y_pool, w_gate_bf, w_up_bf = _mixers(
            proj, _pad_heads_to_lanes(dt_bias[i]), _pad_heads_to_lanes(a_log[i]),
            jnp.repeat(d_skip[i], HEAD_DIM)[None, :], ssd_norm_w[i][None, :],
            pool_w[i], pool_scale[i][None, :], (w_gate[i], w_up[i]), groups=N_GROUPS)

        h, hn = _out_proj(y_ssd, y_pool, w_out_bf, h, ffn_norm_w[i][None, :], tm=256)

        act, w_down_bf = _swiglu(hn, w_gate_bf, w_up_bf, (w_down[i],), tm=2048, tn=512)
        last = i == depth - 1
        h = _down_proj(act, w_down_bf, h, final_norm_w[None, :], tm=512, final_norm=last)

    return h.reshape(bsz, seqlen, d_model)
```

```python
import functools

import jax
import jax.numpy as jnp
from jax import lax
from jax.experimental import pallas as pl
from jax.experimental.pallas import tpu as pltpu

F32 = jnp.float32
BF16 = jnp.bfloat16

NORM_EPS = 1e-5
LOG2_E = 1.4426950408889634
D_MODEL = 2048
D_SSM = 2048
HEAD_DIM = 64
N_HEADS = D_SSM // HEAD_DIM
N_GROUPS = 4
HEADS_PER_GROUP = N_HEADS // N_GROUPS
GROUP_WIDTH = D_SSM // N_GROUPS
D_STATE = 128
CONV_WIDTH = 4
D_POOL = 2048
POOL_WINDOWS = (2, 4, 8, 16)
POOL_GROUP_DIM = D_POOL // len(POOL_WINDOWS)
D_MIX = D_SSM + D_POOL

V7X_LANES = 128
V7X_SUBLANES = 8
V7X_VMEM_LIMIT_BYTES = 60 * 1024 * 1024

COL_XS = 0
COL_B = COL_XS + D_SSM
COL_C = COL_B + N_GROUPS * D_STATE
D_CONV = COL_C + N_GROUPS * D_STATE
COL_Z = D_CONV
COL_U = COL_Z + D_SSM
COL_DT = COL_U + D_POOL
D_PROJ = COL_DT + N_GROUPS * V7X_LANES

SSD_CHUNK = 256
SSD_SUBCHUNK = 128
N_EXPANDED = 3
N_REPLICAS = 3 * N_EXPANDED
HALO_ROWS = V7X_SUBLANES
POOL_HALO = 16


def _sigmoid(x):
    return 1.0 / (1.0 + jnp.exp(-x))


def _silu(x):
    return x * _sigmoid(x)


def _rms_normalize(x, w):
    ms = jnp.mean(x * x, axis=-1, keepdims=True)
    return x * lax.rsqrt(ms + NORM_EPS) * w


def _split3(x):
    hi = x.astype(BF16)
    r1 = x - hi.astype(F32)
    mid = r1.astype(BF16)
    lo = (r1 - mid.astype(F32)).astype(BF16)
    return hi, mid, lo


def _dot(a, b):
    return jnp.dot(a, b, preferred_element_type=F32)


def _dot_nt(a, b):
    return lax.dot_general(a, b, (((1,), (1,)), ((), ())), preferred_element_type=F32)


def _dot3_left(lhs_bf16, x):
    hi, mid, lo = _split3(x)
    return _dot(lhs_bf16, hi) + _dot(lhs_bf16, mid) + _dot(lhs_bf16, lo)


SRC_XBC = D_SSM
SRC_DT = SRC_XBC + D_SSM + 2 * N_GROUPS * D_STATE
SRC_U = SRC_DT + N_HEADS
D_IN_PROJ = SRC_U + D_POOL


def _w_in_prep_kernel(w_ref, o_ref):
    o_ref[COL_Z:COL_Z + D_SSM, :] = w_ref[:D_SSM, :].astype(BF16)
    o_ref[COL_U:COL_U + D_POOL, :] = w_ref[SRC_U:SRC_U + D_POOL, :].astype(BF16)
    o_ref[COL_XS:D_CONV, :] = w_ref[SRC_XBC:SRC_DT, :].astype(BF16)
    zeros = jnp.zeros((V7X_LANES - N_REPLICAS * HEADS_PER_GROUP, o_ref.shape[1]), F32)
    for g in range(N_GROUPS):
        row0 = COL_DT + g * V7X_LANES
        src0 = SRC_DT + g * HEADS_PER_GROUP
        heads = w_ref[src0:src0 + HEADS_PER_GROUP, :]
        block = jnp.concatenate([heads] * N_REPLICAS + [zeros], axis=0)
        o_ref[row0:row0 + V7X_LANES, :] = block.astype(BF16)


def _w_in_prep(w_t, *, tk):
    n, k = w_t.shape
    assert n == D_IN_PROJ
    return pl.pallas_call(
        _w_in_prep_kernel,
        out_shape=jax.ShapeDtypeStruct((D_PROJ, k), BF16),
        grid=(k // tk,),
        in_specs=[pl.BlockSpec((n, tk), lambda i: (0, i))],
        out_specs=pl.BlockSpec((D_PROJ, tk), lambda i: (0, i)),
        compiler_params=pltpu.CompilerParams(
            dimension_semantics=("parallel",),
            vmem_limit_bytes=V7X_VMEM_LIMIT_BYTES),
        name="w_in_prep",
    )(w_t)


def _side_cast_specs(weights, n_slabs, step_of):
    specs = []
    for w in weights:
        rows = w.shape[0] // n_slabs
        assert rows * n_slabs == w.shape[0] and rows % (2 * V7X_SUBLANES) == 0
        specs.append(pl.BlockSpec(
            (rows, w.shape[1]), lambda *g: (jnp.minimum(step_of(*g), n_slabs - 1), 0)))
    return specs


CONV_CHUNK = 256
N_PARK_SLOTS = 1


def _conv_silu(r, halo, w_ref, b_ref, cols):
    ext = jnp.concatenate([halo, r], axis=0)
    acc = r * w_ref[CONV_WIDTH - 1:CONV_WIDTH, cols] + b_ref[:, cols]
    for back in range(1, CONV_WIDTH):
        tap = w_ref[CONV_WIDTH - 1 - back:CONV_WIDTH - back, cols]
        acc = acc + pltpu.roll(ext, back, axis=0)[HALO_ROWS:] * tap
    return _silu(acc)


def _in_proj_kernel(x_ref, nw_ref, w_ref, cw_ref, cb_ref, *rest, n_conv_blocks, n_side):
    side_in, rest = rest[:n_side], rest[n_side:]
    o_ref, side_out = rest[0], rest[1:1 + n_side]
    hn_ref, halo_ref, raw_ref = rest[1 + n_side:]
    i = pl.program_id(0)
    j = pl.program_id(1)
    tm, tn = o_ref.shape

    for src_ref, dst_ref in zip(side_in, side_out):
        dst_ref[...] = src_ref[...].astype(BF16)

    @pl.when(j == 0)
    def _():
        hn_ref[...] = _rms_normalize(x_ref[...], nw_ref[...]).astype(BF16)

    @pl.when((i == 0) & (j == 0))
    def _():
        halo_ref[...] = jnp.zeros_like(halo_ref)

    @pl.when(j < n_conv_blocks)
    def _():
        n_pieces = tn // CONV_CHUNK
        piece = lambda c: slice(c * CONV_CHUNK, (c + 1) * CONV_CHUNK)
        slot = lambda c: jnp.minimum(j, 0)
        raw_ref[slot(0)] = _dot_nt(hn_ref[...], w_ref[piece(0), :])
        for c in range(n_pieces):
            cols = piece(c)
            r = raw_ref[slot(c)]
            o_ref[:, cols] = _conv_silu(r, halo_ref[j, :, cols], cw_ref, cb_ref, cols)
            halo_ref[j, :, cols] = r[tm - HALO_ROWS:]
            if c + 1 < n_pieces:
                raw_ref[slot(c + 1)] = _dot_nt(hn_ref[...], w_ref[piece(c + 1), :])

    @pl.when(j >= n_conv_blocks)
    def _():
        o_ref[...] = _dot_nt(hn_ref[...], w_ref[...])


def _in_proj(x, nw, w_t, conv_w, conv_b, side_weights, *, tm, tn, side_slabs):
    m, k = x.shape
    n = w_t.shape[0]
    assert D_CONV % tn == 0 and tn % CONV_CHUNK == 0
    n_conv_blocks = D_CONV // tn
    n_col_blocks = n // tn
    assert (m // tm) * n_col_blocks >= side_slabs
    conv_block = lambda i, j: (0, jnp.minimum(j, n_conv_blocks - 1))
    side_specs = _side_cast_specs(side_weights, side_slabs, lambda i, j: i * n_col_blocks + j)
    return pl.pallas_call(
        functools.partial(_in_proj_kernel, n_conv_blocks=n_conv_blocks, n_side=len(side_weights)),
        out_shape=(jax.ShapeDtypeStruct((m, n), F32),
                   *[jax.ShapeDtypeStruct(w.shape, BF16) for w in side_weights]),
        grid=(m // tm, n_col_blocks),
        in_specs=[
            pl.BlockSpec((tm, k), lambda i, j: (i, 0)),
            pl.BlockSpec((1, k), lambda i, j: (0, 0)),
            pl.BlockSpec((tn, k), lambda i, j: (j, 0)),
            pl.BlockSpec((CONV_WIDTH, tn), conv_block),
            pl.BlockSpec((1, tn), conv_block),
        ] + side_specs,
        out_specs=(pl.BlockSpec((tm, tn), lambda i, j: (i, j)), *side_specs),
        scratch_shapes=[pltpu.VMEM((tm, k), BF16),
                        pltpu.VMEM((n_conv_blocks, HALO_ROWS, tn), F32),
                        pltpu.VMEM((N_PARK_SLOTS, tm, CONV_CHUNK), F32)],
        compiler_params=pltpu.CompilerParams(
            dimension_semantics=("arbitrary", "arbitrary"),
            vmem_limit_bytes=V7X_VMEM_LIMIT_BYTES),
        name="in_proj",
    )(x, nw, w_t, conv_w, conv_b, *side_weights)


def _swiglu_kernel(hn_ref, wg_ref, wu_ref, *rest):
    n_side = (len(rest) - 1) // 2
    side_in, o_ref, side_out = rest[:n_side], rest[n_side], rest[n_side + 1:]
    for src_ref, dst_ref in zip(side_in, side_out):
        dst_ref[...] = src_ref[...].astype(BF16)
    half = o_ref.shape[1] // 2
    for cols in (slice(0, half), slice(half, 2 * half)):
        gate = _dot(hn_ref[...], wg_ref[:, cols])
        up = _dot(hn_ref[...], wu_ref[:, cols])
        o_ref[:, cols] = (_silu(gate) * up).astype(BF16)


def _swiglu(hn, wg, wu, side_weights, *, tm, tn):
    m, k = hn.shape
    n = wg.shape[1]
    n_col_tiles = n // tn
    side_specs = _side_cast_specs(side_weights, (m // tm) * n_col_tiles,
                                  lambda i, j: i * n_col_tiles + j)
    return pl.pallas_call(
        _swiglu_kernel,
        out_shape=(jax.ShapeDtypeStruct((m, n), BF16),
                   *[jax.ShapeDtypeStruct(w.shape, BF16) for w in side_weights]),
        grid=(m // tm, n_col_tiles),
        in_specs=[
            pl.BlockSpec((tm, k), lambda i, j: (i, 0)),
            pl.BlockSpec((k, tn), lambda i, j: (0, j)),
            pl.BlockSpec((k, tn), lambda i, j: (0, j)),
        ] + side_specs,
        out_specs=(pl.BlockSpec((tm, tn), lambda i, j: (i, j)), *side_specs),
        compiler_params=pltpu.CompilerParams(
            dimension_semantics=("arbitrary", "arbitrary"),
            vmem_limit_bytes=V7X_VMEM_LIMIT_BYTES),
        name="ffn_up",
    )(hn, wg, wu, *side_weights)


def _out_proj_kernel(ys_ref, yp_ref, w_ref, x_ref, nw_ref, h_ref, hn_ref):
    k = ys_ref.shape[1]
    h = x_ref[...] + _dot(ys_ref[...], w_ref[:k, :]) + _dot(yp_ref[...], w_ref[k:, :])
    h_ref[...] = h
    hn_ref[...] = _rms_normalize(h, nw_ref[...]).astype(BF16)


def _resident(shape):
    return pl.BlockSpec(shape, lambda i: (0,) * len(shape), pipeline_mode=pl.Buffered(1))


def _out_proj(ys, yp, w, x, nw, *, tm):
    m, k = ys.shape
    n = w.shape[1]
    row_spec = lambda width: pl.BlockSpec((tm, width), lambda i: (i, 0))
    return pl.pallas_call(
        _out_proj_kernel,
        out_shape=(jax.ShapeDtypeStruct((m, n), F32), jax.ShapeDtypeStruct((m, n), BF16)),
        grid=(m // tm,),
        in_specs=[row_spec(k), row_spec(k), _resident(w.shape), row_spec(n), _resident((1, n))],
        out_specs=(row_spec(n), row_spec(n)),
        compiler_params=pltpu.CompilerParams(
            dimension_semantics=("parallel",),
            vmem_limit_bytes=V7X_VMEM_LIMIT_BYTES),
        name="out_proj",
    )(ys, yp, w, x, nw)


def _down_kernel(a_ref, w_ref, h_ref, nw_ref, o_ref, *, final_norm):
    h = h_ref[...] + _dot(a_ref[...], w_ref[...])
    o_ref[...] = _rms_normalize(h, nw_ref[...]) if final_norm else h


def _down_proj(act, w, h, nw, *, tm, final_norm):
    m, k = act.shape
    n = w.shape[1]
    row_spec = lambda width: pl.BlockSpec((tm, width), lambda i: (i, 0))
    return pl.pallas_call(
        functools.partial(_down_kernel, final_norm=final_norm),
        out_shape=jax.ShapeDtypeStruct((m, n), F32),
        grid=(m // tm,),
        in_specs=[row_spec(k), _resident(w.shape), row_spec(n), _resident((1, n))],
        out_specs=row_spec(n),
        compiler_params=pltpu.CompilerParams(
            dimension_semantics=("parallel",),
            vmem_limit_bytes=V7X_VMEM_LIMIT_BYTES),
        name="ffn_down",
    )(act, w, h, nw)


def _pool_tile(i, u_refs, halo_refs, pw_ref, ps_ref, o_ref):
    tl = o_ref.shape[0]
    t = i * tl + lax.broadcasted_iota(jnp.int32, (tl, POOL_GROUP_DIM), 0)
    for g, window in enumerate(POOL_WINDOWS):
        cols = slice(g * POOL_GROUP_DIM, (g + 1) * POOL_GROUP_DIM)
        u = u_refs[g][...]
        halo = jnp.where(i == 0, 0.0, halo_refs[g][...])
        s = jnp.concatenate([halo, u], axis=0)
        shift = 1
        while shift < window:
            s = s + pltpu.roll(s, shift, axis=0)
            shift *= 2
        count = jnp.minimum(t + 1, window).astype(F32)
        pooled = s[POOL_HALO:] / count - u
        y = _dot(pooled.astype(BF16), pw_ref[g].astype(BF16)) * ps_ref[:, cols]
        o_ref[:, cols] = y.astype(BF16)


def _pool_specs(tl, step_of):
    gd = POOL_GROUP_DIM
    assert COL_U % gd == 0 and tl % POOL_HALO == 0
    halo_blocks_per_tile = tl // POOL_HALO
    n_groups = len(POOL_WINDOWS)
    u_specs = [pl.BlockSpec((tl, gd), lambda *g, k=k: (step_of(*g), COL_U // gd + k))
               for k in range(n_groups)]
    halo_specs = [
        pl.BlockSpec((POOL_HALO, gd),
                     lambda *g, k=k: (jnp.maximum(step_of(*g) * halo_blocks_per_tile - 1, 0),
                                      COL_U // gd + k))
        for k in range(n_groups)]
    return u_specs + halo_specs


assert N_REPLICAS * HEADS_PER_GROUP <= V7X_LANES


def _expansion_matrix():
    lane = jnp.arange(V7X_LANES)
    col = jnp.arange(N_EXPANDED * GROUP_WIDTH)
    replica, head = lane // HEADS_PER_GROUP, lane % HEADS_PER_GROUP
    quantity = replica // 3
    hit = ((quantity[:, None] == col[None, :] // GROUP_WIDTH)
           & (head[:, None] == (col[None, :] % GROUP_WIDTH) // HEAD_DIM)
           & (replica[:, None] < N_REPLICAS))
    return hit.astype(BF16)


def _ssd_group(z, xs, bm, cm, dt_raw, dtb, alog, dskip, nw, state_ref, expand, tril, causal):
    q = xs.shape[0]
    x = dt_raw + dtb
    dt = jnp.maximum(x, 0.0) + jnp.log1p(jnp.exp(-jnp.abs(x)))
    a = -jnp.exp(alog)
    a_cum = _dot3_left(tril, dt * a) * LOG2_E
    a_cum_t = a_cum.T
    a_last = a_cum[q - 1:q, :]

    replica = lax.broadcasted_iota(jnp.int32, (1, V7X_LANES), 1) // HEADS_PER_GROUP
    packed = jnp.where(replica < 3, dt,
                       jnp.where(replica < 6, jnp.exp2(a_cum), jnp.exp2(a_last - a_cum)))
    hi = packed.astype(BF16)
    r1 = packed - hi.astype(F32)
    r2 = r1 - r1.astype(BF16).astype(F32)
    level = replica % 3
    split = jnp.where(level == 0, packed, jnp.where(level == 1, r1, r2)).astype(BF16)
    expanded = _dot(split, expand)
    dt_b = expanded[:, :GROUP_WIDTH]
    decay_in_b = expanded[:, GROUP_WIDTH:2 * GROUP_WIDTH]
    decay_out_b = expanded[:, 2 * GROUP_WIDTH:]

    xdt = xs * dt_b
    bm_bf = bm.astype(BF16)
    cm_bf = cm.astype(BF16)
    cb = _dot_nt(cm_bf, bm_bf)

    xdt_bf = xdt.astype(BF16)
    lane = lax.broadcasted_iota(jnp.int32, (1, V7X_LANES), 1)
    keep = [jnp.where((lane // HEAD_DIM) == side, 1.0, 0.0).astype(BF16) for side in range(2)]
    pairs = []
    for pair in range(HEADS_PER_GROUP // 2):
        xdt_pair = xdt_bf[:, pair * V7X_LANES:(pair + 1) * V7X_LANES]
        acc = jnp.zeros((q, V7X_LANES), F32)
        for side in range(2):
            r = 2 * pair + side
            seg = a_cum[:, r:r + 1] - a_cum_t[r:r + 1, :]
            decay = jnp.exp2(jnp.where(causal, seg, -jnp.inf))
            m_r = (cb * decay).astype(BF16)
            acc = acc + _dot(m_r, xdt_pair * keep[side])
        pairs.append(acc)
    y_diag = jnp.concatenate(pairs, axis=-1)

    state = state_ref[...]
    y_off = _dot(cm_bf, state.astype(BF16)) * decay_in_b
    xw = (xdt * decay_out_b).astype(BF16)
    new_state = lax.dot_general(bm_bf, xw, (((0,), (0,)), ((), ())), preferred_element_type=F32)
    state_ref[...] = state * decay_in_b[q - 1:q, :] + new_state

    y = y_diag + y_off + dskip * xs
    g = y * _silu(z)
    return _rms_normalize(g, nw).astype(BF16)


def _mixers_kernel(*refs, groups, n_side):
    refs = list(refs)
    take = lambda n: [refs.pop(0) for _ in range(n)]
    n_pool = len(POOL_WINDOWS)
    z_refs = take(groups)
    xs_ref, b_ref, c_ref, dt_ref, dtb_ref, alog_ref, dskip_ref, nw_ref, expand_ref = take(9)
    u_refs, halo_refs = take(n_pool), take(n_pool)
    pw_ref, ps_ref = take(2)
    side_in = take(n_side)
    o_ref, yp_ref = take(2)
    side_out = take(n_side)
    state_ref, = refs
    q = SSD_CHUNK

    _pool_tile(pl.program_id(1), u_refs, halo_refs, pw_ref, ps_ref, yp_ref)

    for src_ref, dst_ref in zip(side_in, side_out):
        dst_ref[...] = src_ref[...].astype(BF16)

    @pl.when(pl.program_id(1) == 0)
    def _():
        state_ref[...] = jnp.zeros_like(state_ref)

    sq = SSD_SUBCHUNK
    row_i = lax.broadcasted_iota(jnp.int32, (sq, sq), 0)
    col_i = lax.broadcasted_iota(jnp.int32, (sq, sq), 1)
    causal = row_i >= col_i
    tril = jnp.where(causal, 1.0, 0.0).astype(BF16)
    expand = expand_ref[...]
    for g in range(groups):
        wide = slice(g * GROUP_WIDTH, (g + 1) * GROUP_WIDTH)
        narrow = slice(g * D_STATE, (g + 1) * D_STATE)
        for sub in range(q // sq):
            rows = slice(sub * sq, (sub + 1) * sq)
            o_ref[rows, wide] = _ssd_group(
                z_refs[g][rows, :], xs_ref[rows, wide], b_ref[rows, narrow], c_ref[rows, narrow],
                dt_ref[rows, narrow], dtb_ref[:, narrow], alog_ref[:, narrow], dskip_ref[:, wide],
                nw_ref[:, wide], state_ref.at[g], expand, tril, causal)


def _mixers(proj, dtb_c, alog_c, dskip_full, norm_w, pool_w, pool_scale, side_weights, *, groups):
    m = proj.shape[0]
    q = SSD_CHUNK
    gw = groups * GROUP_WIDTH
    ds = groups * D_STATE
    assert V7X_LANES == D_STATE and N_GROUPS % groups == 0

    def col(start, width):
        assert start % width == 0
        return start // width

    row_spec = lambda width, block0: pl.BlockSpec((q, width), lambda g, c: (c, block0 + g))
    par_spec = lambda width: pl.BlockSpec((1, width), lambda g, c: (0, g))
    z_specs = [
        pl.BlockSpec((q, GROUP_WIDTH),
                     lambda g, c, k=k: (c, col(COL_Z, GROUP_WIDTH) + g * groups + k))
        for k in range(groups)]
    expand = _expansion_matrix()
    assert groups == N_GROUPS
    side_specs = _side_cast_specs(side_weights, m // q, lambda g, c: c)
    return pl.pallas_call(
        functools.partial(_mixers_kernel, groups=groups, n_side=len(side_weights)),
        out_shape=(jax.ShapeDtypeStruct((m, D_SSM), BF16), jax.ShapeDtypeStruct((m, D_POOL), BF16),
                   *[jax.ShapeDtypeStruct(w.shape, BF16) for w in side_weights]),
        grid=(N_GROUPS // groups, m // q),
        in_specs=z_specs + [
            row_spec(gw, col(COL_XS, gw)),
            row_spec(ds, col(COL_B, ds)),
            row_spec(ds, col(COL_C, ds)),
            row_spec(ds, col(COL_DT, ds)),
            par_spec(ds),
            par_spec(ds),
            par_spec(gw),
            par_spec(gw),
            pl.BlockSpec(expand.shape, lambda g, c: (0, 0)),
        ] + _pool_specs(q, lambda g, c: c) + [
            pl.BlockSpec(pool_w.shape, lambda g, c: (0, 0, 0)),
            pl.BlockSpec(pool_scale.shape, lambda g, c: (0, 0)),
        ] + side_specs,
        out_specs=(pl.BlockSpec((q, gw), lambda g, c: (c, g)),
                   pl.BlockSpec((q, D_POOL), lambda g, c: (c, 0)), *side_specs),
        scratch_shapes=[pltpu.VMEM((groups, D_STATE, GROUP_WIDTH), F32)],
        compiler_params=pltpu.CompilerParams(
            dimension_semantics=("parallel", "arbitrary"),
            vmem_limit_bytes=V7X_VMEM_LIMIT_BYTES),
        name="mixers",
    )(*([proj] * (groups + 4)), dtb_c, alog_c, dskip_full, norm_w, expand,
      *([proj] * (2 * len(POOL_WINDOWS))), pool_w, pool_scale, *side_weights)


def _pad_heads_to_lanes(v):
    v = jnp.tile(v.reshape(N_GROUPS, HEADS_PER_GROUP), (1, N_REPLICAS))
    v = jnp.pad(v, ((0, 0), (0, V7X_LANES - N_REPLICAS * HEADS_PER_GROUP)))
    return v.reshape(1, N_GROUPS * V7X_LANES)


def kernel(x, attn_norm_w, w_in, conv_w, conv_b, dt_bias, a_log, d_skip, ssd_norm_w, pool_w,
           pool_scale, w_out, ffn_norm_w, w_gate, w_up, w_down, final_norm_w):
    bsz, seqlen, d_model = x.shape
    depth = w_in.shape[0]
    h = x.reshape(bsz * seqlen, d_model)
    assert bsz == 1 and seqlen % 2048 == 0 and d_model == D_MODEL

    for i in range(depth):
        w_cat = _w_in_prep(w_in[i].T, tk=256)
        proj, w_out_bf = _in_proj(
            h, attn_norm_w[i][None, :], w_cat, conv_w[i], conv_b[i][None, :],
            (w_out[i],), tm=1024, tn=1536, side_slabs=32)

        y_ssd, y_pool, w_gate_bf, w_up_bf = _mixers(
            proj, _pad_heads_to_lanes(dt_bias[i]), _pad_heads_to_lanes(a_log[i]),
            jnp.repeat(d_skip[i], HEAD_DIM)[None, :], ssd_norm_w[i][None, :],
            pool_w[i], pool_scale[i][None, :], (w_gate[i], w_up[i]), groups=N_GROUPS)

        h, hn = _out_proj(y_ssd, y_pool, w_out_bf, h, ffn_norm_w[i][None, :], tm=512)

        act, w_down_bf = _swiglu(hn, w_gate_bf, w_up_bf, (w_down[i],), tm=2048, tn=512)
        last = i == depth - 1
        h = _down_proj(act, w_down_bf, h, final_norm_w[None, :], tm=512, final_norm=last)

    return h.reshape(bsz, seqlen, d_model)
```

```python
import functools

import jax
import jax.numpy as jnp
from jax import lax
from jax.experimental import pallas as pl
from jax.experimental.pallas import tpu as pltpu

F32 = jnp.float32
BF16 = jnp.bfloat16

NORM_EPS = 1e-5
LOG2_E = 1.4426950408889634
D_MODEL = 2048
D_SSM = 2048
HEAD_DIM = 64
N_HEADS = D_SSM // HEAD_DIM
N_GROUPS = 4
HEADS_PER_GROUP = N_HEADS // N_GROUPS
GROUP_WIDTH = D_SSM // N_GROUPS
D_STATE = 128
CONV_WIDTH = 4
D_POOL = 2048
POOL_WINDOWS = (2, 4, 8, 16)
POOL_GROUP_DIM = D_POOL // len(POOL_WINDOWS)
D_MIX = D_SSM + D_POOL

V7X_LANES = 128
V7X_SUBLANES = 8
V7X_VMEM_LIMIT_BYTES = 60 * 1024 * 1024

COL_XS = 0
COL_B = COL_XS + D_SSM
COL_C = COL_B + N_GROUPS * D_STATE
D_CONV = COL_C + N_GROUPS * D_STATE
COL_Z = D_CONV
COL_U = COL_Z + D_SSM
COL_DT = COL_U + D_POOL
D_PROJ = COL_DT + N_GROUPS * V7X_LANES

SSD_CHUNK = 256
N_EXPANDED = 3
N_REPLICAS = 3 * N_EXPANDED
HALO_ROWS = V7X_SUBLANES
POOL_HALO = 16


def _sigmoid(x):
    return 1.0 / (1.0 + jnp.exp(-x))


def _silu(x):
    return x * _sigmoid(x)


def _rms_normalize(x, w):
    ms = jnp.mean(x * x, axis=-1, keepdims=True)
    return x * lax.rsqrt(ms + NORM_EPS) * w


def _split3(x):
    hi = x.astype(BF16)
    r1 = x - hi.astype(F32)
    mid = r1.astype(BF16)
    lo = (r1 - mid.astype(F32)).astype(BF16)
    return hi, mid, lo


def _dot(a, b):
    return jnp.dot(a, b, preferred_element_type=F32)


def _dot_nt(a, b):
    return lax.dot_general(a, b, (((1,), (1,)), ((), ())), preferred_element_type=F32)


def _dot3_left(lhs_bf16, x):
    hi, mid, lo = _split3(x)
    return _dot(lhs_bf16, hi) + _dot(lhs_bf16, mid) + _dot(lhs_bf16, lo)


SRC_XBC = D_SSM
SRC_DT = SRC_XBC + D_SSM + 2 * N_GROUPS * D_STATE
SRC_U = SRC_DT + N_HEADS
D_IN_PROJ = SRC_U + D_POOL


def _w_in_prep_kernel(w_ref, o_ref):
    o_ref[COL_Z:COL_Z + D_SSM, :] = w_ref[:D_SSM, :].astype(BF16)
    o_ref[COL_U:COL_U + D_POOL, :] = w_ref[SRC_U:SRC_U + D_POOL, :].astype(BF16)
    o_ref[COL_XS:D_CONV, :] = w_ref[SRC_XBC:SRC_DT, :].astype(BF16)
    zeros = jnp.zeros((V7X_LANES - N_REPLICAS * HEADS_PER_GROUP, o_ref.shape[1]), F32)
    for g in range(N_GROUPS):
        row0 = COL_DT + g * V7X_LANES
        src0 = SRC_DT + g * HEADS_PER_GROUP
        heads = w_ref[src0:src0 + HEADS_PER_GROUP, :]
        block = jnp.concatenate([heads] * N_REPLICAS + [zeros], axis=0)
        o_ref[row0:row0 + V7X_LANES, :] = block.astype(BF16)


def _w_in_prep(w_t, *, tk):
    n, k = w_t.shape
    assert n == D_IN_PROJ
    return pl.pallas_call(
        _w_in_prep_kernel,
        out_shape=jax.ShapeDtypeStruct((D_PROJ, k), BF16),
        grid=(k // tk,),
        in_specs=[pl.BlockSpec((n, tk), lambda i: (0, i))],
        out_specs=pl.BlockSpec((D_PROJ, tk), lambda i: (0, i)),
        compiler_params=pltpu.CompilerParams(
            dimension_semantics=("parallel",),
            vmem_limit_bytes=V7X_VMEM_LIMIT_BYTES),
        name="w_in_prep",
    )(w_t)


def _side_cast_specs(weights, n_slabs, step_of):
    specs = []
    for w in weights:
        rows = w.shape[0] // n_slabs
        assert rows * n_slabs == w.shape[0] and rows % (2 * V7X_SUBLANES) == 0
        specs.append(pl.BlockSpec(
            (rows, w.shape[1]), lambda *g: (jnp.minimum(step_of(*g), n_slabs - 1), 0)))
    return specs


CONV_CHUNK = 256
N_PARK_SLOTS = 1


def _conv_silu(r, halo, w_ref, b_ref, cols):
    ext = jnp.concatenate([halo, r], axis=0)
    acc = r * w_ref[CONV_WIDTH - 1:CONV_WIDTH, cols] + b_ref[:, cols]
    for back in range(1, CONV_WIDTH):
        tap = w_ref[CONV_WIDTH - 1 - back:CONV_WIDTH - back, cols]
        acc = acc + pltpu.roll(ext, back, axis=0)[HALO_ROWS:] * tap
    return _silu(acc)


def _in_proj_kernel(x_ref, nw_ref, w_ref, cw_ref, cb_ref, *rest, n_conv_blocks, n_side):
    side_in, rest = rest[:n_side], rest[n_side:]
    o_ref, side_out = rest[0], rest[1:1 + n_side]
    hn_ref, halo_ref, raw_ref = rest[1 + n_side:]
    i = pl.program_id(0)
    j = pl.program_id(1)
    tm, tn = o_ref.shape

    for src_ref, dst_ref in zip(side_in, side_out):
        dst_ref[...] = src_ref[...].astype(BF16)

    @pl.when(j == 0)
    def _():
        hn_ref[...] = _rms_normalize(x_ref[...], nw_ref[...]).astype(BF16)

    @pl.when((i == 0) & (j == 0))
    def _():
        halo_ref[...] = jnp.zeros_like(halo_ref)

    @pl.when(j < n_conv_blocks)
    def _():
        n_pieces = tn // CONV_CHUNK
        piece = lambda c: slice(c * CONV_CHUNK, (c + 1) * CONV_CHUNK)
        slot = lambda c: jnp.minimum(j, 0)
        raw_ref[slot(0)] = _dot_nt(hn_ref[...], w_ref[piece(0), :])
        for c in range(n_pieces):
            cols = piece(c)
            r = raw_ref[slot(c)]
            o_ref[:, cols] = _conv_silu(r, halo_ref[j, :, cols], cw_ref, cb_ref, cols)
            halo_ref[j, :, cols] = r[tm - HALO_ROWS:]
            if c + 1 < n_pieces:
                raw_ref[slot(c + 1)] = _dot_nt(hn_ref[...], w_ref[piece(c + 1), :])

    @pl.when(j >= n_conv_blocks)
    def _():
        o_ref[...] = _dot_nt(hn_ref[...], w_ref[...])


def _in_proj(x, nw, w_t, conv_w, conv_b, side_weights, *, tm, tn, side_slabs):
    m, k = x.shape
    n = w_t.shape[0]
    assert D_CONV % tn == 0 and tn % CONV_CHUNK == 0
    n_conv_blocks = D_CONV // tn
    n_col_blocks = n // tn
    assert (m // tm) * n_col_blocks >= side_slabs
    conv_block = lambda i, j: (0, jnp.minimum(j, n_conv_blocks - 1))
    side_specs = _side_cast_specs(side_weights, side_slabs, lambda i, j: i * n_col_blocks + j)
    return pl.pallas_call(
        functools.partial(_in_proj_kernel, n_conv_blocks=n_conv_blocks, n_side=len(side_weights)),
        out_shape=(jax.ShapeDtypeStruct((m, n), F32),
                   *[jax.ShapeDtypeStruct(w.shape, BF16) for w in side_weights]),
        grid=(m // tm, n_col_blocks),
        in_specs=[
            pl.BlockSpec((tm, k), lambda i, j: (i, 0)),
            pl.BlockSpec((1, k), lambda i, j: (0, 0)),
            pl.BlockSpec((tn, k), lambda i, j: (j, 0)),
            pl.BlockSpec((CONV_WIDTH, tn), conv_block),
            pl.BlockSpec((1, tn), conv_block),
        ] + side_specs,
        out_specs=(pl.BlockSpec((tm, tn), lambda i, j: (i, j)), *side_specs),
        scratch_shapes=[pltpu.VMEM((tm, k), BF16),
                        pltpu.VMEM((n_conv_blocks, HALO_ROWS, tn), F32),
                        pltpu.VMEM((N_PARK_SLOTS, tm, CONV_CHUNK), F32)],
        compiler_params=pltpu.CompilerParams(
            dimension_semantics=("arbitrary", "arbitrary"),
            vmem_limit_bytes=V7X_VMEM_LIMIT_BYTES),
        name="in_proj",
    )(x, nw, w_t, conv_w, conv_b, *side_weights)


def _swiglu_kernel(hn_ref, wg_ref, wu_ref, *rest):
    n_side = (len(rest) - 1) // 2
    side_in, o_ref, side_out = rest[:n_side], rest[n_side], rest[n_side + 1:]
    for src_ref, dst_ref in zip(side_in, side_out):
        dst_ref[...] = src_ref[...].astype(BF16)
    half = o_ref.shape[1] // 2
    for cols in (slice(0, half), slice(half, 2 * half)):
        gate = _dot(hn_ref[...], wg_ref[:, cols])
        up = _dot(hn_ref[...], wu_ref[:, cols])
        o_ref[:, cols] = (_silu(gate) * up).astype(BF16)


def _swiglu(hn, wg, wu, side_weights, *, tm, tn):
    m, k = hn.shape
    n = wg.shape[1]
    n_col_tiles = n // tn
    side_specs = _side_cast_specs(side_weights, (m // tm) * n_col_tiles,
                                  lambda i, j: i * n_col_tiles + j)
    return pl.pallas_call(
        _swiglu_kernel,
        out_shape=(jax.ShapeDtypeStruct((m, n), BF16),
                   *[jax.ShapeDtypeStruct(w.shape, BF16) for w in side_weights]),
        grid=(m // tm, n_col_tiles),
        in_specs=[
            pl.BlockSpec((tm, k), lambda i, j: (i, 0)),
            pl.BlockSpec((k, tn), lambda i, j: (0, j)),
            pl.BlockSpec((k, tn), lambda i, j: (0, j)),
        ] + side_specs,
        out_specs=(pl.BlockSpec((tm, tn), lambda i, j: (i, j)), *side_specs),
        compiler_params=pltpu.CompilerParams(
            dimension_semantics=("arbitrary", "arbitrary"),
            vmem_limit_bytes=V7X_VMEM_LIMIT_BYTES),
        name="ffn_up",
    )(hn, wg, wu, *side_weights)


def _out_proj_kernel(ys_ref, yp_ref, w_ref, x_ref, nw_ref, h_ref, hn_ref):
    k = ys_ref.shape[1]
    h = x_ref[...] + _dot(ys_ref[...], w_ref[:k, :]) + _dot(yp_ref[...], w_ref[k:, :])
    h_ref[...] = h
    hn_ref[...] = _rms_normalize(h, nw_ref[...]).astype(BF16)


def _resident(shape):
    return pl.BlockSpec(shape, lambda i: (0,) * len(shape), pipeline_mode=pl.Buffered(1))


def _out_proj(ys, yp, w, x, nw, *, tm):
    m, k = ys.shape
    n = w.shape[1]
    row_spec = lambda width: pl.BlockSpec((tm, width), lambda i: (i, 0))
    return pl.pallas_call(
        _out_proj_kernel,
        out_shape=(jax.ShapeDtypeStruct((m, n), F32), jax.ShapeDtypeStruct((m, n), BF16)),
        grid=(m // tm,),
        in_specs=[row_spec(k), row_spec(k), _resident(w.shape), row_spec(n), _resident((1, n))],
        out_specs=(row_spec(n), row_spec(n)),
        compiler_params=pltpu.CompilerParams(
            dimension_semantics=("parallel",),
            vmem_limit_bytes=V7X_VMEM_LIMIT_BYTES),
        name="out_proj",
    )(ys, yp, w, x, nw)


def _down_kernel(a_ref, w_ref, h_ref, nw_ref, o_ref, *, final_norm):
    h = h_ref[...] + _dot(a_ref[...], w_ref[...])
    o_ref[...] = _rms_normalize(h, nw_ref[...]) if final_norm else h


def _down_proj(act, w, h, nw, *, tm, final_norm):
    m, k = act.shape
    n = w.shape[1]
    row_spec = lambda width: pl.BlockSpec((tm, width), lambda i: (i, 0))
    return pl.pallas_call(
        functools.partial(_down_kernel, final_norm=final_norm),
        out_shape=jax.ShapeDtypeStruct((m, n), F32),
        grid=(m // tm,),
        in_specs=[row_spec(k), _resident(w.shape), row_spec(n), _resident((1, n))],
        out_specs=row_spec(n),
        compiler_params=pltpu.CompilerParams(
            dimension_semantics=("parallel",),
            vmem_limit_bytes=V7X_VMEM_LIMIT_BYTES),
        name="ffn_down",
    )(act, w, h, nw)


def _pool_tile(i, u_refs, halo_refs, pw_ref, ps_ref, o_ref):
    tl = o_ref.shape[0]
    t = i * tl + lax.broadcasted_iota(jnp.int32, (tl, POOL_GROUP_DIM), 0)
    for g, window in enumerate(POOL_WINDOWS):
        cols = slice(g * POOL_GROUP_DIM, (g + 1) * POOL_GROUP_DIM)
        u = u_refs[g][...]
        halo = jnp.where(i == 0, 0.0, halo_refs[g][...])
        s = jnp.concatenate([halo, u], axis=0)
        shift = 1
        while shift < window:
            s = s + pltpu.roll(s, shift, axis=0)
            shift *= 2
        count = jnp.minimum(t + 1, window).astype(F32)
        pooled = s[POOL_HALO:] / count - u
        y = _dot(pooled.astype(BF16), pw_ref[g].astype(BF16)) * ps_ref[:, cols]
        o_ref[:, cols] = y.astype(BF16)


def _pool_specs(tl, step_of):
    gd = POOL_GROUP_DIM
    assert COL_U % gd == 0 and tl % POOL_HALO == 0
    halo_blocks_per_tile = tl // POOL_HALO
    n_groups = len(POOL_WINDOWS)
    u_specs = [pl.BlockSpec((tl, gd), lambda *g, k=k: (step_of(*g), COL_U // gd + k))
               for k in range(n_groups)]
    halo_specs = [
        pl.BlockSpec((POOL_HALO, gd),
                     lambda *g, k=k: (jnp.maximum(step_of(*g) * halo_blocks_per_tile - 1, 0),
                                      COL_U // gd + k))
        for k in range(n_groups)]
    return u_specs + halo_specs


assert N_REPLICAS * HEADS_PER_GROUP <= V7X_LANES


def _expansion_matrix():
    lane = jnp.arange(V7X_LANES)
    col = jnp.arange(N_EXPANDED * GROUP_WIDTH)
    replica, head = lane // HEADS_PER_GROUP, lane % HEADS_PER_GROUP
    quantity = replica // 3
    hit = ((quantity[:, None] == col[None, :] // GROUP_WIDTH)
           & (head[:, None] == (col[None, :] % GROUP_WIDTH) // HEAD_DIM)
           & (replica[:, None] < N_REPLICAS))
    return hit.astype(BF16)


def _ssd_group(z, xs, bm, cm, dt_raw, dtb, alog, dskip, nw, state_ref, expand, tril, causal):
    q = xs.shape[0]
    x = dt_raw + dtb
    dt = jnp.maximum(x, 0.0) + jnp.log1p(jnp.exp(-jnp.abs(x)))
    a = -jnp.exp(alog)
    a_cum = _dot3_left(tril, dt * a) * LOG2_E
    a_cum_t = a_cum.T
    a_last = a_cum[q - 1:q, :]

    replica = lax.broadcasted_iota(jnp.int32, (1, V7X_LANES), 1) // HEADS_PER_GROUP
    packed = jnp.where(replica < 3, dt,
                       jnp.where(replica < 6, jnp.exp2(a_cum), jnp.exp2(a_last - a_cum)))
    hi = packed.astype(BF16)
    r1 = packed - hi.astype(F32)
    r2 = r1 - r1.astype(BF16).astype(F32)
    level = replica % 3
    split = jnp.where(level == 0, packed, jnp.where(level == 1, r1, r2)).astype(BF16)
    expanded = _dot(split, expand)
    dt_b = expanded[:, :GROUP_WIDTH]
    decay_in_b = expanded[:, GROUP_WIDTH:2 * GROUP_WIDTH]
    decay_out_b = expanded[:, 2 * GROUP_WIDTH:]

    xdt = xs * dt_b
    bm_bf = bm.astype(BF16)
    cm_bf = cm.astype(BF16)
    cb = _dot_nt(cm_bf, bm_bf)

    xdt_bf = xdt.astype(BF16)
    lane = lax.broadcasted_iota(jnp.int32, (1, V7X_LANES), 1)
    keep = [jnp.where((lane // HEAD_DIM) == side, 1.0, 0.0).astype(BF16) for side in range(2)]
    pairs = []
    for pair in range(HEADS_PER_GROUP // 2):
        xdt_pair = xdt_bf[:, pair * V7X_LANES:(pair + 1) * V7X_LANES]
        acc = jnp.zeros((q, V7X_LANES), F32)
        for side in range(2):
            r = 2 * pair + side
            seg = a_cum[:, r:r + 1] - a_cum_t[r:r + 1, :]
            decay = jnp.exp2(jnp.where(causal, seg, -jnp.inf))
            m_r = (cb * decay).astype(BF16)
            acc = acc + _dot(m_r, xdt_pair * keep[side])
        pairs.append(acc)
    y_diag = jnp.concatenate(pairs, axis=-1)

    state = state_ref[...]
    y_off = _dot(cm_bf, state.astype(BF16)) * decay_in_b
    xw = (xdt * decay_out_b).astype(BF16)
    new_state = lax.dot_general(bm_bf, xw, (((0,), (0,)), ((), ())), preferred_element_type=F32)
    state_ref[...] = state * decay_in_b[q - 1:q, :] + new_state

    y = y_diag + y_off + dskip * xs
    g = y * _silu(z)
    return _rms_normalize(g, nw).astype(BF16)


def _mixers_kernel(*refs, groups, n_side):
    refs = list(refs)
    take = lambda n: [refs.pop(0) for _ in range(n)]
    n_pool = len(POOL_WINDOWS)
    z_refs = take(groups)
    xs_ref, b_ref, c_ref, dt_ref, dtb_ref, alog_ref, dskip_ref, nw_ref, expand_ref = take(9)
    u_refs, halo_refs = take(n_pool), take(n_pool)
    pw_ref, ps_ref = take(2)
    side_in = take(n_side)
    o_ref, yp_ref = take(2)
    side_out = take(n_side)
    state_ref, = refs
    q = SSD_CHUNK

    _pool_tile(pl.program_id(1), u_refs, halo_refs, pw_ref, ps_ref, yp_ref)

    for src_ref, dst_ref in zip(side_in, side_out):
        dst_ref[...] = src_ref[...].astype(BF16)

    @pl.when(pl.program_id(1) == 0)
    def _():
        state_ref[...] = jnp.zeros_like(state_ref)

    row_i = lax.broadcasted_iota(jnp.int32, (q, q), 0)
    col_i = lax.broadcasted_iota(jnp.int32, (q, q), 1)
    causal = row_i >= col_i
    tril = jnp.where(causal, 1.0, 0.0).astype(BF16)
    expand = expand_ref[...]
    for g in range(groups):
        wide = slice(g * GROUP_WIDTH, (g + 1) * GROUP_WIDTH)
        narrow = slice(g * D_STATE, (g + 1) * D_STATE)
        o_ref[:, wide] = _ssd_group(
            z_refs[g][...], xs_ref[:, wide], b_ref[:, narrow], c_ref[:, narrow], dt_ref[:, narrow],
            dtb_ref[:, narrow], alog_ref[:, narrow], dskip_ref[:, wide], nw_ref[:, wide],
            state_ref.at[g], expand, tril, causal)


def _mixers(proj, dtb_c, alog_c, dskip_full, norm_w, pool_w, pool_scale, side_weights, *, groups):
    m = proj.shape[0]
    q = SSD_CHUNK
    gw = groups * GROUP_WIDTH
    ds = groups * D_STATE
    assert V7X_LANES == D_STATE and N_GROUPS % groups == 0

    def col(start, width):
        assert start % width == 0
        return start // width

    row_spec = lambda width, block0: pl.BlockSpec((q, width), lambda g, c: (c, block0 + g))
    par_spec = lambda width: pl.BlockSpec((1, width), lambda g, c: (0, g))
    z_specs = [
        pl.BlockSpec((q, GROUP_WIDTH),
                     lambda g, c, k=k: (c, col(COL_Z, GROUP_WIDTH) + g * groups + k))
        for k in range(groups)]
    expand = _expansion_matrix()
    assert groups == N_GROUPS
    side_specs = _side_cast_specs(side_weights, m // q, lambda g, c: c)
    return pl.pallas_call(
        functools.partial(_mixers_kernel, groups=groups, n_side=len(side_weights)),
        out_shape=(jax.ShapeDtypeStruct((m, D_SSM), BF16), jax.ShapeDtypeStruct((m, D_POOL), BF16),
                   *[jax.ShapeDtypeStruct(w.shape, BF16) for w in side_weights]),
        grid=(N_GROUPS // groups, m // q),
        in_specs=z_specs + [
            row_spec(gw, col(COL_XS, gw)),
            row_spec(ds, col(COL_B, ds)),
            row_spec(ds, col(COL_C, ds)),
            row_spec(ds, col(COL_DT, ds)),
            par_spec(ds),
            par_spec(ds),
            par_spec(gw),
            par_spec(gw),
            pl.BlockSpec(expand.shape, lambda g, c: (0, 0)),
        ] + _pool_specs(q, lambda g, c: c) + [
            pl.BlockSpec(pool_w.shape, lambda g, c: (0, 0, 0)),
            pl.BlockSpec(pool_scale.shape, lambda g, c: (0, 0)),
        ] + side_specs,
        out_specs=(pl.BlockSpec((q, gw), lambda g, c: (c, g)),
                   pl.BlockSpec((q, D_POOL), lambda g, c: (c, 0)), *side_specs),
        scratch_shapes=[pltpu.VMEM((groups, D_STATE, GROUP_WIDTH), F32)],
        compiler_params=pltpu.CompilerParams(
            dimension_semantics=("parallel", "arbitrary"),
            vmem_limit_bytes=V7X_VMEM_LIMIT_BYTES),
        name="mixers",
    )(*([proj] * (groups + 4)), dtb_c, alog_c, dskip_full, norm_w, expand,
      *([proj] * (2 * len(POOL_WINDOWS))), pool_w, pool_scale, *side_weights)


def _pad_heads_to_lanes(v):
    v = jnp.tile(v.reshape(N_GROUPS, HEADS_PER_GROUP), (1, N_REPLICAS))
    v = jnp.pad(v, ((0, 0), (0, V7X_LANES - N_REPLICAS * HEADS_PER_GROUP)))
    return v.reshape(1, N_GROUPS * V7X_LANES)


def kernel(x, attn_norm_w, w_in, conv_w, conv_b, dt_bias, a_log, d_skip, ssd_norm_w, pool_w,
           pool_scale, w_out, ffn_norm_w, w_gate, w_up, w_down, final_norm_w):
    bsz, seqlen, d_model = x.shape
    depth = w_in.shape[0]
    h = x.reshape(bsz * seqlen, d_model)
    assert bsz == 1 and seqlen % 2048 == 0 and d_model == D_MODEL

    for i in range(depth):
        w_cat = _w_in_prep(w_in[i].T, tk=256)
        pool_w_rows = pool_w[i].reshape(-1, POOL_GROUP_DIM)
        proj, w_out_bf, pool_w_bf = _in_proj(
            h, attn_norm_w[i][None, :], w_cat, conv_w[i], conv_b[i][None, :],
            (w_out[i], pool_w_rows), tm=1024, tn=1536, side_slabs=32)

        y_ssd, y_pool, w_gate_bf, w_up_bf = _mixers(
            proj, _pad_heads_to_lanes(dt_bias[i]), _pad_heads_to_lanes(a_log[i]),
            jnp.repeat(d_skip[i], HEAD_DIM)[None, :], ssd_norm_w[i][None, :],
            pool_w_bf.reshape(pool_w[i].shape), pool_scale[i][None, :], (w_gate[i], w_up[i]),
            groups=N_GROUPS)

        h, hn = _out_proj(y_ssd, y_pool, w_out_bf, h, ffn_norm_w[i][None, :], tm=512)

        act, w_down_bf = _swiglu(hn, w_gate_bf, w_up_bf, (w_down[i],), tm=2048, tn=512)
        last = i == depth - 1
        h = _down_proj(act, w_down_bf, h, final_norm_w[None, :], tm=512, final_norm=last)

    return h.reshape(bsz, seqlen, d_model)
```

```python
import functools

import jax
import jax.numpy as jnp
from jax import lax
from jax.experimental import pallas as pl
from jax.experimental.pallas import tpu as pltpu

F32 = jnp.float32
BF16 = jnp.bfloat16

NORM_EPS = 1e-5
LOG2_E = 1.4426950408889634
D_MODEL = 2048
D_SSM = 2048
HEAD_DIM = 64
N_HEADS = D_SSM // HEAD_DIM
N_GROUPS = 4
HEADS_PER_GROUP = N_HEADS // N_GROUPS
GROUP_WIDTH = D_SSM // N_GROUPS
D_STATE = 128
CONV_WIDTH = 4
D_POOL = 2048
POOL_WINDOWS = (2, 4, 8, 16)
POOL_GROUP_DIM = D_POOL // len(POOL_WINDOWS)

V7X_LANES = 128
V7X_SUBLANES = 8
V7X_VMEM_LIMIT_BYTES = 60 * 1024 * 1024

W_PREP_COLS = 256
IN_PROJ_TILE = (1024, 1536)
IN_PROJ_SIDE_SLABS = 32
RESIDENT_ROW_TILE = 512
FFN_UP_TILE = (2048, 512)

COL_XS = 0
COL_B = COL_XS + D_SSM
COL_C = COL_B + N_GROUPS * D_STATE
D_CONV = COL_C + N_GROUPS * D_STATE
COL_Z = D_CONV
COL_U = COL_Z + D_SSM
COL_DT = COL_U + D_POOL
D_PROJ = COL_DT + N_GROUPS * V7X_LANES

SSD_CHUNK = 256
N_EXPANDED = 3
N_REPLICAS = 3 * N_EXPANDED
HALO_ROWS = V7X_SUBLANES
POOL_HALO = 16


def _sigmoid(x):
    return 1.0 / (1.0 + jnp.exp(-x))


def _silu(x):
    return x * _sigmoid(x)


def _rms_normalize(x, w):
    ms = jnp.mean(x * x, axis=-1, keepdims=True)
    return x * lax.rsqrt(ms + NORM_EPS) * w


def _split3(x):
    hi = x.astype(BF16)
    r1 = x - hi.astype(F32)
    mid = r1.astype(BF16)
    lo = (r1 - mid.astype(F32)).astype(BF16)
    return hi, mid, lo


def _dot(a, b):
    return jnp.dot(a, b, preferred_element_type=F32)


def _dot_nt(a, b):
    return lax.dot_general(a, b, (((1,), (1,)), ((), ())), preferred_element_type=F32)


def _dot3_left(lhs_bf16, x):
    hi, mid, lo = _split3(x)
    return _dot(lhs_bf16, hi) + _dot(lhs_bf16, mid) + _dot(lhs_bf16, lo)


SRC_XBC = D_SSM
SRC_DT = SRC_XBC + D_SSM + 2 * N_GROUPS * D_STATE
SRC_U = SRC_DT + N_HEADS
D_IN_PROJ = SRC_U + D_POOL


def _w_in_prep_kernel(w_ref, o_ref):
    o_ref[COL_Z:COL_Z + D_SSM, :] = w_ref[:D_SSM, :].astype(BF16)
    o_ref[COL_U:COL_U + D_POOL, :] = w_ref[SRC_U:SRC_U + D_POOL, :].astype(BF16)
    o_ref[COL_XS:D_CONV, :] = w_ref[SRC_XBC:SRC_DT, :].astype(BF16)
    zeros = jnp.zeros((V7X_LANES - N_REPLICAS * HEADS_PER_GROUP, o_ref.shape[1]), F32)
    for g in range(N_GROUPS):
        row0 = COL_DT + g * V7X_LANES
        src0 = SRC_DT + g * HEADS_PER_GROUP
        heads = w_ref[src0:src0 + HEADS_PER_GROUP, :]
        block = jnp.concatenate([heads] * N_REPLICAS + [zeros], axis=0)
        o_ref[row0:row0 + V7X_LANES, :] = block.astype(BF16)


def _w_in_prep(w_t, *, tk):
    n, k = w_t.shape
    assert n == D_IN_PROJ
    return pl.pallas_call(
        _w_in_prep_kernel,
        out_shape=jax.ShapeDtypeStruct((D_PROJ, k), BF16),
        grid=(k // tk,),
        in_specs=[pl.BlockSpec((n, tk), lambda i: (0, i))],
        out_specs=pl.BlockSpec((D_PROJ, tk), lambda i: (0, i)),
        compiler_params=pltpu.CompilerParams(
            dimension_semantics=("parallel",),
            vmem_limit_bytes=V7X_VMEM_LIMIT_BYTES),
        name="w_in_prep",
    )(w_t)


def _side_cast_specs(weights, n_slabs, step_of):
    specs = []
    for w in weights:
        rows = w.shape[0] // n_slabs
        assert rows * n_slabs == w.shape[0] and rows % (2 * V7X_SUBLANES) == 0
        specs.append(pl.BlockSpec(
            (rows, w.shape[1]), lambda *g: (jnp.minimum(step_of(*g), n_slabs - 1), 0)))
    return specs


CONV_CHUNK = 256


def _conv_silu(r, halo, w_ref, b_ref, cols):
    ext = jnp.concatenate([halo, r], axis=0)
    acc = r * w_ref[CONV_WIDTH - 1:CONV_WIDTH, cols] + b_ref[:, cols]
    for back in range(1, CONV_WIDTH):
        tap = w_ref[CONV_WIDTH - 1 - back:CONV_WIDTH - back, cols]
        acc = acc + pltpu.roll(ext, back, axis=0)[HALO_ROWS:] * tap
    return _silu(acc)


def _in_proj_kernel(x_ref, nw_ref, w_ref, cw_ref, cb_ref, *rest, n_conv_blocks, n_side):
    side_in, rest = rest[:n_side], rest[n_side:]
    o_ref, side_out = rest[0], rest[1:1 + n_side]
    hn_ref, halo_ref, raw_ref = rest[1 + n_side:]
    i = pl.program_id(0)
    j = pl.program_id(1)
    tm, tn = o_ref.shape

    for src_ref, dst_ref in zip(side_in, side_out):
        dst_ref[...] = src_ref[...].astype(BF16)

    @pl.when(j == 0)
    def _():
        hn_ref[...] = _rms_normalize(x_ref[...], nw_ref[...]).astype(BF16)

    @pl.when((i == 0) & (j == 0))
    def _():
        halo_ref[...] = jnp.zeros_like(halo_ref)

    @pl.when(j < n_conv_blocks)
    def _():
        n_pieces = tn // CONV_CHUNK
        piece = lambda c: slice(c * CONV_CHUNK, (c + 1) * CONV_CHUNK)
        park = jnp.minimum(j, 0)
        raw_ref[park] = _dot_nt(hn_ref[...], w_ref[piece(0), :])
        for c in range(n_pieces):
            cols = piece(c)
            r = raw_ref[park]
            o_ref[:, cols] = _conv_silu(r, halo_ref[j, :, cols], cw_ref, cb_ref, cols)
            halo_ref[j, :, cols] = r[tm - HALO_ROWS:]
            if c + 1 < n_pieces:
                raw_ref[park] = _dot_nt(hn_ref[...], w_ref[piece(c + 1), :])

    @pl.when(j >= n_conv_blocks)
    def _():
        o_ref[...] = _dot_nt(hn_ref[...], w_ref[...])


def _in_proj(x, nw, w_t, conv_w, conv_b, side_weights, *, tm, tn, side_slabs):
    m, k = x.shape
    n = w_t.shape[0]
    assert D_CONV % tn == 0 and tn % CONV_CHUNK == 0
    n_conv_blocks = D_CONV // tn
    n_col_blocks = n // tn
    assert (m // tm) * n_col_blocks >= side_slabs
    conv_block = lambda i, j: (0, jnp.minimum(j, n_conv_blocks - 1))
    side_specs = _side_cast_specs(side_weights, side_slabs, lambda i, j: i * n_col_blocks + j)
    return pl.pallas_call(
        functools.partial(_in_proj_kernel, n_conv_blocks=n_conv_blocks, n_side=len(side_weights)),
        out_shape=(jax.ShapeDtypeStruct((m, n), F32),
                   *[jax.ShapeDtypeStruct(w.shape, BF16) for w in side_weights]),
        grid=(m // tm, n_col_blocks),
        in_specs=[
            pl.BlockSpec((tm, k), lambda i, j: (i, 0)),
            pl.BlockSpec((1, k), lambda i, j: (0, 0)),
            pl.BlockSpec((tn, k), lambda i, j: (j, 0)),
            pl.BlockSpec((CONV_WIDTH, tn), conv_block),
            pl.BlockSpec((1, tn), conv_block),
        ] + side_specs,
        out_specs=(pl.BlockSpec((tm, tn), lambda i, j: (i, j)), *side_specs),
        scratch_shapes=[pltpu.VMEM((tm, k), BF16),
                        pltpu.VMEM((n_conv_blocks, HALO_ROWS, tn), F32),
                        pltpu.VMEM((1, tm, CONV_CHUNK), F32)],
        compiler_params=pltpu.CompilerParams(
            dimension_semantics=("arbitrary", "arbitrary"),
            vmem_limit_bytes=V7X_VMEM_LIMIT_BYTES),
        name="in_proj",
    )(x, nw, w_t, conv_w, conv_b, *side_weights)


def _swiglu_kernel(hn_ref, wg_ref, wu_ref, *rest):
    n_side = (len(rest) - 1) // 2
    side_in, o_ref, side_out = rest[:n_side], rest[n_side], rest[n_side + 1:]
    for src_ref, dst_ref in zip(side_in, side_out):
        dst_ref[...] = src_ref[...].astype(BF16)
    half = o_ref.shape[1] // 2
    for cols in (slice(0, half), slice(half, 2 * half)):
        gate = _dot(hn_ref[...], wg_ref[:, cols])
        up = _dot(hn_ref[...], wu_ref[:, cols])
        o_ref[:, cols] = (_silu(gate) * up).astype(BF16)


def _swiglu(hn, wg, wu, side_weights, *, tm, tn):
    m, k = hn.shape
    n = wg.shape[1]
    n_col_tiles = n // tn
    side_specs = _side_cast_specs(side_weights, (m // tm) * n_col_tiles,
                                  lambda i, j: i * n_col_tiles + j)
    return pl.pallas_call(
        _swiglu_kernel,
        out_shape=(jax.ShapeDtypeStruct((m, n), BF16),
                   *[jax.ShapeDtypeStruct(w.shape, BF16) for w in side_weights]),
        grid=(m // tm, n_col_tiles),
        in_specs=[
            pl.BlockSpec((tm, k), lambda i, j: (i, 0)),
            pl.BlockSpec((k, tn), lambda i, j: (0, j)),
            pl.BlockSpec((k, tn), lambda i, j: (0, j)),
        ] + side_specs,
        out_specs=(pl.BlockSpec((tm, tn), lambda i, j: (i, j)), *side_specs),
        compiler_params=pltpu.CompilerParams(
            dimension_semantics=("arbitrary", "arbitrary"),
            vmem_limit_bytes=V7X_VMEM_LIMIT_BYTES),
        name="ffn_up",
    )(hn, wg, wu, *side_weights)


def _out_proj_kernel(ys_ref, yp_ref, w_ref, x_ref, nw_ref, h_ref, hn_ref):
    k = ys_ref.shape[1]
    h = x_ref[...] + _dot(ys_ref[...], w_ref[:k, :]) + _dot(yp_ref[...], w_ref[k:, :])
    h_ref[...] = h
    hn_ref[...] = _rms_normalize(h, nw_ref[...]).astype(BF16)


def _resident(shape):
    return pl.BlockSpec(shape, lambda i: (0,) * len(shape), pipeline_mode=pl.Buffered(1))


def _out_proj(ys, yp, w, x, nw, *, tm):
    m, k = ys.shape
    n = w.shape[1]
    row_spec = lambda width: pl.BlockSpec((tm, width), lambda i: (i, 0))
    return pl.pallas_call(
        _out_proj_kernel,
        out_shape=(jax.ShapeDtypeStruct((m, n), F32), jax.ShapeDtypeStruct((m, n), BF16)),
        grid=(m // tm,),
        in_specs=[row_spec(k), row_spec(k), _resident(w.shape), row_spec(n), _resident((1, n))],
        out_specs=(row_spec(n), row_spec(n)),
        compiler_params=pltpu.CompilerParams(
            dimension_semantics=("parallel",),
            vmem_limit_bytes=V7X_VMEM_LIMIT_BYTES),
        name="out_proj",
    )(ys, yp, w, x, nw)


def _down_kernel(a_ref, w_ref, h_ref, nw_ref, o_ref, *, final_norm):
    h = h_ref[...] + _dot(a_ref[...], w_ref[...])
    o_ref[...] = _rms_normalize(h, nw_ref[...]) if final_norm else h


def _down_proj(act, w, h, nw, *, tm, final_norm):
    m, k = act.shape
    n = w.shape[1]
    row_spec = lambda width: pl.BlockSpec((tm, width), lambda i: (i, 0))
    return pl.pallas_call(
        functools.partial(_down_kernel, final_norm=final_norm),
        out_shape=jax.ShapeDtypeStruct((m, n), F32),
        grid=(m // tm,),
        in_specs=[row_spec(k), _resident(w.shape), row_spec(n), _resident((1, n))],
        out_specs=row_spec(n),
        compiler_params=pltpu.CompilerParams(
            dimension_semantics=("parallel",),
            vmem_limit_bytes=V7X_VMEM_LIMIT_BYTES),
        name="ffn_down",
    )(act, w, h, nw)


def _pool_tile(i, u_refs, halo_refs, pw_ref, ps_ref, o_ref):
    tl = o_ref.shape[0]
    t = i * tl + lax.broadcasted_iota(jnp.int32, (tl, POOL_GROUP_DIM), 0)
    for g, window in enumerate(POOL_WINDOWS):
        cols = slice(g * POOL_GROUP_DIM, (g + 1) * POOL_GROUP_DIM)
        u = u_refs[g][...]
        halo = jnp.where(i == 0, 0.0, halo_refs[g][...])
        s = jnp.concatenate([halo, u], axis=0)
        shift = 1
        while shift < window:
            s = s + pltpu.roll(s, shift, axis=0)
            shift *= 2
        count = jnp.minimum(t + 1, window).astype(F32)
        pooled = s[POOL_HALO:] / count - u
        y = _dot(pooled.astype(BF16), pw_ref[g].astype(BF16)) * ps_ref[:, cols]
        o_ref[:, cols] = y.astype(BF16)


def _pool_specs(tl, step_of):
    gd = POOL_GROUP_DIM
    assert COL_U % gd == 0 and tl % POOL_HALO == 0
    halo_blocks_per_tile = tl // POOL_HALO
    n_groups = len(POOL_WINDOWS)
    u_specs = [pl.BlockSpec((tl, gd), lambda *g, k=k: (step_of(*g), COL_U // gd + k))
               for k in range(n_groups)]
    halo_specs = [
        pl.BlockSpec((POOL_HALO, gd),
                     lambda *g, k=k: (jnp.maximum(step_of(*g) * halo_blocks_per_tile - 1, 0),
                                      COL_U // gd + k))
        for k in range(n_groups)]
    return u_specs + halo_specs


assert N_REPLICAS * HEADS_PER_GROUP <= V7X_LANES


def _expansion_matrix():
    lane = jnp.arange(V7X_LANES)
    col = jnp.arange(N_EXPANDED * GROUP_WIDTH)
    replica, head = lane // HEADS_PER_GROUP, lane % HEADS_PER_GROUP
    quantity = replica // 3
    hit = ((quantity[:, None] == col[None, :] // GROUP_WIDTH)
           & (head[:, None] == (col[None, :] % GROUP_WIDTH) // HEAD_DIM)
           & (replica[:, None] < N_REPLICAS))
    return hit.astype(BF16)


def _ssd_group(z, xs, bm, cm, dt_raw, dtb, alog, dskip, nw, state_ref, expand, tril, causal):
    q = xs.shape[0]
    x = dt_raw + dtb
    dt = jnp.maximum(x, 0.0) + jnp.log1p(jnp.exp(-jnp.abs(x)))
    a = -jnp.exp(alog)
    a_cum = _dot3_left(tril, dt * a) * LOG2_E
    a_cum_t = a_cum.T
    a_last = a_cum[q - 1:q, :]

    replica = lax.broadcasted_iota(jnp.int32, (1, V7X_LANES), 1) // HEADS_PER_GROUP
    packed = jnp.where(replica < 3, dt,
                       jnp.where(replica < 6, jnp.exp2(a_cum), jnp.exp2(a_last - a_cum)))
    hi = packed.astype(BF16)
    r1 = packed - hi.astype(F32)
    r2 = r1 - r1.astype(BF16).astype(F32)
    level = replica % 3
    split = jnp.where(level == 0, packed, jnp.where(level == 1, r1, r2)).astype(BF16)
    expanded = _dot(split, expand)
    dt_b = expanded[:, :GROUP_WIDTH]
    decay_in_b = expanded[:, GROUP_WIDTH:2 * GROUP_WIDTH]
    decay_out_b = expanded[:, 2 * GROUP_WIDTH:]

    xdt = xs * dt_b
    bm_bf = bm.astype(BF16)
    cm_bf = cm.astype(BF16)
    cb = _dot_nt(cm_bf, bm_bf)

    xdt_bf = xdt.astype(BF16)
    lane = lax.broadcasted_iota(jnp.int32, (1, V7X_LANES), 1)
    keep = [jnp.where((lane // HEAD_DIM) == side, 1.0, 0.0).astype(BF16) for side in range(2)]
    pairs = []
    for pair in range(HEADS_PER_GROUP // 2):
        xdt_pair = xdt_bf[:, pair * V7X_LANES:(pair + 1) * V7X_LANES]
        acc = jnp.zeros((q, V7X_LANES), F32)
        for side in range(2):
            r = 2 * pair + side
            seg = a_cum[:, r:r + 1] - a_cum_t[r:r + 1, :]
            decay = jnp.exp2(jnp.where(causal, seg, -jnp.inf))
            m_r = (cb * decay).astype(BF16)
            acc = acc + _dot(m_r, xdt_pair * keep[side])
        pairs.append(acc)
    y_diag = jnp.concatenate(pairs, axis=-1)

    state = state_ref[...]
    y_off = _dot(cm_bf, state.astype(BF16)) * decay_in_b
    xw = (xdt * decay_out_b).astype(BF16)
    new_state = lax.dot_general(bm_bf, xw, (((0,), (0,)), ((), ())), preferred_element_type=F32)
    state_ref[...] = state * decay_in_b[q - 1:q, :] + new_state

    y = y_diag + y_off + dskip * xs
    g = y * _silu(z)
    return _rms_normalize(g, nw).astype(BF16)


def _mixers_kernel(*refs, groups, n_side):
    refs = list(refs)
    take = lambda n: [refs.pop(0) for _ in range(n)]
    n_pool = len(POOL_WINDOWS)
    z_refs = take(groups)
    xs_ref, b_ref, c_ref, dt_ref, dtb_ref, alog_ref, dskip_ref, nw_ref, expand_ref = take(9)
    u_refs, halo_refs = take(n_pool), take(n_pool)
    pw_ref, ps_ref = take(2)
    side_in = take(n_side)
    o_ref, yp_ref = take(2)
    side_out = take(n_side)
    state_ref, = refs
    q = SSD_CHUNK

    _pool_tile(pl.program_id(1), u_refs, halo_refs, pw_ref, ps_ref, yp_ref)

    for src_ref, dst_ref in zip(side_in, side_out):
        dst_ref[...] = src_ref[...].astype(BF16)

    @pl.when(pl.program_id(1) == 0)
    def _():
        state_ref[...] = jnp.zeros_like(state_ref)

    row_i = lax.broadcasted_iota(jnp.int32, (q, q), 0)
    col_i = lax.broadcasted_iota(jnp.int32, (q, q), 1)
    causal = row_i >= col_i
    tril = jnp.where(causal, 1.0, 0.0).astype(BF16)
    expand = expand_ref[...]
    for g in range(groups):
        wide = slice(g * GROUP_WIDTH, (g + 1) * GROUP_WIDTH)
        narrow = slice(g * D_STATE, (g + 1) * D_STATE)
        o_ref[:, wide] = _ssd_group(
            z_refs[g][...], xs_ref[:, wide], b_ref[:, narrow], c_ref[:, narrow], dt_ref[:, narrow],
            dtb_ref[:, narrow], alog_ref[:, narrow], dskip_ref[:, wide], nw_ref[:, wide],
            state_ref.at[g], expand, tril, causal)


def _mixers(proj, dtb_c, alog_c, dskip_full, norm_w, pool_w, pool_scale, side_weights, *, groups):
    m = proj.shape[0]
    q = SSD_CHUNK
    gw = groups * GROUP_WIDTH
    ds = groups * D_STATE
    assert V7X_LANES == D_STATE and N_GROUPS % groups == 0

    def col(start, width):
        assert start % width == 0
        return start // width

    row_spec = lambda width, block0: pl.BlockSpec((q, width), lambda g, c: (c, block0 + g))
    par_spec = lambda width: pl.BlockSpec((1, width), lambda g, c: (0, g))
    z_specs = [
        pl.BlockSpec((q, GROUP_WIDTH),
                     lambda g, c, k=k: (c, col(COL_Z, GROUP_WIDTH) + g * groups + k))
        for k in range(groups)]
    expand = _expansion_matrix()
    assert groups == N_GROUPS
    side_specs = _side_cast_specs(side_weights, m // q, lambda g, c: c)
    return pl.pallas_call(
        functools.partial(_mixers_kernel, groups=groups, n_side=len(side_weights)),
        out_shape=(jax.ShapeDtypeStruct((m, D_SSM), BF16), jax.ShapeDtypeStruct((m, D_POOL), BF16),
                   *[jax.ShapeDtypeStruct(w.shape, BF16) for w in side_weights]),
        grid=(N_GROUPS // groups, m // q),
        in_specs=z_specs + [
            row_spec(gw, col(COL_XS, gw)),
            row_spec(ds, col(COL_B, ds)),
            row_spec(ds, col(COL_C, ds)),
            row_spec(ds, col(COL_DT, ds)),
            par_spec(ds),
            par_spec(ds),
            par_spec(gw),
            par_spec(gw),
            pl.BlockSpec(expand.shape, lambda g, c: (0, 0)),
        ] + _pool_specs(q, lambda g, c: c) + [
            pl.BlockSpec(pool_w.shape, lambda g, c: (0, 0, 0)),
            pl.BlockSpec(pool_scale.shape, lambda g, c: (0, 0)),
        ] + side_specs,
        out_specs=(pl.BlockSpec((q, gw), lambda g, c: (c, g)),
                   pl.BlockSpec((q, D_POOL), lambda g, c: (c, 0)), *side_specs),
        scratch_shapes=[pltpu.VMEM((groups, D_STATE, GROUP_WIDTH), F32)],
        compiler_params=pltpu.CompilerParams(
            dimension_semantics=("parallel", "arbitrary"),
            vmem_limit_bytes=V7X_VMEM_LIMIT_BYTES),
        name="mixers",
    )(*([proj] * (groups + 4)), dtb_c, alog_c, dskip_full, norm_w, expand,
      *([proj] * (2 * len(POOL_WINDOWS))), pool_w, pool_scale, *side_weights)


def _pad_heads_to_lanes(v):
    v = jnp.tile(v.reshape(N_GROUPS, HEADS_PER_GROUP), (1, N_REPLICAS))
    v = jnp.pad(v, ((0, 0), (0, V7X_LANES - N_REPLICAS * HEADS_PER_GROUP)))
    return v.reshape(1, N_GROUPS * V7X_LANES)


def kernel(x, attn_norm_w, w_in, conv_w, conv_b, dt_bias, a_log, d_skip, ssd_norm_w, pool_w,
           pool_scale, w_out, ffn_norm_w, w_gate, w_up, w_down, final_norm_w):
    bsz, seqlen, d_model = x.shape
    depth = w_in.shape[0]
    h = x.reshape(bsz * seqlen, d_model)
    assert bsz == 1 and seqlen % 2048 == 0 and d_model == D_MODEL

    for i in range(depth):
        w_cat = _w_in_prep(w_in[i].T, tk=W_PREP_COLS)
        pool_w_rows = pool_w[i].reshape(-1, POOL_GROUP_DIM)
        proj, w_out_bf, pool_w_bf = _in_proj(
            h, attn_norm_w[i][None, :], w_cat, conv_w[i], conv_b[i][None, :],
            (w_out[i], pool_w_rows), tm=IN_PROJ_TILE[0], tn=IN_PROJ_TILE[1],
            side_slabs=IN_PROJ_SIDE_SLABS)

        y_ssd, y_pool, w_gate_bf, w_up_bf = _mixers(
            proj, _pad_heads_to_lanes(dt_bias[i]), _pad_heads_to_lanes(a_log[i]),
            jnp.repeat(d_skip[i], HEAD_DIM)[None, :], ssd_norm_w[i][None, :],
            pool_w_bf.reshape(pool_w[i].shape), pool_scale[i][None, :], (w_gate[i], w_up[i]),
            groups=N_GROUPS)

        h, hn = _out_proj(y_ssd, y_pool, w_out_bf, h, ffn_norm_w[i][None, :], tm=RESIDENT_ROW_TILE)

        act, w_down_bf = _swiglu(hn, w_gate_bf, w_up_bf, (w_down[i],),
                                 tm=FFN_UP_TILE[0], tn=FFN_UP_TILE[1])
        last = i == depth - 1
        h = _down_proj(act, w_down_bf, h, final_norm_w[None, :], tm=RESIDENT_ROW_TILE,
                       final_norm=last)

    return h.reshape(bsz, seqlen, d_model)
```

```python
import functools

import jax
import jax.numpy as jnp
from jax import lax
from jax.experimental import pallas as pl
from jax.experimental.pallas import tpu as pltpu

F32 = jnp.float32
BF16 = jnp.bfloat16

NORM_EPS = 1e-5
LOG2_E = 1.4426950408889634
D_MODEL = 2048
D_SSM = 2048
HEAD_DIM = 64
N_HEADS = D_SSM // HEAD_DIM
N_GROUPS = 4
HEADS_PER_GROUP = N_HEADS // N_GROUPS
GROUP_WIDTH = D_SSM // N_GROUPS
D_STATE = 128
CONV_WIDTH = 4
D_POOL = 2048
POOL_WINDOWS = (2, 4, 8, 16)
POOL_GROUP_DIM = D_POOL // len(POOL_WINDOWS)

V7X_LANES = 128
V7X_SUBLANES = 8
V7X_VMEM_LIMIT_BYTES = 60 * 1024 * 1024

W_PREP_COLS = 512
IN_PROJ_TILE = (1024, 1536)
IN_PROJ_SIDE_SLABS = 32
RESIDENT_ROW_TILE = 512
FFN_UP_TILE = (2048, 512)

COL_XS = 0
COL_B = COL_XS + D_SSM
COL_C = COL_B + N_GROUPS * D_STATE
D_CONV = COL_C + N_GROUPS * D_STATE
COL_Z = D_CONV
COL_U = COL_Z + D_SSM
COL_DT = COL_U + D_POOL
D_PROJ = COL_DT + N_GROUPS * V7X_LANES

SSD_CHUNK = 256
N_EXPANDED = 3
N_REPLICAS = 3 * N_EXPANDED
HALO_ROWS = V7X_SUBLANES
POOL_HALO = 16


def _sigmoid(x):
    return 1.0 / (1.0 + jnp.exp(-x))


def _silu(x):
    return x * _sigmoid(x)


def _rms_normalize(x, w):
    ms = jnp.mean(x * x, axis=-1, keepdims=True)
    return x * lax.rsqrt(ms + NORM_EPS) * w


def _split3(x):
    hi = x.astype(BF16)
    r1 = x - hi.astype(F32)
    mid = r1.astype(BF16)
    lo = (r1 - mid.astype(F32)).astype(BF16)
    return hi, mid, lo


def _dot(a, b):
    return jnp.dot(a, b, preferred_element_type=F32)


def _dot_nt(a, b):
    return lax.dot_general(a, b, (((1,), (1,)), ((), ())), preferred_element_type=F32)


def _dot3_left(lhs_bf16, x):
    hi, mid, lo = _split3(x)
    return _dot(lhs_bf16, hi) + _dot(lhs_bf16, mid) + _dot(lhs_bf16, lo)


SRC_XBC = D_SSM
SRC_DT = SRC_XBC + D_SSM + 2 * N_GROUPS * D_STATE
SRC_U = SRC_DT + N_HEADS
D_IN_PROJ = SRC_U + D_POOL


def _w_in_prep_kernel(w_ref, o_ref):
    o_ref[COL_Z:COL_Z + D_SSM, :] = w_ref[:D_SSM, :].astype(BF16)
    o_ref[COL_U:COL_U + D_POOL, :] = w_ref[SRC_U:SRC_U + D_POOL, :].astype(BF16)
    o_ref[COL_XS:D_CONV, :] = w_ref[SRC_XBC:SRC_DT, :].astype(BF16)
    zeros = jnp.zeros((V7X_LANES - N_REPLICAS * HEADS_PER_GROUP, o_ref.shape[1]), F32)
    for g in range(N_GROUPS):
        row0 = COL_DT + g * V7X_LANES
        src0 = SRC_DT + g * HEADS_PER_GROUP
        heads = w_ref[src0:src0 + HEADS_PER_GROUP, :]
        block = jnp.concatenate([heads] * N_REPLICAS + [zeros], axis=0)
        o_ref[row0:row0 + V7X_LANES, :] = block.astype(BF16)


def _w_in_prep(w_t, *, tk):
    n, k = w_t.shape
    assert n == D_IN_PROJ
    return pl.pallas_call(
        _w_in_prep_kernel,
        out_shape=jax.ShapeDtypeStruct((D_PROJ, k), BF16),
        grid=(k // tk,),
        in_specs=[pl.BlockSpec((n, tk), lambda i: (0, i))],
        out_specs=pl.BlockSpec((D_PROJ, tk), lambda i: (0, i)),
        compiler_params=pltpu.CompilerParams(
            dimension_semantics=("parallel",),
            vmem_limit_bytes=V7X_VMEM_LIMIT_BYTES),
        name="w_in_prep",
    )(w_t)


def _side_cast_specs(weights, n_slabs, step_of):
    specs = []
    for w in weights:
        rows = w.shape[0] // n_slabs
        assert rows * n_slabs == w.shape[0] and rows % (2 * V7X_SUBLANES) == 0
        specs.append(pl.BlockSpec(
            (rows, w.shape[1]), lambda *g: (jnp.minimum(step_of(*g), n_slabs - 1), 0)))
    return specs


CONV_CHUNK = 256


def _conv_silu(r, halo, w_ref, b_ref, cols):
    ext = jnp.concatenate([halo, r], axis=0)
    acc = r * w_ref[CONV_WIDTH - 1:CONV_WIDTH, cols] + b_ref[:, cols]
    for back in range(1, CONV_WIDTH):
        tap = w_ref[CONV_WIDTH - 1 - back:CONV_WIDTH - back, cols]
        acc = acc + pltpu.roll(ext, back, axis=0)[HALO_ROWS:] * tap
    return _silu(acc)


def _in_proj_kernel(x_ref, nw_ref, w_ref, cw_ref, cb_ref, *rest, n_conv_blocks, n_side):
    side_in, rest = rest[:n_side], rest[n_side:]
    o_ref, side_out = rest[0], rest[1:1 + n_side]
    hn_ref, halo_ref, raw_ref = rest[1 + n_side:]
    i = pl.program_id(0)
    j = pl.program_id(1)
    tm, tn = o_ref.shape

    for src_ref, dst_ref in zip(side_in, side_out):
        dst_ref[...] = src_ref[...].astype(BF16)

    @pl.when(j == 0)
    def _():
        hn_ref[...] = _rms_normalize(x_ref[...], nw_ref[...]).astype(BF16)

    @pl.when((i == 0) & (j == 0))
    def _():
        halo_ref[...] = jnp.zeros_like(halo_ref)

    @pl.when(j < n_conv_blocks)
    def _():
        n_pieces = tn // CONV_CHUNK
        piece = lambda c: slice(c * CONV_CHUNK, (c + 1) * CONV_CHUNK)
        park = jnp.minimum(j, 0)
        raw_ref[park] = _dot_nt(hn_ref[...], w_ref[piece(0), :])
        for c in range(n_pieces):
            cols = piece(c)
            r = raw_ref[park]
            o_ref[:, cols] = _conv_silu(r, halo_ref[j, :, cols], cw_ref, cb_ref, cols)
            halo_ref[j, :, cols] = r[tm - HALO_ROWS:]
            if c + 1 < n_pieces:
                raw_ref[park] = _dot_nt(hn_ref[...], w_ref[piece(c + 1), :])

    @pl.when(j >= n_conv_blocks)
    def _():
        o_ref[...] = _dot_nt(hn_ref[...], w_ref[...])


def _in_proj(x, nw, w_t, conv_w, conv_b, side_weights, *, tm, tn, side_slabs):
    m, k = x.shape
    n = w_t.shape[0]
    assert D_CONV % tn == 0 and tn % CONV_CHUNK == 0
    n_conv_blocks = D_CONV // tn
    n_col_blocks = n // tn
    assert (m // tm) * n_col_blocks >= side_slabs
    conv_block = lambda i, j: (0, jnp.minimum(j, n_conv_blocks - 1))
    side_specs = _side_cast_specs(side_weights, side_slabs, lambda i, j: i * n_col_blocks + j)
    return pl.pallas_call(
        functools.partial(_in_proj_kernel, n_conv_blocks=n_conv_blocks, n_side=len(side_weights)),
        out_shape=(jax.ShapeDtypeStruct((m, n), F32),
                   *[jax.ShapeDtypeStruct(w.shape, BF16) for w in side_weights]),
        grid=(m // tm, n_col_blocks),
        in_specs=[
            pl.BlockSpec((tm, k), lambda i, j: (i, 0)),
            pl.BlockSpec((1, k), lambda i, j: (0, 0)),
            pl.BlockSpec((tn, k), lambda i, j: (j, 0)),
            pl.BlockSpec((CONV_WIDTH, tn), conv_block),
            pl.BlockSpec((1, tn), conv_block),
        ] + side_specs,
        out_specs=(pl.BlockSpec((tm, tn), lambda i, j: (i, j)), *side_specs),
        scratch_shapes=[pltpu.VMEM((tm, k), BF16),
                        pltpu.VMEM((n_conv_blocks, HALO_ROWS, tn), F32),
                        pltpu.VMEM((1, tm, CONV_CHUNK), F32)],
        compiler_params=pltpu.CompilerParams(
            dimension_semantics=("arbitrary", "arbitrary"),
            vmem_limit_bytes=V7X_VMEM_LIMIT_BYTES),
        name="in_proj",
    )(x, nw, w_t, conv_w, conv_b, *side_weights)


def _swiglu_kernel(hn_ref, wg_ref, wu_ref, *rest):
    n_side = (len(rest) - 1) // 2
    side_in, o_ref, side_out = rest[:n_side], rest[n_side], rest[n_side + 1:]
    for src_ref, dst_ref in zip(side_in, side_out):
        dst_ref[...] = src_ref[...].astype(BF16)
    half = o_ref.shape[1] // 2
    for cols in (slice(0, half), slice(half, 2 * half)):
        gate = _dot(hn_ref[...], wg_ref[:, cols])
        up = _dot(hn_ref[...], wu_ref[:, cols])
        o_ref[:, cols] = (_silu(gate) * up).astype(BF16)


def _swiglu(hn, wg, wu, side_weights, *, tm, tn):
    m, k = hn.shape
    n = wg.shape[1]
    n_col_tiles = n // tn
    side_specs = _side_cast_specs(side_weights, (m // tm) * n_col_tiles,
                                  lambda i, j: i * n_col_tiles + j)
    return pl.pallas_call(
        _swiglu_kernel,
        out_shape=(jax.ShapeDtypeStruct((m, n), BF16),
                   *[jax.ShapeDtypeStruct(w.shape, BF16) for w in side_weights]),
        grid=(m // tm, n_col_tiles),
        in_specs=[
            pl.BlockSpec((tm, k), lambda i, j: (i, 0)),
            pl.BlockSpec((k, tn), lambda i, j: (0, j)),
            pl.BlockSpec((k, tn), lambda i, j: (0, j)),
        ] + side_specs,
        out_specs=(pl.BlockSpec((tm, tn), lambda i, j: (i, j)), *side_specs),
        compiler_params=pltpu.CompilerParams(
            dimension_semantics=("arbitrary", "arbitrary"),
            vmem_limit_bytes=V7X_VMEM_LIMIT_BYTES),
        name="ffn_up",
    )(hn, wg, wu, *side_weights)


def _out_proj_kernel(ys_ref, yp_ref, w_ref, x_ref, nw_ref, h_ref, hn_ref):
    k = ys_ref.shape[1]
    h = x_ref[...] + _dot(ys_ref[...], w_ref[:k, :]) + _dot(yp_ref[...], w_ref[k:, :])
    h_ref[...] = h
    hn_ref[...] = _rms_normalize(h, nw_ref[...]).astype(BF16)


def _resident(shape):
    return pl.BlockSpec(shape, lambda i: (0,) * len(shape), pipeline_mode=pl.Buffered(1))


def _out_proj(ys, yp, w, x, nw, *, tm):
    m, k = ys.shape
    n = w.shape[1]
    row_spec = lambda width: pl.BlockSpec((tm, width), lambda i: (i, 0))
    return pl.pallas_call(
        _out_proj_kernel,
        out_shape=(jax.ShapeDtypeStruct((m, n), F32), jax.ShapeDtypeStruct((m, n), BF16)),
        grid=(m // tm,),
        in_specs=[row_spec(k), row_spec(k), _resident(w.shape), row_spec(n), _resident((1, n))],
        out_specs=(row_spec(n), row_spec(n)),
        compiler_params=pltpu.CompilerParams(
            dimension_semantics=("parallel",),
            vmem_limit_bytes=V7X_VMEM_LIMIT_BYTES),
        name="out_proj",
    )(ys, yp, w, x, nw)


def _down_kernel(a_ref, w_ref, h_ref, nw_ref, o_ref, *, final_norm):
    h = h_ref[...] + _dot(a_ref[...], w_ref[...])
    o_ref[...] = _rms_normalize(h, nw_ref[...]) if final_norm else h


def _down_proj(act, w, h, nw, *, tm, final_norm):
    m, k = act.shape
    n = w.shape[1]
    row_spec = lambda width: pl.BlockSpec((tm, width), lambda i: (i, 0))
    return pl.pallas_call(
        functools.partial(_down_kernel, final_norm=final_norm),
        out_shape=jax.ShapeDtypeStruct((m, n), F32),
        grid=(m // tm,),
        in_specs=[row_spec(k), _resident(w.shape), row_spec(n), _resident((1, n))],
        out_specs=row_spec(n),
        compiler_params=pltpu.CompilerParams(
            dimension_semantics=("parallel",),
            vmem_limit_bytes=V7X_VMEM_LIMIT_BYTES),
        name="ffn_down",
    )(act, w, h, nw)


def _pool_tile(i, u_refs, halo_refs, pw_ref, ps_ref, o_ref):
    tl = o_ref.shape[0]
    t = i * tl + lax.broadcasted_iota(jnp.int32, (tl, POOL_GROUP_DIM), 0)
    for g, window in enumerate(POOL_WINDOWS):
        cols = slice(g * POOL_GROUP_DIM, (g + 1) * POOL_GROUP_DIM)
        u = u_refs[g][...]
        halo = jnp.where(i == 0, 0.0, halo_refs[g][...])
        s = jnp.concatenate([halo, u], axis=0)
        shift = 1
        while shift < window:
            s = s + pltpu.roll(s, shift, axis=0)
            shift *= 2
        count = jnp.minimum(t + 1, window).astype(F32)
        pooled = s[POOL_HALO:] / count - u
        y = _dot(pooled.astype(BF16), pw_ref[g].astype(BF16)) * ps_ref[:, cols]
        o_ref[:, cols] = y.astype(BF16)


def _pool_specs(tl, step_of):
    gd = POOL_GROUP_DIM
    assert COL_U % gd == 0 and tl % POOL_HALO == 0
    halo_blocks_per_tile = tl // POOL_HALO
    n_groups = len(POOL_WINDOWS)
    u_specs = [pl.BlockSpec((tl, gd), lambda *g, k=k: (step_of(*g), COL_U // gd + k))
               for k in range(n_groups)]
    halo_specs = [
        pl.BlockSpec((POOL_HALO, gd),
                     lambda *g, k=k: (jnp.maximum(step_of(*g) * halo_blocks_per_tile - 1, 0),
                                      COL_U // gd + k))
        for k in range(n_groups)]
    return u_specs + halo_specs


assert N_REPLICAS * HEADS_PER_GROUP <= V7X_LANES


def _expansion_matrix():
    lane = jnp.arange(V7X_LANES)
    col = jnp.arange(N_EXPANDED * GROUP_WIDTH)
    replica, head = lane // HEADS_PER_GROUP, lane % HEADS_PER_GROUP
    quantity = replica // 3
    hit = ((quantity[:, None] == col[None, :] // GROUP_WIDTH)
           & (head[:, None] == (col[None, :] % GROUP_WIDTH) // HEAD_DIM)
           & (replica[:, None] < N_REPLICAS))
    return hit.astype(BF16)


def _ssd_group(z, xs, bm, cm, dt_raw, dtb, alog, dskip, nw, state_ref, expand, tril, causal):
    q = xs.shape[0]
    x = dt_raw + dtb
    dt = jnp.maximum(x, 0.0) + jnp.log1p(jnp.exp(-jnp.abs(x)))
    a = -jnp.exp(alog)
    a_cum = _dot3_left(tril, dt * a) * LOG2_E
    a_cum_t = a_cum.T
    a_last = a_cum[q - 1:q, :]

    replica = lax.broadcasted_iota(jnp.int32, (1, V7X_LANES), 1) // HEADS_PER_GROUP
    packed = jnp.where(replica < 3, dt,
                       jnp.where(replica < 6, jnp.exp2(a_cum), jnp.exp2(a_last - a_cum)))
    hi = packed.astype(BF16)
    r1 = packed - hi.astype(F32)
    r2 = r1 - r1.astype(BF16).astype(F32)
    level = replica % 3
    split = jnp.where(level == 0, packed, jnp.where(level == 1, r1, r2)).astype(BF16)
    expanded = _dot(split, expand)
    dt_b = expanded[:, :GROUP_WIDTH]
    decay_in_b = expanded[:, GROUP_WIDTH:2 * GROUP_WIDTH]
    decay_out_b = expanded[:, 2 * GROUP_WIDTH:]

    xdt = xs * dt_b
    bm_bf = bm.astype(BF16)
    cm_bf = cm.astype(BF16)
    cb = _dot_nt(cm_bf, bm_bf)

    xdt_bf = xdt.astype(BF16)
    lane = lax.broadcasted_iota(jnp.int32, (1, V7X_LANES), 1)
    keep = [jnp.where((lane // HEAD_DIM) == side, 1.0, 0.0).astype(BF16) for side in range(2)]
    pairs = []
    for pair in range(HEADS_PER_GROUP // 2):
        xdt_pair = xdt_bf[:, pair * V7X_LANES:(pair + 1) * V7X_LANES]
        acc = jnp.zeros((q, V7X_LANES), F32)
        for side in range(2):
            r = 2 * pair + side
            seg = a_cum[:, r:r + 1] - a_cum_t[r:r + 1, :]
            decay = jnp.exp2(jnp.where(causal, seg, -jnp.inf))
            m_r = (cb * decay).astype(BF16)
            acc = acc + _dot(m_r, xdt_pair * keep[side])
        pairs.append(acc)
    y_diag = jnp.concatenate(pairs, axis=-1)

    state = state_ref[...]
    y_off = _dot(cm_bf, state.astype(BF16)) * decay_in_b
    xw = (xdt * decay_out_b).astype(BF16)
    new_state = lax.dot_general(bm_bf, xw, (((0,), (0,)), ((), ())), preferred_element_type=F32)
    state_ref[...] = state * decay_in_b[q - 1:q, :] + new_state

    y = y_diag + y_off + dskip * xs
    g = y * _silu(z)
    return _rms_normalize(g, nw).astype(BF16)


def _mixers_kernel(*refs, groups, n_side):
    refs = list(refs)
    take = lambda n: [refs.pop(0) for _ in range(n)]
    n_pool = len(POOL_WINDOWS)
    z_refs = take(groups)
    xs_ref, b_ref, c_ref, dt_ref, dtb_ref, alog_ref, dskip_ref, nw_ref, expand_ref = take(9)
    u_refs, halo_refs = take(n_pool), take(n_pool)
    pw_ref, ps_ref = take(2)
    side_in = take(n_side)
    o_ref, yp_ref = take(2)
    side_out = take(n_side)
    state_ref, = refs
    q = SSD_CHUNK

    _pool_tile(pl.program_id(1), u_refs, halo_refs, pw_ref, ps_ref, yp_ref)

    for src_ref, dst_ref in zip(side_in, side_out):
        dst_ref[...] = src_ref[...].astype(BF16)

    @pl.when(pl.program_id(1) == 0)
    def _():
        state_ref[...] = jnp.zeros_like(state_ref)

    row_i = lax.broadcasted_iota(jnp.int32, (q, q), 0)
    col_i = lax.broadcasted_iota(jnp.int32, (q, q), 1)
    causal = row_i >= col_i
    tril = jnp.where(causal, 1.0, 0.0).astype(BF16)
    expand = expand_ref[...]
    for g in range(groups):
        wide = slice(g * GROUP_WIDTH, (g + 1) * GROUP_WIDTH)
        narrow = slice(g * D_STATE, (g + 1) * D_STATE)
        o_ref[:, wide] = _ssd_group(
            z_refs[g][...], xs_ref[:, wide], b_ref[:, narrow], c_ref[:, narrow], dt_ref[:, narrow],
            dtb_ref[:, narrow], alog_ref[:, narrow], dskip_ref[:, wide], nw_ref[:, wide],
            state_ref.at[g], expand, tril, causal)


def _mixers(proj, dtb_c, alog_c, dskip_full, norm_w, pool_w, pool_scale, side_weights, *, groups):
    m = proj.shape[0]
    q = SSD_CHUNK
    gw = groups * GROUP_WIDTH
    ds = groups * D_STATE
    assert V7X_LANES == D_STATE and N_GROUPS % groups == 0

    def col(start, width):
        assert start % width == 0
        return start // width

    row_spec = lambda width, block0: pl.BlockSpec((q, width), lambda g, c: (c, block0 + g))
    par_spec = lambda width: pl.BlockSpec((1, width), lambda g, c: (0, g))
    z_specs = [
        pl.BlockSpec((q, GROUP_WIDTH),
                     lambda g, c, k=k: (c, col(COL_Z, GROUP_WIDTH) + g * groups + k))
        for k in range(groups)]
    expand = _expansion_matrix()
    assert groups == N_GROUPS
    side_specs = _side_cast_specs(side_weights, m // q, lambda g, c: c)
    return pl.pallas_call(
        functools.partial(_mixers_kernel, groups=groups, n_side=len(side_weights)),
        out_shape=(jax.ShapeDtypeStruct((m, D_SSM), BF16), jax.ShapeDtypeStruct((m, D_POOL), BF16),
                   *[jax.ShapeDtypeStruct(w.shape, BF16) for w in side_weights]),
        grid=(N_GROUPS // groups, m // q),
        in_specs=z_specs + [
            row_spec(gw, col(COL_XS, gw)),
            row_spec(ds, col(COL_B, ds)),
            row_spec(ds, col(COL_C, ds)),
            row_spec(ds, col(COL_DT, ds)),
            par_spec(ds),
            par_spec(ds),
            par_spec(gw),
            par_spec(gw),
            pl.BlockSpec(expand.shape, lambda g, c: (0, 0)),
        ] + _pool_specs(q, lambda g, c: c) + [
            pl.BlockSpec(pool_w.shape, lambda g, c: (0, 0, 0)),
            pl.BlockSpec(pool_scale.shape, lambda g, c: (0, 0)),
        ] + side_specs,
        out_specs=(pl.BlockSpec((q, gw), lambda g, c: (c, g)),
                   pl.BlockSpec((q, D_POOL), lambda g, c: (c, 0)), *side_specs),
        scratch_shapes=[pltpu.VMEM((groups, D_STATE, GROUP_WIDTH), F32)],
        compiler_params=pltpu.CompilerParams(
            dimension_semantics=("parallel", "arbitrary"),
            vmem_limit_bytes=V7X_VMEM_LIMIT_BYTES),
        name="mixers",
    )(*([proj] * (groups + 4)), dtb_c, alog_c, dskip_full, norm_w, expand,
      *([proj] * (2 * len(POOL_WINDOWS))), pool_w, pool_scale, *side_weights)


def _pad_heads_to_lanes(v):
    v = jnp.tile(v.reshape(N_GROUPS, HEADS_PER_GROUP), (1, N_REPLICAS))
    v = jnp.pad(v, ((0, 0), (0, V7X_LANES - N_REPLICAS * HEADS_PER_GROUP)))
    return v.reshape(1, N_GROUPS * V7X_LANES)


def kernel(x, attn_norm_w, w_in, conv_w, conv_b, dt_bias, a_log, d_skip, ssd_norm_w, pool_w,
           pool_scale, w_out, ffn_norm_w, w_gate, w_up, w_down, final_norm_w):
    bsz, seqlen, d_model = x.shape
    depth = w_in.shape[0]
    h = x.reshape(bsz * seqlen, d_model)
    assert bsz == 1 and seqlen % 2048 == 0 and d_model == D_MODEL

    for i in range(depth):
        w_cat = _w_in_prep(w_in[i].T, tk=W_PREP_COLS)
        pool_w_rows = pool_w[i].reshape(-1, POOL_GROUP_DIM)
        proj, w_out_bf, pool_w_bf = _in_proj(
            h, attn_norm_w[i][None, :], w_cat, conv_w[i], conv_b[i][None, :],
            (w_out[i], pool_w_rows), tm=IN_PROJ_TILE[0], tn=IN_PROJ_TILE[1],
            side_slabs=IN_PROJ_SIDE_SLABS)

        y_ssd, y_pool, w_gate_bf, w_up_bf = _mixers(
            proj, _pad_heads_to_lanes(dt_bias[i]), _pad_heads_to_lanes(a_log[i]),
            jnp.repeat(d_skip[i], HEAD_DIM)[None, :], ssd_norm_w[i][None, :],
            pool_w_bf.reshape(pool_w[i].shape), pool_scale[i][None, :], (w_gate[i], w_up[i]),
            groups=N_GROUPS)

        h, hn = _out_proj(y_ssd, y_pool, w_out_bf, h, ffn_norm_w[i][None, :], tm=RESIDENT_ROW_TILE)

        act, w_down_bf = _swiglu(hn, w_gate_bf, w_up_bf, (w_down[i],),
                                 tm=FFN_UP_TILE[0], tn=FFN_UP_TILE[1])
        last = i == depth - 1
        h = _down_proj(act, w_down_bf, h, final_norm_w[None, :], tm=RESIDENT_ROW_TILE,
                       final_norm=last)

    return h.reshape(bsz, seqlen, d_model)
```

```python
import functools

import jax
import jax.numpy as jnp
from jax import lax
from jax.experimental import pallas as pl
from jax.experimental.pallas import tpu as pltpu

F32 = jnp.float32
BF16 = jnp.bfloat16

NORM_EPS = 1e-5
LOG2_E = 1.4426950408889634
D_MODEL = 2048
D_SSM = 2048
HEAD_DIM = 64
N_HEADS = D_SSM // HEAD_DIM
N_GROUPS = 4
HEADS_PER_GROUP = N_HEADS // N_GROUPS
GROUP_WIDTH = D_SSM // N_GROUPS
D_STATE = 128
CONV_WIDTH = 4
D_POOL = 2048
POOL_WINDOWS = (2, 4, 8, 16)
POOL_GROUP_DIM = D_POOL // len(POOL_WINDOWS)

V7X_LANES = 128
V7X_SUBLANES = 8
V7X_VMEM_LIMIT_BYTES = 60 * 1024 * 1024

W_PREP_COLS = 256
IN_PROJ_TILE = (1024, 1536)
IN_PROJ_SIDE_SLABS = 32
RESIDENT_ROW_TILE = 512
FFN_UP_TILE = (2048, 512)
FFN_UP_CHUNK_ROWS = 256

COL_XS = 0
COL_B = COL_XS + D_SSM
COL_C = COL_B + N_GROUPS * D_STATE
D_CONV = COL_C + N_GROUPS * D_STATE
COL_Z = D_CONV
COL_U = COL_Z + D_SSM
COL_DT = COL_U + D_POOL
D_PROJ = COL_DT + N_GROUPS * V7X_LANES

SSD_CHUNK = 256
N_EXPANDED = 3
N_REPLICAS = 3 * N_EXPANDED
HALO_ROWS = V7X_SUBLANES
POOL_HALO = 16


def _sigmoid(x):
    return 1.0 / (1.0 + jnp.exp(-x))


def _silu(x):
    return x * _sigmoid(x)


def _rms_normalize(x, w):
    ms = jnp.mean(x * x, axis=-1, keepdims=True)
    return x * lax.rsqrt(ms + NORM_EPS) * w


def _split3(x):
    hi = x.astype(BF16)
    r1 = x - hi.astype(F32)
    mid = r1.astype(BF16)
    lo = (r1 - mid.astype(F32)).astype(BF16)
    return hi, mid, lo


def _dot(a, b):
    return jnp.dot(a, b, preferred_element_type=F32)


def _dot_nt(a, b):
    return lax.dot_general(a, b, (((1,), (1,)), ((), ())), preferred_element_type=F32)


def _dot3_left(lhs_bf16, x):
    hi, mid, lo = _split3(x)
    return _dot(lhs_bf16, hi) + _dot(lhs_bf16, mid) + _dot(lhs_bf16, lo)


SRC_XBC = D_SSM
SRC_DT = SRC_XBC + D_SSM + 2 * N_GROUPS * D_STATE
SRC_U = SRC_DT + N_HEADS
D_IN_PROJ = SRC_U + D_POOL


def _w_in_prep_kernel(w_ref, o_ref):
    o_ref[COL_Z:COL_Z + D_SSM, :] = w_ref[:D_SSM, :].astype(BF16)
    o_ref[COL_U:COL_U + D_POOL, :] = w_ref[SRC_U:SRC_U + D_POOL, :].astype(BF16)
    o_ref[COL_XS:D_CONV, :] = w_ref[SRC_XBC:SRC_DT, :].astype(BF16)
    zeros = jnp.zeros((V7X_LANES - N_REPLICAS * HEADS_PER_GROUP, o_ref.shape[1]), F32)
    for g in range(N_GROUPS):
        row0 = COL_DT + g * V7X_LANES
        src0 = SRC_DT + g * HEADS_PER_GROUP
        heads = w_ref[src0:src0 + HEADS_PER_GROUP, :]
        block = jnp.concatenate([heads] * N_REPLICAS + [zeros], axis=0)
        o_ref[row0:row0 + V7X_LANES, :] = block.astype(BF16)


def _w_in_prep(w_t, *, tk):
    n, k = w_t.shape
    assert n == D_IN_PROJ
    return pl.pallas_call(
        _w_in_prep_kernel,
        out_shape=jax.ShapeDtypeStruct((D_PROJ, k), BF16),
        grid=(k // tk,),
        in_specs=[pl.BlockSpec((n, tk), lambda i: (0, i))],
        out_specs=pl.BlockSpec((D_PROJ, tk), lambda i: (0, i)),
        compiler_params=pltpu.CompilerParams(
            dimension_semantics=("parallel",),
            vmem_limit_bytes=V7X_VMEM_LIMIT_BYTES),
        name="w_in_prep",
    )(w_t)


def _side_cast_specs(weights, n_slabs, step_of):
    specs = []
    for w in weights:
        rows = w.shape[0] // n_slabs
        assert rows * n_slabs == w.shape[0] and rows % (2 * V7X_SUBLANES) == 0
        specs.append(pl.BlockSpec(
            (rows, w.shape[1]), lambda *g: (jnp.minimum(step_of(*g), n_slabs - 1), 0)))
    return specs


CONV_CHUNK = 256


def _conv_silu(r, halo, w_ref, b_ref, cols):
    ext = jnp.concatenate([halo, r], axis=0)
    acc = r * w_ref[CONV_WIDTH - 1:CONV_WIDTH, cols] + b_ref[:, cols]
    for back in range(1, CONV_WIDTH):
        tap = w_ref[CONV_WIDTH - 1 - back:CONV_WIDTH - back, cols]
        acc = acc + pltpu.roll(ext, back, axis=0)[HALO_ROWS:] * tap
    return _silu(acc)


def _in_proj_kernel(x_ref, nw_ref, w_ref, cw_ref, cb_ref, *rest, n_conv_blocks, n_side):
    side_in, rest = rest[:n_side], rest[n_side:]
    o_ref, side_out = rest[0], rest[1:1 + n_side]
    hn_ref, halo_ref, raw_ref = rest[1 + n_side:]
    i = pl.program_id(0)
    j = pl.program_id(1)
    tm, tn = o_ref.shape

    for src_ref, dst_ref in zip(side_in, side_out):
        dst_ref[...] = src_ref[...].astype(BF16)

    @pl.when(j == 0)
    def _():
        hn_ref[...] = _rms_normalize(x_ref[...], nw_ref[...]).astype(BF16)

    @pl.when((i == 0) & (j == 0))
    def _():
        halo_ref[...] = jnp.zeros_like(halo_ref)

    @pl.when(j < n_conv_blocks)
    def _():
        n_pieces = tn // CONV_CHUNK
        piece = lambda c: slice(c * CONV_CHUNK, (c + 1) * CONV_CHUNK)
        park = jnp.minimum(j, 0)
        raw_ref[park] = _dot_nt(hn_ref[...], w_ref[piece(0), :])
        for c in range(n_pieces):
            cols = piece(c)
            r = raw_ref[park]
            o_ref[:, cols] = _conv_silu(r, halo_ref[j, :, cols], cw_ref, cb_ref, cols)
            halo_ref[j, :, cols] = r[tm - HALO_ROWS:]
            if c + 1 < n_pieces:
                raw_ref[park] = _dot_nt(hn_ref[...], w_ref[piece(c + 1), :])

    @pl.when(j >= n_conv_blocks)
    def _():
        o_ref[...] = _dot_nt(hn_ref[...], w_ref[...])


def _in_proj(x, nw, w_t, conv_w, conv_b, side_weights, *, tm, tn, side_slabs):
    m, k = x.shape
    n = w_t.shape[0]
    assert D_CONV % tn == 0 and tn % CONV_CHUNK == 0
    n_conv_blocks = D_CONV // tn
    n_col_blocks = n // tn
    assert (m // tm) * n_col_blocks >= side_slabs
    conv_block = lambda i, j: (0, jnp.minimum(j, n_conv_blocks - 1))
    side_specs = _side_cast_specs(side_weights, side_slabs, lambda i, j: i * n_col_blocks + j)
    return pl.pallas_call(
        functools.partial(_in_proj_kernel, n_conv_blocks=n_conv_blocks, n_side=len(side_weights)),
        out_shape=(jax.ShapeDtypeStruct((m, n), F32),
                   *[jax.ShapeDtypeStruct(w.shape, BF16) for w in side_weights]),
        grid=(m // tm, n_col_blocks),
        in_specs=[
            pl.BlockSpec((tm, k), lambda i, j: (i, 0)),
            pl.BlockSpec((1, k), lambda i, j: (0, 0)),
            pl.BlockSpec((tn, k), lambda i, j: (j, 0)),
            pl.BlockSpec((CONV_WIDTH, tn), conv_block),
            pl.BlockSpec((1, tn), conv_block),
        ] + side_specs,
        out_specs=(pl.BlockSpec((tm, tn), lambda i, j: (i, j)), *side_specs),
        scratch_shapes=[pltpu.VMEM((tm, k), BF16),
                        pltpu.VMEM((n_conv_blocks, HALO_ROWS, tn), F32),
                        pltpu.VMEM((1, tm, CONV_CHUNK), F32)],
        compiler_params=pltpu.CompilerParams(
            dimension_semantics=("arbitrary", "arbitrary"),
            vmem_limit_bytes=V7X_VMEM_LIMIT_BYTES),
        name="in_proj",
    )(x, nw, w_t, conv_w, conv_b, *side_weights)


def _swiglu_kernel(hn_ref, wg_ref, wu_ref, *rest):
    n_side = (len(rest) - 1) // 2
    side_in, o_ref, side_out = rest[:n_side], rest[n_side], rest[n_side + 1:]
    for src_ref, dst_ref in zip(side_in, side_out):
        dst_ref[...] = src_ref[...].astype(BF16)
    for r in range(o_ref.shape[0] // FFN_UP_CHUNK_ROWS):
        rows = slice(r * FFN_UP_CHUNK_ROWS, (r + 1) * FFN_UP_CHUNK_ROWS)
        gate = _dot(hn_ref[rows, :], wg_ref[...])
        up = _dot(hn_ref[rows, :], wu_ref[...])
        o_ref[rows, :] = (_silu(gate) * up).astype(BF16)


def _swiglu(hn, wg, wu, side_weights, *, tm, tn):
    m, k = hn.shape
    n = wg.shape[1]
    n_col_tiles = n // tn
    side_specs = _side_cast_specs(side_weights, (m // tm) * n_col_tiles,
                                  lambda i, j: i * n_col_tiles + j)
    return pl.pallas_call(
        _swiglu_kernel,
        out_shape=(jax.ShapeDtypeStruct((m, n), BF16),
                   *[jax.ShapeDtypeStruct(w.shape, BF16) for w in side_weights]),
        grid=(m // tm, n_col_tiles),
        in_specs=[
            pl.BlockSpec((tm, k), lambda i, j: (i, 0)),
            pl.BlockSpec((k, tn), lambda i, j: (0, j)),
            pl.BlockSpec((k, tn), lambda i, j: (0, j)),
        ] + side_specs,
        out_specs=(pl.BlockSpec((tm, tn), lambda i, j: (i, j)), *side_specs),
        compiler_params=pltpu.CompilerParams(
            dimension_semantics=("arbitrary", "arbitrary"),
            vmem_limit_bytes=V7X_VMEM_LIMIT_BYTES),
        name="ffn_up",
    )(hn, wg, wu, *side_weights)


def _out_proj_kernel(ys_ref, yp_ref, w_ref, x_ref, nw_ref, h_ref, hn_ref):
    k = ys_ref.shape[1]
    h = x_ref[...] + _dot(ys_ref[...], w_ref[:k, :]) + _dot(yp_ref[...], w_ref[k:, :])
    h_ref[...] = h
    hn_ref[...] = _rms_normalize(h, nw_ref[...]).astype(BF16)


def _resident(shape):
    return pl.BlockSpec(shape, lambda i: (0,) * len(shape), pipeline_mode=pl.Buffered(1))


def _out_proj(ys, yp, w, x, nw, *, tm):
    m, k = ys.shape
    n = w.shape[1]
    row_spec = lambda width: pl.BlockSpec((tm, width), lambda i: (i, 0))
    return pl.pallas_call(
        _out_proj_kernel,
        out_shape=(jax.ShapeDtypeStruct((m, n), F32), jax.ShapeDtypeStruct((m, n), BF16)),
        grid=(m // tm,),
        in_specs=[row_spec(k), row_spec(k), _resident(w.shape), row_spec(n), _resident((1, n))],
        out_specs=(row_spec(n), row_spec(n)),
        compiler_params=pltpu.CompilerParams(
            dimension_semantics=("parallel",),
            vmem_limit_bytes=V7X_VMEM_LIMIT_BYTES),
        name="out_proj",
    )(ys, yp, w, x, nw)


def _down_kernel(a_ref, w_ref, h_ref, nw_ref, o_ref, *, final_norm):
    h = h_ref[...] + _dot(a_ref[...], w_ref[...])
    o_ref[...] = _rms_normalize(h, nw_ref[...]) if final_norm else h


def _down_proj(act, w, h, nw, *, tm, final_norm):
    m, k = act.shape
    n = w.shape[1]
    row_spec = lambda width: pl.BlockSpec((tm, width), lambda i: (i, 0))
    return pl.pallas_call(
        functools.partial(_down_kernel, final_norm=final_norm),
        out_shape=jax.ShapeDtypeStruct((m, n), F32),
        grid=(m // tm,),
        in_specs=[row_spec(k), _resident(w.shape), row_spec(n), _resident((1, n))],
        out_specs=row_spec(n),
        compiler_params=pltpu.CompilerParams(
            dimension_semantics=("parallel",),
            vmem_limit_bytes=V7X_VMEM_LIMIT_BYTES),
        name="ffn_down",
    )(act, w, h, nw)


def _pool_tile(i, u_refs, halo_refs, pw_ref, ps_ref, o_ref):
    tl = o_ref.shape[0]
    t = i * tl + lax.broadcasted_iota(jnp.int32, (tl, POOL_GROUP_DIM), 0)
    for g, window in enumerate(POOL_WINDOWS):
        cols = slice(g * POOL_GROUP_DIM, (g + 1) * POOL_GROUP_DIM)
        u = u_refs[g][...]
        halo = jnp.where(i == 0, 0.0, halo_refs[g][...])
        s = jnp.concatenate([halo, u], axis=0)
        shift = 1
        while shift < window:
            s = s + pltpu.roll(s, shift, axis=0)
            shift *= 2
        count = jnp.minimum(t + 1, window).astype(F32)
        pooled = s[POOL_HALO:] / count - u
        y = _dot(pooled.astype(BF16), pw_ref[g].astype(BF16)) * ps_ref[:, cols]
        o_ref[:, cols] = y.astype(BF16)


def _pool_specs(tl, step_of):
    gd = POOL_GROUP_DIM
    assert COL_U % gd == 0 and tl % POOL_HALO == 0
    halo_blocks_per_tile = tl // POOL_HALO
    n_groups = len(POOL_WINDOWS)
    u_specs = [pl.BlockSpec((tl, gd), lambda *g, k=k: (step_of(*g), COL_U // gd + k))
               for k in range(n_groups)]
    halo_specs = [
        pl.BlockSpec((POOL_HALO, gd),
                     lambda *g, k=k: (jnp.maximum(step_of(*g) * halo_blocks_per_tile - 1, 0),
                                      COL_U // gd + k))
        for k in range(n_groups)]
    return u_specs + halo_specs


assert N_REPLICAS * HEADS_PER_GROUP <= V7X_LANES


def _expansion_matrix():
    lane = jnp.arange(V7X_LANES)
    col = jnp.arange(N_EXPANDED * GROUP_WIDTH)
    replica, head = lane // HEADS_PER_GROUP, lane % HEADS_PER_GROUP
    quantity = replica // 3
    hit = ((quantity[:, None] == col[None, :] // GROUP_WIDTH)
           & (head[:, None] == (col[None, :] % GROUP_WIDTH) // HEAD_DIM)
           & (replica[:, None] < N_REPLICAS))
    return hit.astype(BF16)


def _ssd_group(z, xs, bm, cm, dt_raw, dtb, alog, dskip, nw, state_ref, expand, tril, causal):
    q = xs.shape[0]
    x = dt_raw + dtb
    dt = jnp.maximum(x, 0.0) + jnp.log1p(jnp.exp(-jnp.abs(x)))
    a = -jnp.exp(alog)
    a_cum = _dot3_left(tril, dt * a) * LOG2_E
    a_cum_t = a_cum.T
    a_last = a_cum[q - 1:q, :]

    replica = lax.broadcasted_iota(jnp.int32, (1, V7X_LANES), 1) // HEADS_PER_GROUP
    packed = jnp.where(replica < 3, dt,
                       jnp.where(replica < 6, jnp.exp2(a_cum), jnp.exp2(a_last - a_cum)))
    hi = packed.astype(BF16)
    r1 = packed - hi.astype(F32)
    r2 = r1 - r1.astype(BF16).astype(F32)
    level = replica % 3
    split = jnp.where(level == 0, packed, jnp.where(level == 1, r1, r2)).astype(BF16)
    expanded = _dot(split, expand)
    dt_b = expanded[:, :GROUP_WIDTH]
    decay_in_b = expanded[:, GROUP_WIDTH:2 * GROUP_WIDTH]
    decay_out_b = expanded[:, 2 * GROUP_WIDTH:]

    xdt = xs * dt_b
    bm_bf = bm.astype(BF16)
    cm_bf = cm.astype(BF16)
    cb = _dot_nt(cm_bf, bm_bf)

    xdt_bf = xdt.astype(BF16)
    lane = lax.broadcasted_iota(jnp.int32, (1, V7X_LANES), 1)
    keep = [jnp.where((lane // HEAD_DIM) == side, 1.0, 0.0).astype(BF16) for side in range(2)]
    pairs = []
    for pair in range(HEADS_PER_GROUP // 2):
        xdt_pair = xdt_bf[:, pair * V7X_LANES:(pair + 1) * V7X_LANES]
        acc = jnp.zeros((q, V7X_LANES), F32)
        for side in range(2):
            r = 2 * pair + side
            seg = a_cum[:, r:r + 1] - a_cum_t[r:r + 1, :]
            decay = jnp.exp2(jnp.where(causal, seg, -jnp.inf))
            m_r = (cb * decay).astype(BF16)
            acc = acc + _dot(m_r, xdt_pair * keep[side])
        pairs.append(acc)
    y_diag = jnp.concatenate(pairs, axis=-1)

    state = state_ref[...]
    y_off = _dot(cm_bf, state.astype(BF16)) * decay_in_b
    xw = (xdt * decay_out_b).astype(BF16)
    new_state = lax.dot_general(bm_bf, xw, (((0,), (0,)), ((), ())), preferred_element_type=F32)
    state_ref[...] = state * decay_in_b[q - 1:q, :] + new_state

    y = y_diag + y_off + dskip * xs
    g = y * _silu(z)
    return _rms_normalize(g, nw).astype(BF16)


def _mixers_kernel(*refs, groups, n_side):
    refs = list(refs)
    take = lambda n: [refs.pop(0) for _ in range(n)]
    n_pool = len(POOL_WINDOWS)
    z_refs = take(groups)
    xs_ref, b_ref, c_ref, dt_ref, dtb_ref, alog_ref, dskip_ref, nw_ref, expand_ref = take(9)
    u_refs, halo_refs = take(n_pool), take(n_pool)
    pw_ref, ps_ref = take(2)
    side_in = take(n_side)
    o_ref, yp_ref = take(2)
    side_out = take(n_side)
    state_ref, = refs
    q = SSD_CHUNK

    _pool_tile(pl.program_id(1), u_refs, halo_refs, pw_ref, ps_ref, yp_ref)

    for src_ref, dst_ref in zip(side_in, side_out):
        dst_ref[...] = src_ref[...].astype(BF16)

    @pl.when(pl.program_id(1) == 0)
    def _():
        state_ref[...] = jnp.zeros_like(state_ref)

    row_i = lax.broadcasted_iota(jnp.int32, (q, q), 0)
    col_i = lax.broadcasted_iota(jnp.int32, (q, q), 1)
    causal = row_i >= col_i
    tril = jnp.where(causal, 1.0, 0.0).astype(BF16)
    expand = expand_ref[...]
    for g in range(groups):
        wide = slice(g * GROUP_WIDTH, (g + 1) * GROUP_WIDTH)
        narrow = slice(g * D_STATE, (g + 1) * D_STATE)
        o_ref[:, wide] = _ssd_group(
            z_refs[g][...], xs_ref[:, wide], b_ref[:, narrow], c_ref[:, narrow], dt_ref[:, narrow],
            dtb_ref[:, narrow], alog_ref[:, narrow], dskip_ref[:, wide], nw_ref[:, wide],
            state_ref.at[g], expand, tril, causal)


def _mixers(proj, dtb_c, alog_c, dskip_full, norm_w, pool_w, pool_scale, side_weights, *, groups):
    m = proj.shape[0]
    q = SSD_CHUNK
    gw = groups * GROUP_WIDTH
    ds = groups * D_STATE
    assert V7X_LANES == D_STATE and N_GROUPS % groups == 0

    def col(start, width):
        assert start % width == 0
        return start // width

    row_spec = lambda width, block0: pl.BlockSpec((q, width), lambda g, c: (c, block0 + g))
    par_spec = lambda width: pl.BlockSpec((1, width), lambda g, c: (0, g))
    z_specs = [
        pl.BlockSpec((q, GROUP_WIDTH),
                     lambda g, c, k=k: (c, col(COL_Z, GROUP_WIDTH) + g * groups + k))
        for k in range(groups)]
    expand = _expansion_matrix()
    assert groups == N_GROUPS
    side_specs = _side_cast_specs(side_weights, m // q, lambda g, c: c)
    return pl.pallas_call(
        functools.partial(_mixers_kernel, groups=groups, n_side=len(side_weights)),
        out_shape=(jax.ShapeDtypeStruct((m, D_SSM), BF16), jax.ShapeDtypeStruct((m, D_POOL), BF16),
                   *[jax.ShapeDtypeStruct(w.shape, BF16) for w in side_weights]),
        grid=(N_GROUPS // groups, m // q),
        in_specs=z_specs + [
            row_spec(gw, col(COL_XS, gw)),
            row_spec(ds, col(COL_B, ds)),
            row_spec(ds, col(COL_C, ds)),
            row_spec(ds, col(COL_DT, ds)),
            par_spec(ds),
            par_spec(ds),
            par_spec(gw),
            par_spec(gw),
            pl.BlockSpec(expand.shape, lambda g, c: (0, 0)),
        ] + _pool_specs(q, lambda g, c: c) + [
            pl.BlockSpec(pool_w.shape, lambda g, c: (0, 0, 0)),
            pl.BlockSpec(pool_scale.shape, lambda g, c: (0, 0)),
        ] + side_specs,
        out_specs=(pl.BlockSpec((q, gw), lambda g, c: (c, g)),
                   pl.BlockSpec((q, D_POOL), lambda g, c: (c, 0)), *side_specs),
        scratch_shapes=[pltpu.VMEM((groups, D_STATE, GROUP_WIDTH), F32)],
        compiler_params=pltpu.CompilerParams(
            dimension_semantics=("parallel", "arbitrary"),
            vmem_limit_bytes=V7X_VMEM_LIMIT_BYTES),
        name="mixers",
    )(*([proj] * (groups + 4)), dtb_c, alog_c, dskip_full, norm_w, expand,
      *([proj] * (2 * len(POOL_WINDOWS))), pool_w, pool_scale, *side_weights)


def _pad_heads_to_lanes(v):
    v = jnp.tile(v.reshape(N_GROUPS, HEADS_PER_GROUP), (1, N_REPLICAS))
    v = jnp.pad(v, ((0, 0), (0, V7X_LANES - N_REPLICAS * HEADS_PER_GROUP)))
    return v.reshape(1, N_GROUPS * V7X_LANES)


def kernel(x, attn_norm_w, w_in, conv_w, conv_b, dt_bias, a_log, d_skip, ssd_norm_w, pool_w,
           pool_scale, w_out, ffn_norm_w, w_gate, w_up, w_down, final_norm_w):
    bsz, seqlen, d_model = x.shape
    depth = w_in.shape[0]
    h = x.reshape(bsz * seqlen, d_model)
    assert bsz == 1 and seqlen % 2048 == 0 and d_model == D_MODEL

    for i in range(depth):
        w_cat = _w_in_prep(w_in[i].T, tk=W_PREP_COLS)
        pool_w_rows = pool_w[i].reshape(-1, POOL_GROUP_DIM)
        proj, w_out_bf, pool_w_bf = _in_proj(
            h, attn_norm_w[i][None, :], w_cat, conv_w[i], conv_b[i][None, :],
            (w_out[i], pool_w_rows), tm=IN_PROJ_TILE[0], tn=IN_PROJ_TILE[1],
            side_slabs=IN_PROJ_SIDE_SLABS)

        y_ssd, y_pool, w_gate_bf, w_up_bf = _mixers(
            proj, _pad_heads_to_lanes(dt_bias[i]), _pad_heads_to_lanes(a_log[i]),
            jnp.repeat(d_skip[i], HEAD_DIM)[None, :], ssd_norm_w[i][None, :],
            pool_w_bf.reshape(pool_w[i].shape), pool_scale[i][None, :], (w_gate[i], w_up[i]),
            groups=N_GROUPS)

        h, hn = _out_proj(y_ssd, y_pool, w_out_bf, h, ffn_norm_w[i][None, :], tm=RESIDENT_ROW_TILE)

        act, w_down_bf = _swiglu(hn, w_gate_bf, w_up_bf, (w_down[i],),
                                 tm=FFN_UP_TILE[0], tn=FFN_UP_TILE[1])
        last = i == depth - 1
        h = _down_proj(act, w_down_bf, h, final_norm_w[None, :], tm=RESIDENT_ROW_TILE,
                       final_norm=last)

    return h.reshape(bsz, seqlen, d_model)
```

```python
import functools

import jax
import jax.numpy as jnp
from jax import lax
from jax.experimental import pallas as pl
from jax.experimental.pallas import tpu as pltpu

F32 = jnp.float32
BF16 = jnp.bfloat16

NORM_EPS = 1e-5
LOG2_E = 1.4426950408889634
D_MODEL = 2048
D_SSM = 2048
HEAD_DIM = 64
N_HEADS = D_SSM // HEAD_DIM
N_GROUPS = 4
HEADS_PER_GROUP = N_HEADS // N_GROUPS
GROUP_WIDTH = D_SSM // N_GROUPS
D_STATE = 128
CONV_WIDTH = 4
D_POOL = 2048
POOL_WINDOWS = (2, 4, 8, 16)
POOL_GROUP_DIM = D_POOL // len(POOL_WINDOWS)

V7X_LANES = 128
V7X_SUBLANES = 8
V7X_VMEM_LIMIT_BYTES = 60 * 1024 * 1024

W_PREP_COLS = 256
IN_PROJ_TILE = (1024, 1536)
IN_PROJ_SIDE_SLABS = 32
RESIDENT_ROW_TILE = 512
FFN_UP_TILE = (2048, 512)
MATMUL_CHUNK_ROWS = 256

COL_XS = 0
COL_B = COL_XS + D_SSM
COL_C = COL_B + N_GROUPS * D_STATE
D_CONV = COL_C + N_GROUPS * D_STATE
COL_Z = D_CONV
COL_U = COL_Z + D_SSM
COL_DT = COL_U + D_POOL
D_PROJ = COL_DT + N_GROUPS * V7X_LANES

SSD_CHUNK = 256
N_EXPANDED = 3
N_REPLICAS = 3 * N_EXPANDED
HALO_ROWS = V7X_SUBLANES
POOL_HALO = 16


def _sigmoid(x):
    return 1.0 / (1.0 + jnp.exp(-x))


def _silu(x):
    return x * _sigmoid(x)


def _rms_normalize(x, w):
    ms = jnp.mean(x * x, axis=-1, keepdims=True)
    return x * lax.rsqrt(ms + NORM_EPS) * w


def _split3(x):
    hi = x.astype(BF16)
    r1 = x - hi.astype(F32)
    mid = r1.astype(BF16)
    lo = (r1 - mid.astype(F32)).astype(BF16)
    return hi, mid, lo


def _dot(a, b):
    return jnp.dot(a, b, preferred_element_type=F32)


def _row_chunks(n_rows):
    assert n_rows % MATMUL_CHUNK_ROWS == 0
    return [slice(r, r + MATMUL_CHUNK_ROWS) for r in range(0, n_rows, MATMUL_CHUNK_ROWS)]


def _dot_nt(a, b):
    return lax.dot_general(a, b, (((1,), (1,)), ((), ())), preferred_element_type=F32)


def _dot3_left(lhs_bf16, x):
    hi, mid, lo = _split3(x)
    return _dot(lhs_bf16, hi) + _dot(lhs_bf16, mid) + _dot(lhs_bf16, lo)


SRC_XBC = D_SSM
SRC_DT = SRC_XBC + D_SSM + 2 * N_GROUPS * D_STATE
SRC_U = SRC_DT + N_HEADS
D_IN_PROJ = SRC_U + D_POOL


def _w_in_prep_kernel(w_ref, o_ref):
    o_ref[COL_Z:COL_Z + D_SSM, :] = w_ref[:D_SSM, :].astype(BF16)
    o_ref[COL_U:COL_U + D_POOL, :] = w_ref[SRC_U:SRC_U + D_POOL, :].astype(BF16)
    o_ref[COL_XS:D_CONV, :] = w_ref[SRC_XBC:SRC_DT, :].astype(BF16)
    zeros = jnp.zeros((V7X_LANES - N_REPLICAS * HEADS_PER_GROUP, o_ref.shape[1]), F32)
    for g in range(N_GROUPS):
        row0 = COL_DT + g * V7X_LANES
        src0 = SRC_DT + g * HEADS_PER_GROUP
        heads = w_ref[src0:src0 + HEADS_PER_GROUP, :]
        block = jnp.concatenate([heads] * N_REPLICAS + [zeros], axis=0)
        o_ref[row0:row0 + V7X_LANES, :] = block.astype(BF16)


def _w_in_prep(w_t, *, tk):
    n, k = w_t.shape
    assert n == D_IN_PROJ
    return pl.pallas_call(
        _w_in_prep_kernel,
        out_shape=jax.ShapeDtypeStruct((D_PROJ, k), BF16),
        grid=(k // tk,),
        in_specs=[pl.BlockSpec((n, tk), lambda i: (0, i))],
        out_specs=pl.BlockSpec((D_PROJ, tk), lambda i: (0, i)),
        compiler_params=pltpu.CompilerParams(
            dimension_semantics=("parallel",),
            vmem_limit_bytes=V7X_VMEM_LIMIT_BYTES),
        name="w_in_prep",
    )(w_t)


def _side_cast_specs(weights, n_slabs, step_of):
    specs = []
    for w in weights:
        rows = w.shape[0] // n_slabs
        assert rows * n_slabs == w.shape[0] and rows % (2 * V7X_SUBLANES) == 0
        specs.append(pl.BlockSpec(
            (rows, w.shape[1]), lambda *g: (jnp.minimum(step_of(*g), n_slabs - 1), 0)))
    return specs


CONV_CHUNK = 256


def _conv_silu(r, halo, w_ref, b_ref, cols):
    ext = jnp.concatenate([halo, r], axis=0)
    acc = r * w_ref[CONV_WIDTH - 1:CONV_WIDTH, cols] + b_ref[:, cols]
    for back in range(1, CONV_WIDTH):
        tap = w_ref[CONV_WIDTH - 1 - back:CONV_WIDTH - back, cols]
        acc = acc + pltpu.roll(ext, back, axis=0)[HALO_ROWS:] * tap
    return _silu(acc)


def _in_proj_kernel(x_ref, nw_ref, w_ref, cw_ref, cb_ref, *rest, n_conv_blocks, n_side):
    side_in, rest = rest[:n_side], rest[n_side:]
    o_ref, side_out = rest[0], rest[1:1 + n_side]
    hn_ref, halo_ref, raw_ref = rest[1 + n_side:]
    i = pl.program_id(0)
    j = pl.program_id(1)
    tm, tn = o_ref.shape

    for src_ref, dst_ref in zip(side_in, side_out):
        dst_ref[...] = src_ref[...].astype(BF16)

    @pl.when(j == 0)
    def _():
        hn_ref[...] = _rms_normalize(x_ref[...], nw_ref[...]).astype(BF16)

    @pl.when((i == 0) & (j == 0))
    def _():
        halo_ref[...] = jnp.zeros_like(halo_ref)

    @pl.when(j < n_conv_blocks)
    def _():
        n_pieces = tn // CONV_CHUNK
        piece = lambda c: slice(c * CONV_CHUNK, (c + 1) * CONV_CHUNK)
        park = jnp.minimum(j, 0)
        raw_ref[park] = _dot_nt(hn_ref[...], w_ref[piece(0), :])
        for c in range(n_pieces):
            cols = piece(c)
            r = raw_ref[park]
            o_ref[:, cols] = _conv_silu(r, halo_ref[j, :, cols], cw_ref, cb_ref, cols)
            halo_ref[j, :, cols] = r[tm - HALO_ROWS:]
            if c + 1 < n_pieces:
                raw_ref[park] = _dot_nt(hn_ref[...], w_ref[piece(c + 1), :])

    @pl.when(j >= n_conv_blocks)
    def _():
        o_ref[...] = _dot_nt(hn_ref[...], w_ref[...])


def _in_proj(x, nw, w_t, conv_w, conv_b, side_weights, *, tm, tn, side_slabs):
    m, k = x.shape
    n = w_t.shape[0]
    assert D_CONV % tn == 0 and tn % CONV_CHUNK == 0
    n_conv_blocks = D_CONV // tn
    n_col_blocks = n // tn
    assert (m // tm) * n_col_blocks >= side_slabs
    conv_block = lambda i, j: (0, jnp.minimum(j, n_conv_blocks - 1))
    side_specs = _side_cast_specs(side_weights, side_slabs, lambda i, j: i * n_col_blocks + j)
    return pl.pallas_call(
        functools.partial(_in_proj_kernel, n_conv_blocks=n_conv_blocks, n_side=len(side_weights)),
        out_shape=(jax.ShapeDtypeStruct((m, n), F32),
                   *[jax.ShapeDtypeStruct(w.shape, BF16) for w in side_weights]),
        grid=(m // tm, n_col_blocks),
        in_specs=[
            pl.BlockSpec((tm, k), lambda i, j: (i, 0)),
            pl.BlockSpec((1, k), lambda i, j: (0, 0)),
            pl.BlockSpec((tn, k), lambda i, j: (j, 0)),
            pl.BlockSpec((CONV_WIDTH, tn), conv_block),
            pl.BlockSpec((1, tn), conv_block),
        ] + side_specs,
        out_specs=(pl.BlockSpec((tm, tn), lambda i, j: (i, j)), *side_specs),
        scratch_shapes=[pltpu.VMEM((tm, k), BF16),
                        pltpu.VMEM((n_conv_blocks, HALO_ROWS, tn), F32),
                        pltpu.VMEM((1, tm, CONV_CHUNK), F32)],
        compiler_params=pltpu.CompilerParams(
            dimension_semantics=("arbitrary", "arbitrary"),
            vmem_limit_bytes=V7X_VMEM_LIMIT_BYTES),
        name="in_proj",
    )(x, nw, w_t, conv_w, conv_b, *side_weights)


def _swiglu_kernel(hn_ref, wg_ref, wu_ref, *rest):
    n_side = (len(rest) - 1) // 2
    side_in, o_ref, side_out = rest[:n_side], rest[n_side], rest[n_side + 1:]
    for src_ref, dst_ref in zip(side_in, side_out):
        dst_ref[...] = src_ref[...].astype(BF16)
    for rows in _row_chunks(o_ref.shape[0]):
        gate = _dot(hn_ref[rows, :], wg_ref[...])
        up = _dot(hn_ref[rows, :], wu_ref[...])
        o_ref[rows, :] = (_silu(gate) * up).astype(BF16)


def _swiglu(hn, wg, wu, side_weights, *, tm, tn):
    m, k = hn.shape
    n = wg.shape[1]
    n_col_tiles = n // tn
    side_specs = _side_cast_specs(side_weights, (m // tm) * n_col_tiles,
                                  lambda i, j: i * n_col_tiles + j)
    return pl.pallas_call(
        _swiglu_kernel,
        out_shape=(jax.ShapeDtypeStruct((m, n), BF16),
                   *[jax.ShapeDtypeStruct(w.shape, BF16) for w in side_weights]),
        grid=(m // tm, n_col_tiles),
        in_specs=[
            pl.BlockSpec((tm, k), lambda i, j: (i, 0)),
            pl.BlockSpec((k, tn), lambda i, j: (0, j)),
            pl.BlockSpec((k, tn), lambda i, j: (0, j)),
        ] + side_specs,
        out_specs=(pl.BlockSpec((tm, tn), lambda i, j: (i, j)), *side_specs),
        compiler_params=pltpu.CompilerParams(
            dimension_semantics=("arbitrary", "arbitrary"),
            vmem_limit_bytes=V7X_VMEM_LIMIT_BYTES),
        name="ffn_up",
    )(hn, wg, wu, *side_weights)


def _out_proj_kernel(ys_ref, yp_ref, w_ref, x_ref, nw_ref, h_ref, hn_ref):
    k = ys_ref.shape[1]
    for rows in _row_chunks(h_ref.shape[0]):
        h = x_ref[rows, :] + _dot(ys_ref[rows, :], w_ref[:k, :]) + _dot(yp_ref[rows, :], w_ref[k:, :])
        h_ref[rows, :] = h
        hn_ref[rows, :] = _rms_normalize(h, nw_ref[...]).astype(BF16)


def _resident(shape):
    return pl.BlockSpec(shape, lambda i: (0,) * len(shape), pipeline_mode=pl.Buffered(1))


def _out_proj(ys, yp, w, x, nw, *, tm):
    m, k = ys.shape
    n = w.shape[1]
    row_spec = lambda width: pl.BlockSpec((tm, width), lambda i: (i, 0))
    return pl.pallas_call(
        _out_proj_kernel,
        out_shape=(jax.ShapeDtypeStruct((m, n), F32), jax.ShapeDtypeStruct((m, n), BF16)),
        grid=(m // tm,),
        in_specs=[row_spec(k), row_spec(k), _resident(w.shape), row_spec(n), _resident((1, n))],
        out_specs=(row_spec(n), row_spec(n)),
        compiler_params=pltpu.CompilerParams(
            dimension_semantics=("parallel",),
            vmem_limit_bytes=V7X_VMEM_LIMIT_BYTES),
        name="out_proj",
    )(ys, yp, w, x, nw)


def _down_kernel(a_ref, w_ref, h_ref, nw_ref, o_ref, *, final_norm):
    for rows in _row_chunks(o_ref.shape[0]):
        h = h_ref[rows, :] + _dot(a_ref[rows, :], w_ref[...])
        o_ref[rows, :] = _rms_normalize(h, nw_ref[...]) if final_norm else h


def _down_proj(act, w, h, nw, *, tm, final_norm):
    m, k = act.shape
    n = w.shape[1]
    row_spec = lambda width: pl.BlockSpec((tm, width), lambda i: (i, 0))
    return pl.pallas_call(
        functools.partial(_down_kernel, final_norm=final_norm),
        out_shape=jax.ShapeDtypeStruct((m, n), F32),
        grid=(m // tm,),
        in_specs=[row_spec(k), _resident(w.shape), row_spec(n), _resident((1, n))],
        out_specs=row_spec(n),
        compiler_params=pltpu.CompilerParams(
            dimension_semantics=("parallel",),
            vmem_limit_bytes=V7X_VMEM_LIMIT_BYTES),
        name="ffn_down",
    )(act, w, h, nw)


def _pool_tile(i, u_refs, halo_refs, pw_ref, ps_ref, o_ref):
    tl = o_ref.shape[0]
    t = i * tl + lax.broadcasted_iota(jnp.int32, (tl, POOL_GROUP_DIM), 0)
    for g, window in enumerate(POOL_WINDOWS):
        cols = slice(g * POOL_GROUP_DIM, (g + 1) * POOL_GROUP_DIM)
        u = u_refs[g][...]
        halo = jnp.where(i == 0, 0.0, halo_refs[g][...])
        s = jnp.concatenate([halo, u], axis=0)
        shift = 1
        while shift < window:
            s = s + pltpu.roll(s, shift, axis=0)
            shift *= 2
        count = jnp.minimum(t + 1, window).astype(F32)
        pooled = s[POOL_HALO:] / count - u
        y = _dot(pooled.astype(BF16), pw_ref[g].astype(BF16)) * ps_ref[:, cols]
        o_ref[:, cols] = y.astype(BF16)


def _pool_specs(tl, step_of):
    gd = POOL_GROUP_DIM
    assert COL_U % gd == 0 and tl % POOL_HALO == 0
    halo_blocks_per_tile = tl // POOL_HALO
    n_groups = len(POOL_WINDOWS)
    u_specs = [pl.BlockSpec((tl, gd), lambda *g, k=k: (step_of(*g), COL_U // gd + k))
               for k in range(n_groups)]
    halo_specs = [
        pl.BlockSpec((POOL_HALO, gd),
                     lambda *g, k=k: (jnp.maximum(step_of(*g) * halo_blocks_per_tile - 1, 0),
                                      COL_U // gd + k))
        for k in range(n_groups)]
    return u_specs + halo_specs


assert N_REPLICAS * HEADS_PER_GROUP <= V7X_LANES


def _expansion_matrix():
    lane = jnp.arange(V7X_LANES)
    col = jnp.arange(N_EXPANDED * GROUP_WIDTH)
    replica, head = lane // HEADS_PER_GROUP, lane % HEADS_PER_GROUP
    quantity = replica // 3
    hit = ((quantity[:, None] == col[None, :] // GROUP_WIDTH)
           & (head[:, None] == (col[None, :] % GROUP_WIDTH) // HEAD_DIM)
           & (replica[:, None] < N_REPLICAS))
    return hit.astype(BF16)


def _ssd_group(z, xs, bm, cm, dt_raw, dtb, alog, dskip, nw, state_ref, expand, tril, causal):
    q = xs.shape[0]
    x = dt_raw + dtb
    dt = jnp.maximum(x, 0.0) + jnp.log1p(jnp.exp(-jnp.abs(x)))
    a = -jnp.exp(alog)
    a_cum = _dot3_left(tril, dt * a) * LOG2_E
    a_cum_t = a_cum.T
    a_last = a_cum[q - 1:q, :]

    replica = lax.broadcasted_iota(jnp.int32, (1, V7X_LANES), 1) // HEADS_PER_GROUP
    packed = jnp.where(replica < 3, dt,
                       jnp.where(replica < 6, jnp.exp2(a_cum), jnp.exp2(a_last - a_cum)))
    hi = packed.astype(BF16)
    r1 = packed - hi.astype(F32)
    r2 = r1 - r1.astype(BF16).astype(F32)
    level = replica % 3
    split = jnp.where(level == 0, packed, jnp.where(level == 1, r1, r2)).astype(BF16)
    expanded = _dot(split, expand)
    dt_b = expanded[:, :GROUP_WIDTH]
    decay_in_b = expanded[:, GROUP_WIDTH:2 * GROUP_WIDTH]
    decay_out_b = expanded[:, 2 * GROUP_WIDTH:]

    xdt = xs * dt_b
    bm_bf = bm.astype(BF16)
    cm_bf = cm.astype(BF16)
    cb = _dot_nt(cm_bf, bm_bf)

    xdt_bf = xdt.astype(BF16)
    lane = lax.broadcasted_iota(jnp.int32, (1, V7X_LANES), 1)
    keep = [jnp.where((lane // HEAD_DIM) == side, 1.0, 0.0).astype(BF16) for side in range(2)]
    pairs = []
    for pair in range(HEADS_PER_GROUP // 2):
        xdt_pair = xdt_bf[:, pair * V7X_LANES:(pair + 1) * V7X_LANES]
        acc = jnp.zeros((q, V7X_LANES), F32)
        for side in range(2):
            r = 2 * pair + side
            seg = a_cum[:, r:r + 1] - a_cum_t[r:r + 1, :]
            decay = jnp.exp2(jnp.where(causal, seg, -jnp.inf))
            m_r = (cb * decay).astype(BF16)
            acc = acc + _dot(m_r, xdt_pair * keep[side])
        pairs.append(acc)
    y_diag = jnp.concatenate(pairs, axis=-1)

    state = state_ref[...]
    y_off = _dot(cm_bf, state.astype(BF16)) * decay_in_b
    xw = (xdt * decay_out_b).astype(BF16)
    new_state = lax.dot_general(bm_bf, xw, (((0,), (0,)), ((), ())), preferred_element_type=F32)
    state_ref[...] = state * decay_in_b[q - 1:q, :] + new_state

    y = y_diag + y_off + dskip * xs
    g = y * _silu(z)
    return _rms_normalize(g, nw).astype(BF16)


def _mixers_kernel(*refs, groups, n_side):
    refs = list(refs)
    take = lambda n: [refs.pop(0) for _ in range(n)]
    n_pool = len(POOL_WINDOWS)
    z_refs = take(groups)
    xs_ref, b_ref, c_ref, dt_ref, dtb_ref, alog_ref, dskip_ref, nw_ref, expand_ref = take(9)
    u_refs, halo_refs = take(n_pool), take(n_pool)
    pw_ref, ps_ref = take(2)
    side_in = take(n_side)
    o_ref, yp_ref = take(2)
    side_out = take(n_side)
    state_ref, = refs
    q = SSD_CHUNK

    _pool_tile(pl.program_id(1), u_refs, halo_refs, pw_ref, ps_ref, yp_ref)

    for src_ref, dst_ref in zip(side_in, side_out):
        dst_ref[...] = src_ref[...].astype(BF16)

    @pl.when(pl.program_id(1) == 0)
    def _():
        state_ref[...] = jnp.zeros_like(state_ref)

    row_i = lax.broadcasted_iota(jnp.int32, (q, q), 0)
    col_i = lax.broadcasted_iota(jnp.int32, (q, q), 1)
    causal = row_i >= col_i
    tril = jnp.where(causal, 1.0, 0.0).astype(BF16)
    expand = expand_ref[...]
    for g in range(groups):
        wide = slice(g * GROUP_WIDTH, (g + 1) * GROUP_WIDTH)
        narrow = slice(g * D_STATE, (g + 1) * D_STATE)
        o_ref[:, wide] = _ssd_group(
            z_refs[g][...], xs_ref[:, wide], b_ref[:, narrow], c_ref[:, narrow], dt_ref[:, narrow],
            dtb_ref[:, narrow], alog_ref[:, narrow], dskip_ref[:, wide], nw_ref[:, wide],
            state_ref.at[g], expand, tril, causal)


def _mixers(proj, dtb_c, alog_c, dskip_full, norm_w, pool_w, pool_scale, side_weights, *, groups):
    m = proj.shape[0]
    q = SSD_CHUNK
    gw = groups * GROUP_WIDTH
    ds = groups * D_STATE
    assert V7X_LANES == D_STATE and N_GROUPS % groups == 0

    def col(start, width):
        assert start % width == 0
        return start // width

    row_spec = lambda width, block0: pl.BlockSpec((q, width), lambda g, c: (c, block0 + g))
    par_spec = lambda width: pl.BlockSpec((1, width), lambda g, c: (0, g))
    z_specs = [
        pl.BlockSpec((q, GROUP_WIDTH),
                     lambda g, c, k=k: (c, col(COL_Z, GROUP_WIDTH) + g * groups + k))
        for k in range(groups)]
    expand = _expansion_matrix()
    assert groups == N_GROUPS
    side_specs = _side_cast_specs(side_weights, m // q, lambda g, c: c)
    return pl.pallas_call(
        functools.partial(_mixers_kernel, groups=groups, n_side=len(side_weights)),
        out_shape=(jax.ShapeDtypeStruct((m, D_SSM), BF16), jax.ShapeDtypeStruct((m, D_POOL), BF16),
                   *[jax.ShapeDtypeStruct(w.shape, BF16) for w in side_weights]),
        grid=(N_GROUPS // groups, m // q),
        in_specs=z_specs + [
            row_spec(gw, col(COL_XS, gw)),
            row_spec(ds, col(COL_B, ds)),
            row_spec(ds, col(COL_C, ds)),
            row_spec(ds, col(COL_DT, ds)),
            par_spec(ds),
            par_spec(ds),
            par_spec(gw),
            par_spec(gw),
            pl.BlockSpec(expand.shape, lambda g, c: (0, 0)),
        ] + _pool_specs(q, lambda g, c: c) + [
            pl.BlockSpec(pool_w.shape, lambda g, c: (0, 0, 0)),
            pl.BlockSpec(pool_scale.shape, lambda g, c: (0, 0)),
        ] + side_specs,
        out_specs=(pl.BlockSpec((q, gw), lambda g, c: (c, g)),
                   pl.BlockSpec((q, D_POOL), lambda g, c: (c, 0)), *side_specs),
        scratch_shapes=[pltpu.VMEM((groups, D_STATE, GROUP_WIDTH), F32)],
        compiler_params=pltpu.CompilerParams(
            dimension_semantics=("parallel", "arbitrary"),
            vmem_limit_bytes=V7X_VMEM_LIMIT_BYTES),
        name="mixers",
    )(*([proj] * (groups + 4)), dtb_c, alog_c, dskip_full, norm_w, expand,
      *([proj] * (2 * len(POOL_WINDOWS))), pool_w, pool_scale, *side_weights)


def _pad_heads_to_lanes(v):
    v = jnp.tile(v.reshape(N_GROUPS, HEADS_PER_GROUP), (1, N_REPLICAS))
    v = jnp.pad(v, ((0, 0), (0, V7X_LANES - N_REPLICAS * HEADS_PER_GROUP)))
    return v.reshape(1, N_GROUPS * V7X_LANES)


def kernel(x, attn_norm_w, w_in, conv_w, conv_b, dt_bias, a_log, d_skip, ssd_norm_w, pool_w,
           pool_scale, w_out, ffn_norm_w, w_gate, w_up, w_down, final_norm_w):
    bsz, seqlen, d_model = x.shape
    depth = w_in.shape[0]
    h = x.reshape(bsz * seqlen, d_model)
    assert bsz == 1 and seqlen % 2048 == 0 and d_model == D_MODEL

    for i in range(depth):
        w_cat = _w_in_prep(w_in[i].T, tk=W_PREP_COLS)
        pool_w_rows = pool_w[i].reshape(-1, POOL_GROUP_DIM)
        proj, w_out_bf, pool_w_bf = _in_proj(
            h, attn_norm_w[i][None, :], w_cat, conv_w[i], conv_b[i][None, :],
            (w_out[i], pool_w_rows), tm=IN_PROJ_TILE[0], tn=IN_PROJ_TILE[1],
            side_slabs=IN_PROJ_SIDE_SLABS)

        y_ssd, y_pool, w_gate_bf, w_up_bf = _mixers(
            proj, _pad_heads_to_lanes(dt_bias[i]), _pad_heads_to_lanes(a_log[i]),
            jnp.repeat(d_skip[i], HEAD_DIM)[None, :], ssd_norm_w[i][None, :],
            pool_w_bf.reshape(pool_w[i].shape), pool_scale[i][None, :], (w_gate[i], w_up[i]),
            groups=N_GROUPS)

        h, hn = _out_proj(y_ssd, y_pool, w_out_bf, h, ffn_norm_w[i][None, :], tm=RESIDENT_ROW_TILE)

        act, w_down_bf = _swiglu(hn, w_gate_bf, w_up_bf, (w_down[i],),
                                 tm=FFN_UP_TILE[0], tn=FFN_UP_TILE[1])
        last = i == depth - 1
        h = _down_proj(act, w_down_bf, h, final_norm_w[None, :], tm=RESIDENT_ROW_TILE,
                       final_norm=last)

    return h.reshape(bsz, seqlen, d_model)
```

```python
import functools

import jax
import jax.numpy as jnp
from jax import lax
from jax.experimental import pallas as pl
from jax.experimental.pallas import tpu as pltpu

F32 = jnp.float32
BF16 = jnp.bfloat16

NORM_EPS = 1e-5
LOG2_E = 1.4426950408889634
D_MODEL = 2048
D_SSM = 2048
HEAD_DIM = 64
N_HEADS = D_SSM // HEAD_DIM
N_GROUPS = 4
HEADS_PER_GROUP = N_HEADS // N_GROUPS
GROUP_WIDTH = D_SSM // N_GROUPS
D_STATE = 128
CONV_WIDTH = 4
D_POOL = 2048
POOL_WINDOWS = (2, 4, 8, 16)
POOL_GROUP_DIM = D_POOL // len(POOL_WINDOWS)

V7X_LANES = 128
V7X_SUBLANES = 8
V7X_VMEM_LIMIT_BYTES = 60 * 1024 * 1024

W_PREP_COLS = 256
IN_PROJ_TILE = (1024, 1536)
IN_PROJ_SIDE_SLABS = 32
RESIDENT_ROW_TILE = 512
FFN_UP_TILE = (2048, 512)
FFN_UP_CHUNK_ROWS = 128

COL_XS = 0
COL_B = COL_XS + D_SSM
COL_C = COL_B + N_GROUPS * D_STATE
D_CONV = COL_C + N_GROUPS * D_STATE
COL_Z = D_CONV
COL_U = COL_Z + D_SSM
COL_DT = COL_U + D_POOL
D_PROJ = COL_DT + N_GROUPS * V7X_LANES

SSD_CHUNK = 256
N_EXPANDED = 3
N_REPLICAS = 3 * N_EXPANDED
HALO_ROWS = V7X_SUBLANES
POOL_HALO = 16


def _sigmoid(x):
    return 1.0 / (1.0 + jnp.exp(-x))


def _silu(x):
    return x * _sigmoid(x)


def _rms_normalize(x, w):
    ms = jnp.mean(x * x, axis=-1, keepdims=True)
    return x * lax.rsqrt(ms + NORM_EPS) * w


def _split3(x):
    hi = x.astype(BF16)
    r1 = x - hi.astype(F32)
    mid = r1.astype(BF16)
    lo = (r1 - mid.astype(F32)).astype(BF16)
    return hi, mid, lo


def _dot(a, b):
    return jnp.dot(a, b, preferred_element_type=F32)


def _dot_nt(a, b):
    return lax.dot_general(a, b, (((1,), (1,)), ((), ())), preferred_element_type=F32)


def _dot3_left(lhs_bf16, x):
    hi, mid, lo = _split3(x)
    return _dot(lhs_bf16, hi) + _dot(lhs_bf16, mid) + _dot(lhs_bf16, lo)


SRC_XBC = D_SSM
SRC_DT = SRC_XBC + D_SSM + 2 * N_GROUPS * D_STATE
SRC_U = SRC_DT + N_HEADS
D_IN_PROJ = SRC_U + D_POOL


def _w_in_prep_kernel(w_ref, o_ref):
    o_ref[COL_Z:COL_Z + D_SSM, :] = w_ref[:D_SSM, :].astype(BF16)
    o_ref[COL_U:COL_U + D_POOL, :] = w_ref[SRC_U:SRC_U + D_POOL, :].astype(BF16)
    o_ref[COL_XS:D_CONV, :] = w_ref[SRC_XBC:SRC_DT, :].astype(BF16)
    zeros = jnp.zeros((V7X_LANES - N_REPLICAS * HEADS_PER_GROUP, o_ref.shape[1]), F32)
    for g in range(N_GROUPS):
        row0 = COL_DT + g * V7X_LANES
        src0 = SRC_DT + g * HEADS_PER_GROUP
        heads = w_ref[src0:src0 + HEADS_PER_GROUP, :]
        block = jnp.concatenate([heads] * N_REPLICAS + [zeros], axis=0)
        o_ref[row0:row0 + V7X_LANES, :] = block.astype(BF16)


def _w_in_prep(w_t, *, tk):
    n, k = w_t.shape
    assert n == D_IN_PROJ
    return pl.pallas_call(
        _w_in_prep_kernel,
        out_shape=jax.ShapeDtypeStruct((D_PROJ, k), BF16),
        grid=(k // tk,),
        in_specs=[pl.BlockSpec((n, tk), lambda i: (0, i))],
        out_specs=pl.BlockSpec((D_PROJ, tk), lambda i: (0, i)),
        compiler_params=pltpu.CompilerParams(
            dimension_semantics=("parallel",),
            vmem_limit_bytes=V7X_VMEM_LIMIT_BYTES),
        name="w_in_prep",
    )(w_t)


def _side_cast_specs(weights, n_slabs, step_of):
    specs = []
    for w in weights:
        rows = w.shape[0] // n_slabs
        assert rows * n_slabs == w.shape[0] and rows % (2 * V7X_SUBLANES) == 0
        specs.append(pl.BlockSpec(
            (rows, w.shape[1]), lambda *g: (jnp.minimum(step_of(*g), n_slabs - 1), 0)))
    return specs


CONV_CHUNK = 256


def _conv_silu(r, halo, w_ref, b_ref, cols):
    ext = jnp.concatenate([halo, r], axis=0)
    acc = r * w_ref[CONV_WIDTH - 1:CONV_WIDTH, cols] + b_ref[:, cols]
    for back in range(1, CONV_WIDTH):
        tap = w_ref[CONV_WIDTH - 1 - back:CONV_WIDTH - back, cols]
        acc = acc + pltpu.roll(ext, back, axis=0)[HALO_ROWS:] * tap
    return _silu(acc)


def _in_proj_kernel(x_ref, nw_ref, w_ref, cw_ref, cb_ref, *rest, n_conv_blocks, n_side):
    side_in, rest = rest[:n_side], rest[n_side:]
    o_ref, side_out = rest[0], rest[1:1 + n_side]
    hn_ref, halo_ref, raw_ref = rest[1 + n_side:]
    i = pl.program_id(0)
    j = pl.program_id(1)
    tm, tn = o_ref.shape

    for src_ref, dst_ref in zip(side_in, side_out):
        dst_ref[...] = src_ref[...].astype(BF16)

    @pl.when(j == 0)
    def _():
        hn_ref[...] = _rms_normalize(x_ref[...], nw_ref[...]).astype(BF16)

    @pl.when((i == 0) & (j == 0))
    def _():
        halo_ref[...] = jnp.zeros_like(halo_ref)

    @pl.when(j < n_conv_blocks)
    def _():
        n_pieces = tn // CONV_CHUNK
        piece = lambda c: slice(c * CONV_CHUNK, (c + 1) * CONV_CHUNK)
        park = jnp.minimum(j, 0)
        raw_ref[park] = _dot_nt(hn_ref[...], w_ref[piece(0), :])
        for c in range(n_pieces):
            cols = piece(c)
            r = raw_ref[park]
            o_ref[:, cols] = _conv_silu(r, halo_ref[j, :, cols], cw_ref, cb_ref, cols)
            halo_ref[j, :, cols] = r[tm - HALO_ROWS:]
            if c + 1 < n_pieces:
                raw_ref[park] = _dot_nt(hn_ref[...], w_ref[piece(c + 1), :])

    @pl.when(j >= n_conv_blocks)
    def _():
        o_ref[...] = _dot_nt(hn_ref[...], w_ref[...])


def _in_proj(x, nw, w_t, conv_w, conv_b, side_weights, *, tm, tn, side_slabs):
    m, k = x.shape
    n = w_t.shape[0]
    assert D_CONV % tn == 0 and tn % CONV_CHUNK == 0
    n_conv_blocks = D_CONV // tn
    n_col_blocks = n // tn
    assert (m // tm) * n_col_blocks >= side_slabs
    conv_block = lambda i, j: (0, jnp.minimum(j, n_conv_blocks - 1))
    side_specs = _side_cast_specs(side_weights, side_slabs, lambda i, j: i * n_col_blocks + j)
    return pl.pallas_call(
        functools.partial(_in_proj_kernel, n_conv_blocks=n_conv_blocks, n_side=len(side_weights)),
        out_shape=(jax.ShapeDtypeStruct((m, n), F32),
                   *[jax.ShapeDtypeStruct(w.shape, BF16) for w in side_weights]),
        grid=(m // tm, n_col_blocks),
        in_specs=[
            pl.BlockSpec((tm, k), lambda i, j: (i, 0)),
            pl.BlockSpec((1, k), lambda i, j: (0, 0)),
            pl.BlockSpec((tn, k), lambda i, j: (j, 0)),
            pl.BlockSpec((CONV_WIDTH, tn), conv_block),
            pl.BlockSpec((1, tn), conv_block),
        ] + side_specs,
        out_specs=(pl.BlockSpec((tm, tn), lambda i, j: (i, j)), *side_specs),
        scratch_shapes=[pltpu.VMEM((tm, k), BF16),
                        pltpu.VMEM((n_conv_blocks, HALO_ROWS, tn), F32),
                        pltpu.VMEM((1, tm, CONV_CHUNK), F32)],
        compiler_params=pltpu.CompilerParams(
            dimension_semantics=("arbitrary", "arbitrary"),
            vmem_limit_bytes=V7X_VMEM_LIMIT_BYTES),
        name="in_proj",
    )(x, nw, w_t, conv_w, conv_b, *side_weights)


def _swiglu_kernel(hn_ref, wg_ref, wu_ref, *rest):
    n_side = (len(rest) - 1) // 2
    side_in, o_ref, side_out = rest[:n_side], rest[n_side], rest[n_side + 1:]
    for src_ref, dst_ref in zip(side_in, side_out):
        dst_ref[...] = src_ref[...].astype(BF16)
    for r in range(o_ref.shape[0] // FFN_UP_CHUNK_ROWS):
        rows = slice(r * FFN_UP_CHUNK_ROWS, (r + 1) * FFN_UP_CHUNK_ROWS)
        gate = _dot(hn_ref[rows, :], wg_ref[...])
        up = _dot(hn_ref[rows, :], wu_ref[...])
        o_ref[rows, :] = (_silu(gate) * up).astype(BF16)


def _swiglu(hn, wg, wu, side_weights, *, tm, tn):
    m, k = hn.shape
    n = wg.shape[1]
    n_col_tiles = n // tn
    side_specs = _side_cast_specs(side_weights, (m // tm) * n_col_tiles,
                                  lambda i, j: i * n_col_tiles + j)
    return pl.pallas_call(
        _swiglu_kernel,
        out_shape=(jax.ShapeDtypeStruct((m, n), BF16),
                   *[jax.ShapeDtypeStruct(w.shape, BF16) for w in side_weights]),
        grid=(m // tm, n_col_tiles),
        in_specs=[
            pl.BlockSpec((tm, k), lambda i, j: (i, 0)),
            pl.BlockSpec((k, tn), lambda i, j: (0, j)),
            pl.BlockSpec((k, tn), lambda i, j: (0, j)),
        ] + side_specs,
        out_specs=(pl.BlockSpec((tm, tn), lambda i, j: (i, j)), *side_specs),
        compiler_params=pltpu.CompilerParams(
            dimension_semantics=("arbitrary", "arbitrary"),
            vmem_limit_bytes=V7X_VMEM_LIMIT_BYTES),
        name="ffn_up",
    )(hn, wg, wu, *side_weights)


def _out_proj_kernel(ys_ref, yp_ref, w_ref, x_ref, nw_ref, h_ref, hn_ref):
    k = ys_ref.shape[1]
    h = x_ref[...] + _dot(ys_ref[...], w_ref[:k, :]) + _dot(yp_ref[...], w_ref[k:, :])
    h_ref[...] = h
    hn_ref[...] = _rms_normalize(h, nw_ref[...]).astype(BF16)


def _resident(shape):
    return pl.BlockSpec(shape, lambda i: (0,) * len(shape), pipeline_mode=pl.Buffered(1))


def _out_proj(ys, yp, w, x, nw, *, tm):
    m, k = ys.shape
    n = w.shape[1]
    row_spec = lambda width: pl.BlockSpec((tm, width), lambda i: (i, 0))
    return pl.pallas_call(
        _out_proj_kernel,
        out_shape=(jax.ShapeDtypeStruct((m, n), F32), jax.ShapeDtypeStruct((m, n), BF16)),
        grid=(m // tm,),
        in_specs=[row_spec(k), row_spec(k), _resident(w.shape), row_spec(n), _resident((1, n))],
        out_specs=(row_spec(n), row_spec(n)),
        compiler_params=pltpu.CompilerParams(
            dimension_semantics=("parallel",),
            vmem_limit_bytes=V7X_VMEM_LIMIT_BYTES),
        name="out_proj",
    )(ys, yp, w, x, nw)


def _down_kernel(a_ref, w_ref, h_ref, nw_ref, o_ref, *, final_norm):
    h = h_ref[...] + _dot(a_ref[...], w_ref[...])
    o_ref[...] = _rms_normalize(h, nw_ref[...]) if final_norm else h


def _down_proj(act, w, h, nw, *, tm, final_norm):
    m, k = act.shape
    n = w.shape[1]
    row_spec = lambda width: pl.BlockSpec((tm, width), lambda i: (i, 0))
    return pl.pallas_call(
        functools.partial(_down_kernel, final_norm=final_norm),
        out_shape=jax.ShapeDtypeStruct((m, n), F32),
        grid=(m // tm,),
        in_specs=[row_spec(k), _resident(w.shape), row_spec(n), _resident((1, n))],
        out_specs=row_spec(n),
        compiler_params=pltpu.CompilerParams(
            dimension_semantics=("parallel",),
            vmem_limit_bytes=V7X_VMEM_LIMIT_BYTES),
        name="ffn_down",
    )(act, w, h, nw)


def _pool_tile(i, u_refs, halo_refs, pw_ref, ps_ref, o_ref):
    tl = o_ref.shape[0]
    t = i * tl + lax.broadcasted_iota(jnp.int32, (tl, POOL_GROUP_DIM), 0)
    for g, window in enumerate(POOL_WINDOWS):
        cols = slice(g * POOL_GROUP_DIM, (g + 1) * POOL_GROUP_DIM)
        u = u_refs[g][...]
        halo = jnp.where(i == 0, 0.0, halo_refs[g][...])
        s = jnp.concatenate([halo, u], axis=0)
        shift = 1
        while shift < window:
            s = s + pltpu.roll(s, shift, axis=0)
            shift *= 2
        count = jnp.minimum(t + 1, window).astype(F32)
        pooled = s[POOL_HALO:] / count - u
        y = _dot(pooled.astype(BF16), pw_ref[g].astype(BF16)) * ps_ref[:, cols]
        o_ref[:, cols] = y.astype(BF16)


def _pool_specs(tl, step_of):
    gd = POOL_GROUP_DIM
    assert COL_U % gd == 0 and tl % POOL_HALO == 0
    halo_blocks_per_tile = tl // POOL_HALO
    n_groups = len(POOL_WINDOWS)
    u_specs = [pl.BlockSpec((tl, gd), lambda *g, k=k: (step_of(*g), COL_U // gd + k))
               for k in range(n_groups)]
    halo_specs = [
        pl.BlockSpec((POOL_HALO, gd),
                     lambda *g, k=k: (jnp.maximum(step_of(*g) * halo_blocks_per_tile - 1, 0),
                                      COL_U // gd + k))
        for k in range(n_groups)]
    return u_specs + halo_specs


assert N_REPLICAS * HEADS_PER_GROUP <= V7X_LANES


def _expansion_matrix():
    lane = jnp.arange(V7X_LANES)
    col = jnp.arange(N_EXPANDED * GROUP_WIDTH)
    replica, head = lane // HEADS_PER_GROUP, lane % HEADS_PER_GROUP
    quantity = replica // 3
    hit = ((quantity[:, None] == col[None, :] // GROUP_WIDTH)
           & (head[:, None] == (col[None, :] % GROUP_WIDTH) // HEAD_DIM)
           & (replica[:, None] < N_REPLICAS))
    return hit.astype(BF16)


def _ssd_group(z, xs, bm, cm, dt_raw, dtb, alog, dskip, nw, state_ref, expand, tril, causal):
    q = xs.shape[0]
    x = dt_raw + dtb
    dt = jnp.maximum(x, 0.0) + jnp.log1p(jnp.exp(-jnp.abs(x)))
    a = -jnp.exp(alog)
    a_cum = _dot3_left(tril, dt * a) * LOG2_E
    a_cum_t = a_cum.T
    a_last = a_cum[q - 1:q, :]

    replica = lax.broadcasted_iota(jnp.int32, (1, V7X_LANES), 1) // HEADS_PER_GROUP
    packed = jnp.where(replica < 3, dt,
                       jnp.where(replica < 6, jnp.exp2(a_cum), jnp.exp2(a_last - a_cum)))
    hi = packed.astype(BF16)
    r1 = packed - hi.astype(F32)
    r2 = r1 - r1.astype(BF16).astype(F32)
    level = replica % 3
    split = jnp.where(level == 0, packed, jnp.where(level == 1, r1, r2)).astype(BF16)
    expanded = _dot(split, expand)
    dt_b = expanded[:, :GROUP_WIDTH]
    decay_in_b = expanded[:, GROUP_WIDTH:2 * GROUP_WIDTH]
    decay_out_b = expanded[:, 2 * GROUP_WIDTH:]

    xdt = xs * dt_b
    bm_bf = bm.astype(BF16)
    cm_bf = cm.astype(BF16)
    cb = _dot_nt(cm_bf, bm_bf)

    xdt_bf = xdt.astype(BF16)
    lane = lax.broadcasted_iota(jnp.int32, (1, V7X_LANES), 1)
    keep = [jnp.where((lane // HEAD_DIM) == side, 1.0, 0.0).astype(BF16) for side in range(2)]
    pairs = []
    for pair in range(HEADS_PER_GROUP // 2):
        xdt_pair = xdt_bf[:, pair * V7X_LANES:(pair + 1) * V7X_LANES]
        acc = jnp.zeros((q, V7X_LANES), F32)
        for side in range(2):
            r = 2 * pair + side
            seg = a_cum[:, r:r + 1] - a_cum_t[r:r + 1, :]
            decay = jnp.exp2(jnp.where(causal, seg, -jnp.inf))
            m_r = (cb * decay).astype(BF16)
            acc = acc + _dot(m_r, xdt_pair * keep[side])
        pairs.append(acc)
    y_diag = jnp.concatenate(pairs, axis=-1)

    state = state_ref[...]
    y_off = _dot(cm_bf, state.astype(BF16)) * decay_in_b
    xw = (xdt * decay_out_b).astype(BF16)
    new_state = lax.dot_general(bm_bf, xw, (((0,), (0,)), ((), ())), preferred_element_type=F32)
    state_ref[...] = state * decay_in_b[q - 1:q, :] + new_state

    y = y_diag + y_off + dskip * xs
    g = y * _silu(z)
    return _rms_normalize(g, nw).astype(BF16)


def _mixers_kernel(*refs, groups, n_side):
    refs = list(refs)
    take = lambda n: [refs.pop(0) for _ in range(n)]
    n_pool = len(POOL_WINDOWS)
    z_refs = take(groups)
    xs_ref, b_ref, c_ref, dt_ref, dtb_ref, alog_ref, dskip_ref, nw_ref, expand_ref = take(9)
    u_refs, halo_refs = take(n_pool), take(n_pool)
    pw_ref, ps_ref = take(2)
    side_in = take(n_side)
    o_ref, yp_ref = take(2)
    side_out = take(n_side)
    state_ref, = refs
    q = SSD_CHUNK

    _pool_tile(pl.program_id(1), u_refs, halo_refs, pw_ref, ps_ref, yp_ref)

    for src_ref, dst_ref in zip(side_in, side_out):
        dst_ref[...] = src_ref[...].astype(BF16)

    @pl.when(pl.program_id(1) == 0)
    def _():
        state_ref[...] = jnp.zeros_like(state_ref)

    row_i = lax.broadcasted_iota(jnp.int32, (q, q), 0)
    col_i = lax.broadcasted_iota(jnp.int32, (q, q), 1)
    causal = row_i >= col_i
    tril = jnp.where(causal, 1.0, 0.0).astype(BF16)
    expand = expand_ref[...]
    for g in range(groups):
        wide = slice(g * GROUP_WIDTH, (g + 1) * GROUP_WIDTH)
        narrow = slice(g * D_STATE, (g + 1) * D_STATE)
        o_ref[:, wide] = _ssd_group(
            z_refs[g][...], xs_ref[:, wide], b_ref[:, narrow], c_ref[:, narrow], dt_ref[:, narrow],
            dtb_ref[:, narrow], alog_ref[:, narrow], dskip_ref[:, wide], nw_ref[:, wide],
            state_ref.at[g], expand, tril, causal)


def _mixers(proj, dtb_c, alog_c, dskip_full, norm_w, pool_w, pool_scale, side_weights, *, groups):
    m = proj.shape[0]
    q = SSD_CHUNK
    gw = groups * GROUP_WIDTH
    ds = groups * D_STATE
    assert V7X_LANES == D_STATE and N_GROUPS % groups == 0

    def col(start, width):
        assert start % width == 0
        return start // width

    row_spec = lambda width, block0: pl.BlockSpec((q, width), lambda g, c: (c, block0 + g))
    par_spec = lambda width: pl.BlockSpec((1, width), lambda g, c: (0, g))
    z_specs = [
        pl.BlockSpec((q, GROUP_WIDTH),
                     lambda g, c, k=k: (c, col(COL_Z, GROUP_WIDTH) + g * groups + k))
        for k in range(groups)]
    expand = _expansion_matrix()
    assert groups == N_GROUPS
    side_specs = _side_cast_specs(side_weights, m // q, lambda g, c: c)
    return pl.pallas_call(
        functools.partial(_mixers_kernel, groups=groups, n_side=len(side_weights)),
        out_shape=(jax.ShapeDtypeStruct((m, D_SSM), BF16), jax.ShapeDtypeStruct((m, D_POOL), BF16),
                   *[jax.ShapeDtypeStruct(w.shape, BF16) for w in side_weights]),
        grid=(N_GROUPS // groups, m // q),
        in_specs=z_specs + [
            row_spec(gw, col(COL_XS, gw)),
            row_spec(ds, col(COL_B, ds)),
            row_spec(ds, col(COL_C, ds)),
            row_spec(ds, col(COL_DT, ds)),
            par_spec(ds),
            par_spec(ds),
            par_spec(gw),
            par_spec(gw),
            pl.BlockSpec(expand.shape, lambda g, c: (0, 0)),
        ] + _pool_specs(q, lambda g, c: c) + [
            pl.BlockSpec(pool_w.shape, lambda g, c: (0, 0, 0)),
            pl.BlockSpec(pool_scale.shape, lambda g, c: (0, 0)),
        ] + side_specs,
        out_specs=(pl.BlockSpec((q, gw), lambda g, c: (c, g)),
                   pl.BlockSpec((q, D_POOL), lambda g, c: (c, 0)), *side_specs),
        scratch_shapes=[pltpu.VMEM((groups, D_STATE, GROUP_WIDTH), F32)],
        compiler_params=pltpu.CompilerParams(
            dimension_semantics=("parallel", "arbitrary"),
            vmem_limit_bytes=V7X_VMEM_LIMIT_BYTES),
        name="mixers",
    )(*([proj] * (groups + 4)), dtb_c, alog_c, dskip_full, norm_w, expand,
      *([proj] * (2 * len(POOL_WINDOWS))), pool_w, pool_scale, *side_weights)


def _pad_heads_to_lanes(v):
    v = jnp.tile(v.reshape(N_GROUPS, HEADS_PER_GROUP), (1, N_REPLICAS))
    v = jnp.pad(v, ((0, 0), (0, V7X_LANES - N_REPLICAS * HEADS_PER_GROUP)))
    return v.reshape(1, N_GROUPS * V7X_LANES)


def kernel(x, attn_norm_w, w_in, conv_w, conv_b, dt_bias, a_log, d_skip, ssd_norm_w, pool_w,
           pool_scale, w_out, ffn_norm_w, w_gate, w_up, w_down, final_norm_w):
    bsz, seqlen, d_model = x.shape
    depth = w_in.shape[0]
    h = x.reshape(bsz * seqlen, d_model)
    assert bsz == 1 and seqlen % 2048 == 0 and d_model == D_MODEL

    for i in range(depth):
        w_cat = _w_in_prep(w_in[i].T, tk=W_PREP_COLS)
        pool_w_rows = pool_w[i].reshape(-1, POOL_GROUP_DIM)
        proj, w_out_bf, pool_w_bf = _in_proj(
            h, attn_norm_w[i][None, :], w_cat, conv_w[i], conv_b[i][None, :],
            (w_out[i], pool_w_rows), tm=IN_PROJ_TILE[0], tn=IN_PROJ_TILE[1],
            side_slabs=IN_PROJ_SIDE_SLABS)

        y_ssd, y_pool, w_gate_bf, w_up_bf = _mixers(
            proj, _pad_heads_to_lanes(dt_bias[i]), _pad_heads_to_lanes(a_log[i]),
            jnp.repeat(d_skip[i], HEAD_DIM)[None, :], ssd_norm_w[i][None, :],
            pool_w_bf.reshape(pool_w[i].shape), pool_scale[i][None, :], (w_gate[i], w_up[i]),
            groups=N_GROUPS)

        h, hn = _out_proj(y_ssd, y_pool, w_out_bf, h, ffn_norm_w[i][None, :], tm=RESIDENT_ROW_TILE)

        act, w_down_bf = _swiglu(hn, w_gate_bf, w_up_bf, (w_down[i],),
                                 tm=FFN_UP_TILE[0], tn=FFN_UP_TILE[1])
        last = i == depth - 1
        h = _down_proj(act, w_down_bf, h, final_norm_w[None, :], tm=RESIDENT_ROW_TILE,
                       final_norm=last)

    return h.reshape(bsz, seqlen, d_model)
```

```python
import functools

import jax
import jax.numpy as jnp
from jax import lax
from jax.experimental import pallas as pl
from jax.experimental.pallas import tpu as pltpu

F32 = jnp.float32
BF16 = jnp.bfloat16

NORM_EPS = 1e-5
LOG2_E = 1.4426950408889634
D_MODEL = 2048
D_SSM = 2048
HEAD_DIM = 64
N_HEADS = D_SSM // HEAD_DIM
N_GROUPS = 4
HEADS_PER_GROUP = N_HEADS // N_GROUPS
GROUP_WIDTH = D_SSM // N_GROUPS
D_STATE = 128
CONV_WIDTH = 4
D_POOL = 2048
POOL_WINDOWS = (2, 4, 8, 16)
POOL_GROUP_DIM = D_POOL // len(POOL_WINDOWS)

V7X_LANES = 128
V7X_SUBLANES = 8
V7X_VMEM_LIMIT_BYTES = 60 * 1024 * 1024

W_PREP_COLS = 256
IN_PROJ_TILE = (1024, 1536)
IN_PROJ_SIDE_SLABS = 32
RESIDENT_ROW_TILE = 512
FFN_UP_TILE = (4096, 512)
FFN_UP_CHUNK_ROWS = 128

COL_XS = 0
COL_B = COL_XS + D_SSM
COL_C = COL_B + N_GROUPS * D_STATE
D_CONV = COL_C + N_GROUPS * D_STATE
COL_Z = D_CONV
COL_U = COL_Z + D_SSM
COL_DT = COL_U + D_POOL
D_PROJ = COL_DT + N_GROUPS * V7X_LANES

SSD_CHUNK = 256
N_EXPANDED = 3
N_REPLICAS = 3 * N_EXPANDED
HALO_ROWS = V7X_SUBLANES
POOL_HALO = 16


def _sigmoid(x):
    return 1.0 / (1.0 + jnp.exp(-x))


def _silu(x):
    return x * _sigmoid(x)


def _rms_normalize(x, w):
    ms = jnp.mean(x * x, axis=-1, keepdims=True)
    return x * lax.rsqrt(ms + NORM_EPS) * w


def _split3(x):
    hi = x.astype(BF16)
    r1 = x - hi.astype(F32)
    mid = r1.astype(BF16)
    lo = (r1 - mid.astype(F32)).astype(BF16)
    return hi, mid, lo


def _dot(a, b):
    return jnp.dot(a, b, preferred_element_type=F32)


def _dot_nt(a, b):
    return lax.dot_general(a, b, (((1,), (1,)), ((), ())), preferred_element_type=F32)


def _dot3_left(lhs_bf16, x):
    hi, mid, lo = _split3(x)
    return _dot(lhs_bf16, hi) + _dot(lhs_bf16, mid) + _dot(lhs_bf16, lo)


SRC_XBC = D_SSM
SRC_DT = SRC_XBC + D_SSM + 2 * N_GROUPS * D_STATE
SRC_U = SRC_DT + N_HEADS
D_IN_PROJ = SRC_U + D_POOL


def _w_in_prep_kernel(w_ref, o_ref):
    o_ref[COL_Z:COL_Z + D_SSM, :] = w_ref[:D_SSM, :].astype(BF16)
    o_ref[COL_U:COL_U + D_POOL, :] = w_ref[SRC_U:SRC_U + D_POOL, :].astype(BF16)
    o_ref[COL_XS:D_CONV, :] = w_ref[SRC_XBC:SRC_DT, :].astype(BF16)
    zeros = jnp.zeros((V7X_LANES - N_REPLICAS * HEADS_PER_GROUP, o_ref.shape[1]), F32)
    for g in range(N_GROUPS):
        row0 = COL_DT + g * V7X_LANES
        src0 = SRC_DT + g * HEADS_PER_GROUP
        heads = w_ref[src0:src0 + HEADS_PER_GROUP, :]
        block = jnp.concatenate([heads] * N_REPLICAS + [zeros], axis=0)
        o_ref[row0:row0 + V7X_LANES, :] = block.astype(BF16)


def _w_in_prep(w_t, *, tk):
    n, k = w_t.shape
    assert n == D_IN_PROJ
    return pl.pallas_call(
        _w_in_prep_kernel,
        out_shape=jax.ShapeDtypeStruct((D_PROJ, k), BF16),
        grid=(k // tk,),
        in_specs=[pl.BlockSpec((n, tk), lambda i: (0, i))],
        out_specs=pl.BlockSpec((D_PROJ, tk), lambda i: (0, i)),
        compiler_params=pltpu.CompilerParams(
            dimension_semantics=("parallel",),
            vmem_limit_bytes=V7X_VMEM_LIMIT_BYTES),
        name="w_in_prep",
    )(w_t)


def _side_cast_specs(weights, n_slabs, step_of):
    specs = []
    for w in weights:
        rows = w.shape[0] // n_slabs
        assert rows * n_slabs == w.shape[0] and rows % (2 * V7X_SUBLANES) == 0
        specs.append(pl.BlockSpec(
            (rows, w.shape[1]), lambda *g: (jnp.minimum(step_of(*g), n_slabs - 1), 0)))
    return specs


CONV_CHUNK = 256


def _conv_silu(r, halo, w_ref, b_ref, cols):
    ext = jnp.concatenate([halo, r], axis=0)
    acc = r * w_ref[CONV_WIDTH - 1:CONV_WIDTH, cols] + b_ref[:, cols]
    for back in range(1, CONV_WIDTH):
        tap = w_ref[CONV_WIDTH - 1 - back:CONV_WIDTH - back, cols]
        acc = acc + pltpu.roll(ext, back, axis=0)[HALO_ROWS:] * tap
    return _silu(acc)


def _in_proj_kernel(x_ref, nw_ref, w_ref, cw_ref, cb_ref, *rest, n_conv_blocks, n_side):
    side_in, rest = rest[:n_side], rest[n_side:]
    o_ref, side_out = rest[0], rest[1:1 + n_side]
    hn_ref, halo_ref, raw_ref = rest[1 + n_side:]
    i = pl.program_id(0)
    j = pl.program_id(1)
    tm, tn = o_ref.shape

    for src_ref, dst_ref in zip(side_in, side_out):
        dst_ref[...] = src_ref[...].astype(BF16)

    @pl.when(j == 0)
    def _():
        hn_ref[...] = _rms_normalize(x_ref[...], nw_ref[...]).astype(BF16)

    @pl.when((i == 0) & (j == 0))
    def _():
        halo_ref[...] = jnp.zeros_like(halo_ref)

    @pl.when(j < n_conv_blocks)
    def _():
        n_pieces = tn // CONV_CHUNK
        piece = lambda c: slice(c * CONV_CHUNK, (c + 1) * CONV_CHUNK)
        park = jnp.minimum(j, 0)
        raw_ref[park] = _dot_nt(hn_ref[...], w_ref[piece(0), :])
        for c in range(n_pieces):
            cols = piece(c)
            r = raw_ref[park]
            o_ref[:, cols] = _conv_silu(r, halo_ref[j, :, cols], cw_ref, cb_ref, cols)
            halo_ref[j, :, cols] = r[tm - HALO_ROWS:]
            if c + 1 < n_pieces:
                raw_ref[park] = _dot_nt(hn_ref[...], w_ref[piece(c + 1), :])

    @pl.when(j >= n_conv_blocks)
    def _():
        o_ref[...] = _dot_nt(hn_ref[...], w_ref[...])


def _in_proj(x, nw, w_t, conv_w, conv_b, side_weights, *, tm, tn, side_slabs):
    m, k = x.shape
    n = w_t.shape[0]
    assert D_CONV % tn == 0 and tn % CONV_CHUNK == 0
    n_conv_blocks = D_CONV // tn
    n_col_blocks = n // tn
    assert (m // tm) * n_col_blocks >= side_slabs
    conv_block = lambda i, j: (0, jnp.minimum(j, n_conv_blocks - 1))
    side_specs = _side_cast_specs(side_weights, side_slabs, lambda i, j: i * n_col_blocks + j)
    return pl.pallas_call(
        functools.partial(_in_proj_kernel, n_conv_blocks=n_conv_blocks, n_side=len(side_weights)),
        out_shape=(jax.ShapeDtypeStruct((m, n), F32),
                   *[jax.ShapeDtypeStruct(w.shape, BF16) for w in side_weights]),
        grid=(m // tm, n_col_blocks),
        in_specs=[
            pl.BlockSpec((tm, k), lambda i, j: (i, 0)),
            pl.BlockSpec((1, k), lambda i, j: (0, 0)),
            pl.BlockSpec((tn, k), lambda i, j: (j, 0)),
            pl.BlockSpec((CONV_WIDTH, tn), conv_block),
            pl.BlockSpec((1, tn), conv_block),
        ] + side_specs,
        out_specs=(pl.BlockSpec((tm, tn), lambda i, j: (i, j)), *side_specs),
        scratch_shapes=[pltpu.VMEM((tm, k), BF16),
                        pltpu.VMEM((n_conv_blocks, HALO_ROWS, tn), F32),
                        pltpu.VMEM((1, tm, CONV_CHUNK), F32)],
        compiler_params=pltpu.CompilerParams(
            dimension_semantics=("arbitrary", "arbitrary"),
            vmem_limit_bytes=V7X_VMEM_LIMIT_BYTES),
        name="in_proj",
    )(x, nw, w_t, conv_w, conv_b, *side_weights)


def _swiglu_kernel(hn_ref, wg_ref, wu_ref, *rest):
    n_side = (len(rest) - 1) // 2
    side_in, o_ref, side_out = rest[:n_side], rest[n_side], rest[n_side + 1:]
    for src_ref, dst_ref in zip(side_in, side_out):
        dst_ref[...] = src_ref[...].astype(BF16)
    for r in range(o_ref.shape[0] // FFN_UP_CHUNK_ROWS):
        rows = slice(r * FFN_UP_CHUNK_ROWS, (r + 1) * FFN_UP_CHUNK_ROWS)
        gate = _dot(hn_ref[rows, :], wg_ref[...])
        up = _dot(hn_ref[rows, :], wu_ref[...])
        o_ref[rows, :] = (_silu(gate) * up).astype(BF16)


def _swiglu(hn, wg, wu, side_weights, *, tm, tn):
    m, k = hn.shape
    n = wg.shape[1]
    n_col_tiles = n // tn
    side_specs = _side_cast_specs(side_weights, (m // tm) * n_col_tiles,
                                  lambda i, j: i * n_col_tiles + j)
    return pl.pallas_call(
        _swiglu_kernel,
        out_shape=(jax.ShapeDtypeStruct((m, n), BF16),
                   *[jax.ShapeDtypeStruct(w.shape, BF16) for w in side_weights]),
        grid=(m // tm, n_col_tiles),
        in_specs=[
            pl.BlockSpec((tm, k), lambda i, j: (i, 0)),
            pl.BlockSpec((k, tn), lambda i, j: (0, j)),
            pl.BlockSpec((k, tn), lambda i, j: (0, j)),
        ] + side_specs,
        out_specs=(pl.BlockSpec((tm, tn), lambda i, j: (i, j)), *side_specs),
        compiler_params=pltpu.CompilerParams(
            dimension_semantics=("arbitrary", "arbitrary"),
            vmem_limit_bytes=V7X_VMEM_LIMIT_BYTES),
        name="ffn_up",
    )(hn, wg, wu, *side_weights)


def _out_proj_kernel(ys_ref, yp_ref, w_ref, x_ref, nw_ref, h_ref, hn_ref):
    k = ys_ref.shape[1]
    h = x_ref[...] + _dot(ys_ref[...], w_ref[:k, :]) + _dot(yp_ref[...], w_ref[k:, :])
    h_ref[...] = h
    hn_ref[...] = _rms_normalize(h, nw_ref[...]).astype(BF16)


def _resident(shape):
    return pl.BlockSpec(shape, lambda i: (0,) * len(shape), pipeline_mode=pl.Buffered(1))


def _out_proj(ys, yp, w, x, nw, *, tm):
    m, k = ys.shape
    n = w.shape[1]
    row_spec = lambda width: pl.BlockSpec((tm, width), lambda i: (i, 0))
    return pl.pallas_call(
        _out_proj_kernel,
        out_shape=(jax.ShapeDtypeStruct((m, n), F32), jax.ShapeDtypeStruct((m, n), BF16)),
        grid=(m // tm,),
        in_specs=[row_spec(k), row_spec(k), _resident(w.shape), row_spec(n), _resident((1, n))],
        out_specs=(row_spec(n), row_spec(n)),
        compiler_params=pltpu.CompilerParams(
            dimension_semantics=("parallel",),
            vmem_limit_bytes=V7X_VMEM_LIMIT_BYTES),
        name="out_proj",
    )(ys, yp, w, x, nw)


def _down_kernel(a_ref, w_ref, h_ref, nw_ref, o_ref, *, final_norm):
    h = h_ref[...] + _dot(a_ref[...], w_ref[...])
    o_ref[...] = _rms_normalize(h, nw_ref[...]) if final_norm else h


def _down_proj(act, w, h, nw, *, tm, final_norm):
    m, k = act.shape
    n = w.shape[1]
    row_spec = lambda width: pl.BlockSpec((tm, width), lambda i: (i, 0))
    return pl.pallas_call(
        functools.partial(_down_kernel, final_norm=final_norm),
        out_shape=jax.ShapeDtypeStruct((m, n), F32),
        grid=(m // tm,),
        in_specs=[row_spec(k), _resident(w.shape), row_spec(n), _resident((1, n))],
        out_specs=row_spec(n),
        compiler_params=pltpu.CompilerParams(
            dimension_semantics=("parallel",),
            vmem_limit_bytes=V7X_VMEM_LIMIT_BYTES),
        name="ffn_down",
    )(act, w, h, nw)


def _pool_tile(i, u_refs, halo_refs, pw_ref, ps_ref, o_ref):
    tl = o_ref.shape[0]
    t = i * tl + lax.broadcasted_iota(jnp.int32, (tl, POOL_GROUP_DIM), 0)
    for g, window in enumerate(POOL_WINDOWS):
        cols = slice(g * POOL_GROUP_DIM, (g + 1) * POOL_GROUP_DIM)
        u = u_refs[g][...]
        halo = jnp.where(i == 0, 0.0, halo_refs[g][...])
        s = jnp.concatenate([halo, u], axis=0)
        shift = 1
        while shift < window:
            s = s + pltpu.roll(s, shift, axis=0)
            shift *= 2
        count = jnp.minimum(t + 1, window).astype(F32)
        pooled = s[POOL_HALO:] / count - u
        y = _dot(pooled.astype(BF16), pw_ref[g].astype(BF16)) * ps_ref[:, cols]
        o_ref[:, cols] = y.astype(BF16)


def _pool_specs(tl, step_of):
    gd = POOL_GROUP_DIM
    assert COL_U % gd == 0 and tl % POOL_HALO == 0
    halo_blocks_per_tile = tl // POOL_HALO
    n_groups = len(POOL_WINDOWS)
    u_specs = [pl.BlockSpec((tl, gd), lambda *g, k=k: (step_of(*g), COL_U // gd + k))
               for k in range(n_groups)]
    halo_specs = [
        pl.BlockSpec((POOL_HALO, gd),
                     lambda *g, k=k: (jnp.maximum(step_of(*g) * halo_blocks_per_tile - 1, 0),
                                      COL_U // gd + k))
        for k in range(n_groups)]
    return u_specs + halo_specs


assert N_REPLICAS * HEADS_PER_GROUP <= V7X_LANES


def _expansion_matrix():
    lane = jnp.arange(V7X_LANES)
    col = jnp.arange(N_EXPANDED * GROUP_WIDTH)
    replica, head = lane // HEADS_PER_GROUP, lane % HEADS_PER_GROUP
    quantity = replica // 3
    hit = ((quantity[:, None] == col[None, :] // GROUP_WIDTH)
           & (head[:, None] == (col[None, :] % GROUP_WIDTH) // HEAD_DIM)
           & (replica[:, None] < N_REPLICAS))
    return hit.astype(BF16)


def _ssd_group(z, xs, bm, cm, dt_raw, dtb, alog, dskip, nw, state_ref, expand, tril, causal):
    q = xs.shape[0]
    x = dt_raw + dtb
    dt = jnp.maximum(x, 0.0) + jnp.log1p(jnp.exp(-jnp.abs(x)))
    a = -jnp.exp(alog)
    a_cum = _dot3_left(tril, dt * a) * LOG2_E
    a_cum_t = a_cum.T
    a_last = a_cum[q - 1:q, :]

    replica = lax.broadcasted_iota(jnp.int32, (1, V7X_LANES), 1) // HEADS_PER_GROUP
    packed = jnp.where(replica < 3, dt,
                       jnp.where(replica < 6, jnp.exp2(a_cum), jnp.exp2(a_last - a_cum)))
    hi = packed.astype(BF16)
    r1 = packed - hi.astype(F32)
    r2 = r1 - r1.astype(BF16).astype(F32)
    level = replica % 3
    split = jnp.where(level == 0, packed, jnp.where(level == 1, r1, r2)).astype(BF16)
    expanded = _dot(split, expand)
    dt_b = expanded[:, :GROUP_WIDTH]
    decay_in_b = expanded[:, GROUP_WIDTH:2 * GROUP_WIDTH]
    decay_out_b = expanded[:, 2 * GROUP_WIDTH:]

    xdt = xs * dt_b
    bm_bf = bm.astype(BF16)
    cm_bf = cm.astype(BF16)
    cb = _dot_nt(cm_bf, bm_bf)

    xdt_bf = xdt.astype(BF16)
    lane = lax.broadcasted_iota(jnp.int32, (1, V7X_LANES), 1)
    keep = [jnp.where((lane // HEAD_DIM) == side, 1.0, 0.0).astype(BF16) for side in range(2)]
    pairs = []
    for pair in range(HEADS_PER_GROUP // 2):
        xdt_pair = xdt_bf[:, pair * V7X_LANES:(pair + 1) * V7X_LANES]
        acc = jnp.zeros((q, V7X_LANES), F32)
        for side in range(2):
            r = 2 * pair + side
            seg = a_cum[:, r:r + 1] - a_cum_t[r:r + 1, :]
            decay = jnp.exp2(jnp.where(causal, seg, -jnp.inf))
            m_r = (cb * decay).astype(BF16)
            acc = acc + _dot(m_r, xdt_pair * keep[side])
        pairs.append(acc)
    y_diag = jnp.concatenate(pairs, axis=-1)

    state = state_ref[...]
    y_off = _dot(cm_bf, state.astype(BF16)) * decay_in_b
    xw = (xdt * decay_out_b).astype(BF16)
    new_state = lax.dot_general(bm_bf, xw, (((0,), (0,)), ((), ())), preferred_element_type=F32)
    state_ref[...] = state * decay_in_b[q - 1:q, :] + new_state

    y = y_diag + y_off + dskip * xs
    g = y * _silu(z)
    return _rms_normalize(g, nw).astype(BF16)


def _mixers_kernel(*refs, groups, n_side):
    refs = list(refs)
    take = lambda n: [refs.pop(0) for _ in range(n)]
    n_pool = len(POOL_WINDOWS)
    z_refs = take(groups)
    xs_ref, b_ref, c_ref, dt_ref, dtb_ref, alog_ref, dskip_ref, nw_ref, expand_ref = take(9)
    u_refs, halo_refs = take(n_pool), take(n_pool)
    pw_ref, ps_ref = take(2)
    side_in = take(n_side)
    o_ref, yp_ref = take(2)
    side_out = take(n_side)
    state_ref, = refs
    q = SSD_CHUNK

    _pool_tile(pl.program_id(1), u_refs, halo_refs, pw_ref, ps_ref, yp_ref)

    for src_ref, dst_ref in zip(side_in, side_out):
        dst_ref[...] = src_ref[...].astype(BF16)

    @pl.when(pl.program_id(1) == 0)
    def _():
        state_ref[...] = jnp.zeros_like(state_ref)

    row_i = lax.broadcasted_iota(jnp.int32, (q, q), 0)
    col_i = lax.broadcasted_iota(jnp.int32, (q, q), 1)
    causal = row_i >= col_i
    tril = jnp.where(causal, 1.0, 0.0).astype(BF16)
    expand = expand_ref[...]
    for g in range(groups):
        wide = slice(g * GROUP_WIDTH, (g + 1) * GROUP_WIDTH)
        narrow = slice(g * D_STATE, (g + 1) * D_STATE)
        o_ref[:, wide] = _ssd_group(
            z_refs[g][...], xs_ref[:, wide], b_ref[:, narrow], c_ref[:, narrow], dt_ref[:, narrow],
            dtb_ref[:, narrow], alog_ref[:, narrow], dskip_ref[:, wide], nw_ref[:, wide],
            state_ref.at[g], expand, tril, causal)


def _mixers(proj, dtb_c, alog_c, dskip_full, norm_w, pool_w, pool_scale, side_weights, *, groups):
    m = proj.shape[0]
    q = SSD_CHUNK
    gw = groups * GROUP_WIDTH
    ds = groups * D_STATE
    assert V7X_LANES == D_STATE and N_GROUPS % groups == 0

    def col(start, width):
        assert start % width == 0
        return start // width

    row_spec = lambda width, block0: pl.BlockSpec((q, width), lambda g, c: (c, block0 + g))
    par_spec = lambda width: pl.BlockSpec((1, width), lambda g, c: (0, g))
    z_specs = [
        pl.BlockSpec((q, GROUP_WIDTH),
                     lambda g, c, k=k: (c, col(COL_Z, GROUP_WIDTH) + g * groups + k))
        for k in range(groups)]
    expand = _expansion_matrix()
    assert groups == N_GROUPS
    side_specs = _side_cast_specs(side_weights, m // q, lambda g, c: c)
    return pl.pallas_call(
        functools.partial(_mixers_kernel, groups=groups, n_side=len(side_weights)),
        out_shape=(jax.ShapeDtypeStruct((m, D_SSM), BF16), jax.ShapeDtypeStruct((m, D_POOL), BF16),
                   *[jax.ShapeDtypeStruct(w.shape, BF16) for w in side_weights]),
        grid=(N_GROUPS // groups, m // q),
        in_specs=z_specs + [
            row_spec(gw, col(COL_XS, gw)),
            row_spec(ds, col(COL_B, ds)),
            row_spec(ds, col(COL_C, ds)),
            row_spec(ds, col(COL_DT, ds)),
            par_spec(ds),
            par_spec(ds),
            par_spec(gw),
            par_spec(gw),
            pl.BlockSpec(expand.shape, lambda g, c: (0, 0)),
        ] + _pool_specs(q, lambda g, c: c) + [
            pl.BlockSpec(pool_w.shape, lambda g, c: (0, 0, 0)),
            pl.BlockSpec(pool_scale.shape, lambda g, c: (0, 0)),
        ] + side_specs,
        out_specs=(pl.BlockSpec((q, gw), lambda g, c: (c, g)),
                   pl.BlockSpec((q, D_POOL), lambda g, c: (c, 0)), *side_specs),
        scratch_shapes=[pltpu.VMEM((groups, D_STATE, GROUP_WIDTH), F32)],
        compiler_params=pltpu.CompilerParams(
            dimension_semantics=("parallel", "arbitrary"),
            vmem_limit_bytes=V7X_VMEM_LIMIT_BYTES),
        name="mixers",
    )(*([proj] * (groups + 4)), dtb_c, alog_c, dskip_full, norm_w, expand,
      *([proj] * (2 * len(POOL_WINDOWS))), pool_w, pool_scale, *side_weights)


def _pad_heads_to_lanes(v):
    v = jnp.tile(v.reshape(N_GROUPS, HEADS_PER_GROUP), (1, N_REPLICAS))
    v = jnp.pad(v, ((0, 0), (0, V7X_LANES - N_REPLICAS * HEADS_PER_GROUP)))
    return v.reshape(1, N_GROUPS * V7X_LANES)


def kernel(x, attn_norm_w, w_in, conv_w, conv_b, dt_bias, a_log, d_skip, ssd_norm_w, pool_w,
           pool_scale, w_out, ffn_norm_w, w_gate, w_up, w_down, final_norm_w):
    bsz, seqlen, d_model = x.shape
    depth = w_in.shape[0]
    h = x.reshape(bsz * seqlen, d_model)
    assert bsz == 1 and seqlen % 2048 == 0 and d_model == D_MODEL

    for i in range(depth):
        w_cat = _w_in_prep(w_in[i].T, tk=W_PREP_COLS)
        pool_w_rows = pool_w[i].reshape(-1, POOL_GROUP_DIM)
        proj, w_out_bf, pool_w_bf = _in_proj(
            h, attn_norm_w[i][None, :], w_cat, conv_w[i], conv_b[i][None, :],
            (w_out[i], pool_w_rows), tm=IN_PROJ_TILE[0], tn=IN_PROJ_TILE[1],
            side_slabs=IN_PROJ_SIDE_SLABS)

        y_ssd, y_pool, w_gate_bf, w_up_bf = _mixers(
            proj, _pad_heads_to_lanes(dt_bias[i]), _pad_heads_to_lanes(a_log[i]),
            jnp.repeat(d_skip[i], HEAD_DIM)[None, :], ssd_norm_w[i][None, :],
            pool_w_bf.reshape(pool_w[i].shape), pool_scale[i][None, :], (w_gate[i], w_up[i]),
            groups=N_GROUPS)

        h, hn = _out_proj(y_ssd, y_pool, w_out_bf, h, ffn_norm_w[i][None, :], tm=RESIDENT_ROW_TILE)

        act, w_down_bf = _swiglu(hn, w_gate_bf, w_up_bf, (w_down[i],),
                                 tm=FFN_UP_TILE[0], tn=FFN_UP_TILE[1])
        last = i == depth - 1
        h = _down_proj(act, w_down_bf, h, final_norm_w[None, :], tm=RESIDENT_ROW_TILE,
                       final_norm=last)

    return h.reshape(bsz, seqlen, d_model)
```
